```python
import jax
import jax.numpy as jnp
from jax import lax
import numpy as np

D_MODEL = 2048
BATCH = 16
SEQ = 256
DEPTH = 4
DEC_BATCH = 8
DEC_SEQ = 2048
PAST_LEN = 512

GRID_W = 64
HEAD_DIM = 128
NA_HEADS = 4
GQA_HEADS = 8
GQA_KV_HEADS = 2
GQA_GROUP = GQA_HEADS // GQA_KV_HEADS
RET_HEADS = 4
NA_WIN_R = 8
NA_WIN_C = 16
ROPE_THETA = 10000.0
RET_CHUNK = 128
Q_BLOCK = 128
N_EXPERTS = 16
EC_CAPACITY_FACTOR = 2
EXPERT_FF = 1024
NA_W = NA_HEADS * HEAD_DIM
GQA_QW = GQA_HEADS * HEAD_DIM
GQA_KVW = GQA_KV_HEADS * HEAD_DIM
RET_W = RET_HEADS * HEAD_DIM
MIX_W = NA_W + GQA_QW + RET_W
IN_SPLITS = (NA_W, NA_W, NA_W, GQA_QW, GQA_KVW, GQA_KVW, RET_W, RET_W, RET_W, RET_W)
IN_W = 3 * NA_W + GQA_QW + 2 * GQA_KVW + 4 * RET_W
DN_ALPHA = (2 * DEPTH) ** 0.25
DN_BETA = (8 * DEPTH) ** -0.25
LN_EPS = 1e-5
RMS_EPS = 1e-6
NEG_INF = -1e30

kernel_name = 'hybrid_na_gqa_retention_ec_moe_diffusion_step'


def _layer_norm(x, g=None, b=None):
    xf = x.astype(jnp.float32)
    mu = xf.mean(-1, keepdims=True)
    var = jnp.square(xf - mu).mean(-1, keepdims=True)
    y = (xf - mu) * lax.rsqrt(var + LN_EPS)
    if g is not None:
        y = y * g.astype(jnp.float32) + b.astype(jnp.float32)
    return y.astype(x.dtype)


def _rms_norm(x, g):
    xf = x.astype(jnp.float32)
    y = xf * lax.rsqrt(jnp.mean(jnp.square(xf), -1, keepdims=True) + RMS_EPS) * g.astype(jnp.float32)
    return y.astype(x.dtype)


def _heads(x, n):
    b, t, _ = x.shape
    return x.reshape(b, t, n, HEAD_DIM).transpose(0, 2, 1, 3)


def _merge(x):
    b, h, t, d = x.shape
    return x.transpose(0, 2, 1, 3).reshape(b, t, h * d)


def _project(h, w_in):
    z = h @ w_in
    offs = [int(o) for o in np.cumsum(IN_SPLITS)[:-1]]
    return jnp.split(z, offs, axis=-1)


def _rope_half(x, ang):
    cos = jnp.cos(ang).astype(x.dtype)
    sin = jnp.sin(ang).astype(x.dtype)
    x1, x2 = jnp.split(x, 2, axis=-1)
    return jnp.concatenate([x1 * cos - x2 * sin, x2 * cos + x1 * sin], axis=-1)


def _axial_rope(x):
    t = jnp.arange(x.shape[2])
    row = (t // GRID_W).astype(jnp.float32)
    col = (t % GRID_W).astype(jnp.float32)
    half = HEAD_DIM // 2
    inv = 1.0 / (ROPE_THETA ** (jnp.arange(0, half, 2, dtype=jnp.float32) / half))
    xr, xc = jnp.split(x, 2, axis=-1)
    return jnp.concatenate([_rope_half(xr, row[:, None] * inv), _rope_half(xc, col[:, None] * inv)], axis=-1)


def _block_attention(q, k, v):
    b, hk, g, t, d = q.shape
    n = t // Q_BLOCK
    qb = jnp.moveaxis(q.reshape(b, hk, g, n, Q_BLOCK, d), 3, 0)
    scale = HEAD_DIM ** -0.5

    def one(qi):
        s = jnp.einsum('bhgqd,bhsd->bhgqs', qi, k).astype(jnp.float32) * scale
        p = jax.nn.softmax(s, axis=-1).astype(v.dtype)
        return jnp.einsum('bhgqs,bhsd->bhgqd', p, v)

    o = lax.map(one, qb)
    return jnp.moveaxis(o, 0, 3).reshape(b, hk, g, t, d)


def _neighbourhood_attention(q, k, v, ctx_k, ctx_v, rpb):
    b, h, t, d = q.shape
    rows = t // GRID_W
    kr = min(NA_WIN_R, rows)
    r = jnp.arange(rows)
    row_start = jnp.clip(r - kr // 2, 0, rows - kr)
    row_idx = row_start[:, None] + jnp.arange(kr)[None, :]
    cq = jnp.arange(GRID_W)
    col_start = jnp.clip(cq - NA_WIN_C // 2, 0, GRID_W - NA_WIN_C)
    ck = jnp.arange(GRID_W)
    col_ok = (ck[None, :] >= col_start[:, None]) & (ck[None, :] < col_start[:, None] + NA_WIN_C)
    dr = row_idx - r[:, None] + (NA_WIN_R - 1)
    dc = jnp.clip(ck[None, :] - cq[:, None] + (NA_WIN_C - 1), 0, 2 * NA_WIN_C - 2)
    bias = rpb[:, dr[:, None, :, None], dc[None, :, None, :]].astype(jnp.float32)
    qg = q.reshape(b, h, rows, GRID_W, d)
    kg = k.reshape(b, h, rows, GRID_W, d)[:, :, row_idx]
    vg = v.reshape(b, h, rows, GRID_W, d)[:, :, row_idx]
    scale = HEAD_DIM ** -0.5
    s_loc = jnp.einsum('bhrqd,bhrkcd->bhrqkc', qg, kg).astype(jnp.float32) * scale + bias[None]
    s_loc = jnp.where(col_ok[:, None, :], s_loc, NEG_INF)
    s_ctx = jnp.einsum('bhrqd,bhsd->bhrqs', qg, ctx_k).astype(jnp.float32) * scale
    n_loc = kr * GRID_W
    s = jnp.concatenate([s_loc.reshape(b, h, rows, GRID_W, n_loc), s_ctx], axis=-1)
    p = jax.nn.softmax(s, axis=-1).astype(v.dtype)
    p_loc = p[..., :n_loc].reshape(b, h, rows, GRID_W, kr, GRID_W)
    p_ctx = p[..., n_loc:]
    o = jnp.einsum('bhrqkc,bhrkcd->bhrqd', p_loc, vg) + jnp.einsum('bhrqs,bhsd->bhrqd', p_ctx, ctx_v)
    return o.reshape(b, h, t, d)


def _retention_chunkwise(q, k, v, log_g, s0):
    b, h, t, d = q.shape
    n = t // RET_CHUNK

    def chunks(a):
        return jnp.moveaxis(a.reshape(b, h, n, RET_CHUNK, a.shape[-1]), 2, 0)

    i = jnp.arange(RET_CHUNK, dtype=jnp.float32)
    rel = i[:, None] - i[None, :]
    intra = jnp.where(rel >= 0, jnp.exp(log_g[:, None, None] * jnp.maximum(rel, 0.0)), 0.0)
    q_dec = jnp.exp(log_g[:, None] * (i + 1.0))[..., None]
    k_dec = jnp.exp(log_g[:, None] * (RET_CHUNK - 1.0 - i))[..., None]
    c_dec = jnp.exp(log_g * RET_CHUNK)[:, None, None]

    def step(s, blk):
        qb, kb, vb = blk
        att = jnp.einsum('bhid,bhjd->bhij', qb, kb) * intra
        o = jnp.einsum('bhij,bhje->bhie', att, vb) + jnp.einsum('bhid,bhde->bhie', qb * q_dec, s)
        s = c_dec * s + jnp.einsum('bhjd,bhje->bhde', kb * k_dec, vb)
        return s, o

    s, o = lax.scan(step, s0, (chunks(q), chunks(k), chunks(v)))
    return jnp.moveaxis(o, 0, 2).reshape(b, h, t, -1), s


def _bidir_retention(q, k, v, dec_f, dec_b, s0_f, s0_b):
    q = q.astype(jnp.float32)
    k = k.astype(jnp.float32) * HEAD_DIM ** -0.5
    v = v.astype(jnp.float32)
    lg_f = jax.nn.log_sigmoid(dec_f.astype(jnp.float32))
    lg_b = jax.nn.log_sigmoid(dec_b.astype(jnp.float32))
    o_f, s_f = _retention_chunkwise(q, k, v, lg_f, s0_f.astype(jnp.float32))
    o_b, s_b = _retention_chunkwise(jnp.flip(q, 2), jnp.flip(k, 2), jnp.flip(v, 2), lg_b, s0_b.astype(jnp.float32))
    return o_f + jnp.flip(o_b, 2), s_f, s_b


def _retention_out(o, g):
    mu = o.mean(-1, keepdims=True)
    var = jnp.square(o - mu).mean(-1, keepdims=True)
    o = (o - mu) * lax.rsqrt(var + LN_EPS)
    return _merge(o.astype(g.dtype)) * jax.nn.silu(g)


def _mixer_context(h, w_in, qn_g, kn_g, dec_f, dec_b):
    na_q, na_k, na_v, gq, gk, gv, rq, rk, rv, rg = _project(h, w_in)
    b, s, _ = h.shape
    na_k = _heads(na_k, NA_HEADS)
    na_v = _heads(na_v, NA_HEADS)
    na_o = _block_attention(_heads(na_q, NA_HEADS)[:, :, None], na_k, na_v)[:, :, 0]
    gk = _rms_norm(_heads(gk, GQA_KV_HEADS), kn_g)
    gv = _heads(gv, GQA_KV_HEADS)
    gq = _rms_norm(_heads(gq, GQA_HEADS), qn_g).reshape(b, GQA_KV_HEADS, GQA_GROUP, s, HEAD_DIM)
    g_o = _block_attention(gq, gk, gv).reshape(b, GQA_HEADS, s, HEAD_DIM)
    zero = jnp.zeros((b, RET_HEADS, HEAD_DIM, HEAD_DIM), jnp.float32)
    r_o, s_f, s_b = _bidir_retention(_heads(rq, RET_HEADS), _heads(rk, RET_HEADS), _heads(rv, RET_HEADS),
                                     dec_f, dec_b, zero, zero)
    out = jnp.concatenate([_merge(na_o), _merge(g_o), _retention_out(r_o, rg)], axis=-1)
    return out, (na_k, na_v, gk, gv, s_f.astype(h.dtype), s_b.astype(h.dtype))


def _mixer_latent(h, c_na_k, c_na_v, c_gk, c_gv, c_sf, c_sb, w_in, qn_g, kn_g, rpb, dec_f, dec_b):
    na_q, na_k, na_v, gq, gk, gv, rq, rk, rv, rg = _project(h, w_in)
    b, t, _ = h.shape
    na_o = _neighbourhood_attention(_heads(na_q, NA_HEADS), _heads(na_k, NA_HEADS), _heads(na_v, NA_HEADS),
                                    c_na_k, c_na_v, rpb)
    gk = _axial_rope(_rms_norm(_heads(gk, GQA_KV_HEADS), kn_g))
    gq = _axial_rope(_rms_norm(_heads(gq, GQA_HEADS), qn_g)).reshape(b, GQA_KV_HEADS, GQA_GROUP, t, HEAD_DIM)
    keys = jnp.concatenate([c_gk, gk], axis=2)
    vals = jnp.concatenate([c_gv, _heads(gv, GQA_KV_HEADS)], axis=2)
    g_o = _block_attention(gq, keys, vals).reshape(b, GQA_HEADS, t, HEAD_DIM)
    r_o, _, _ = _bidir_retention(_heads(rq, RET_HEADS), _heads(rk, RET_HEADS), _heads(rv, RET_HEADS),
                                 dec_f, dec_b, c_sf, c_sb)
    return jnp.concatenate([_merge(na_o), _merge(g_o), _retention_out(r_o, rg)], axis=-1)


def _modulate(x, shift, scale):
    return _layer_norm(x) * (1 + scale) + shift


def _adaln(cvec, w_mod, b_mod):
    mod = jax.nn.silu(cvec) @ w_mod + b_mod
    return [m[:, None, :] for m in jnp.split(mod, 6, axis=-1)]


def _post_norm(x, y, g, b):
    return _layer_norm(DN_ALPHA * x + y, g, b)


def _moe(h, w_router, w_gate, w_up, w_down):
    b, t, d = h.shape
    cap = EC_CAPACITY_FACTOR * t // N_EXPERTS
    aff = jax.nn.softmax((h @ w_router).astype(jnp.float32), axis=-1)
    gate, idx = lax.top_k(jnp.swapaxes(aff, 1, 2), cap)
    xs = jax.vmap(lambda hb, ib: hb[ib])(h, idx)
    a = jnp.einsum('becd,edf->becf', xs, w_gate)
    u = jnp.einsum('becd,edf->becf', xs, w_up)
    y = jnp.einsum('becf,efd->becd', jax.nn.silu(a) * u, w_down) * gate[..., None].astype(h.dtype)
    return jax.vmap(lambda ib, yb: jnp.zeros((t, d), yb.dtype).at[ib.reshape(-1)].add(yb.reshape(-1, d)))(idx, y)


def setup_inputs(seed: int = 0) -> dict:
    key = jax.random.key(seed)
    ks = jax.random.split(key, 32)
    f32 = jnp.float32

    def nrm(k, shape, s=1.0):
        return jax.random.normal(k, shape, f32) * s

    gam = 1.0 - 2.0 ** (-5.0 - jnp.arange(RET_HEADS, dtype=f32))
    dec_logit = jnp.log(gam) - jnp.log1p(-gam)
    return {
        'x_prompt': nrm(ks[0], (BATCH, SEQ, D_MODEL)),
        'x_sample': nrm(ks[1], (DEC_BATCH, DEC_SEQ, D_MODEL)),
        'cache_na_k': nrm(ks[2], (DEC_BATCH, DEPTH, NA_HEADS, PAST_LEN, HEAD_DIM)),
        'cache_na_v': nrm(ks[3], (DEC_BATCH, DEPTH, NA_HEADS, PAST_LEN, HEAD_DIM)),
        'cache_gqa_k': nrm(ks[4], (DEC_BATCH, DEPTH, GQA_KV_HEADS, PAST_LEN, HEAD_DIM)),
        'cache_gqa_v': nrm(ks[5], (DEC_BATCH, DEPTH, GQA_KV_HEADS, PAST_LEN, HEAD_DIM)),
        'state_ret_fwd': nrm(ks[6], (DEC_BATCH, DEPTH, RET_HEADS, HEAD_DIM, HEAD_DIM)),
        'state_ret_bwd': nrm(ks[7], (DEC_BATCH, DEPTH, RET_HEADS, HEAD_DIM, HEAD_DIM)),
        'c': nrm(ks[8], (DEC_BATCH, D_MODEL)),
        'c_ctx': nrm(ks[9], (D_MODEL,)),
        'w_in': nrm(ks[10], (DEPTH, D_MODEL, IN_W), D_MODEL ** -0.5),
        'w_out': nrm(ks[11], (DEPTH, MIX_W, D_MODEL), MIX_W ** -0.5 * DN_BETA),
        'w_mod': nrm(ks[12], (DEPTH, D_MODEL, 6 * D_MODEL), 0.5 * D_MODEL ** -0.5),
        'b_mod': nrm(ks[13], (DEPTH, 6 * D_MODEL), 0.01),
        'ln1_g': 1.0 + nrm(ks[14], (DEPTH, D_MODEL), 0.01),
        'ln1_b': nrm(ks[15], (DEPTH, D_MODEL), 0.01),
        'ln2_g': 1.0 + nrm(ks[16], (DEPTH, D_MODEL), 0.01),
        'ln2_b': nrm(ks[17], (DEPTH, D_MODEL), 0.01),
        'q_norm_g': 1.0 + nrm(ks[18], (DEPTH, HEAD_DIM), 0.01),
        'k_norm_g': 1.0 + nrm(ks[19], (DEPTH, HEAD_DIM), 0.01),
        'na_rpb': nrm(ks[20], (DEPTH, NA_HEADS, 2 * NA_WIN_R - 1, 2 * NA_WIN_C - 1), 0.1),
        'ret_decay_fwd': dec_logit[None] + nrm(ks[21], (DEPTH, RET_HEADS), 0.1),
        'ret_decay_bwd': dec_logit[None] + nrm(ks[22], (DEPTH, RET_HEADS), 0.1),
        'w_router': nrm(ks[23], (DEPTH, D_MODEL, N_EXPERTS), D_MODEL ** -0.5),
        'w_gate': nrm(ks[24], (DEPTH, N_EXPERTS, D_MODEL, EXPERT_FF), D_MODEL ** -0.5),
        'w_up': nrm(ks[25], (DEPTH, N_EXPERTS, D_MODEL, EXPERT_FF), D_MODEL ** -0.5),
        'w_down': nrm(ks[26], (DEPTH, N_EXPERTS, EXPERT_FF, D_MODEL), EXPERT_FF ** -0.5 * DN_BETA),
    }


def reference(x_prompt, x_sample, cache_na_k, cache_na_v, cache_gqa_k, cache_gqa_v, state_ret_fwd, state_ret_bwd,
              c, c_ctx, w_in, w_out, w_mod, b_mod, ln1_g, ln1_b, ln2_g, ln2_b, q_norm_g, k_norm_g, na_rpb,
              ret_decay_fwd, ret_decay_bwd, w_router, w_gate, w_up, w_down):
    xp = x_prompt
    xs = x_sample
    na_ks, na_vs, gqa_ks, gqa_vs, ret_fs, ret_bs = [], [], [], [], [], []
    for l in range(DEPTH):
        sh1, sc1, g1, sh2, sc2, g2 = _adaln(c_ctx[None], w_mod[l], b_mod[l])
        mix, st = _mixer_context(_modulate(xp, sh1, sc1), w_in[l], q_norm_g[l], k_norm_g[l],
                                 ret_decay_fwd[l], ret_decay_bwd[l])
        xp = _post_norm(xp, g1 * (mix @ w_out[l]), ln1_g[l], ln1_b[l])
        moe = _moe(_modulate(xp, sh2, sc2), w_router[l], w_gate[l], w_up[l], w_down[l])
        xp = _post_norm(xp, g2 * moe, ln2_g[l], ln2_b[l])
        na_ks.append(st[0]); na_vs.append(st[1]); gqa_ks.append(st[2]); gqa_vs.append(st[3])
        ret_fs.append(st[4]); ret_bs.append(st[5])
        sh1, sc1, g1, sh2, sc2, g2 = _adaln(c, w_mod[l], b_mod[l])
        mix = _mixer_latent(_modulate(xs, sh1, sc1), cache_na_k[:, l], cache_na_v[:, l], cache_gqa_k[:, l],
                            cache_gqa_v[:, l], state_ret_fwd[:, l], state_ret_bwd[:, l], w_in[l],
                            q_norm_g[l], k_norm_g[l], na_rpb[l], ret_decay_fwd[l], ret_decay_bwd[l])
        xs = _post_norm(xs, g1 * (mix @ w_out[l]), ln1_g[l], ln1_b[l])
        moe = _moe(_modulate(xs, sh2, sc2), w_router[l], w_gate[l], w_up[l], w_down[l])
        xs = _post_norm(xs, g2 * moe, ln2_g[l], ln2_b[l])
    new_na_k = jnp.stack(na_ks, axis=1)
    new_na_v = jnp.stack(na_vs, axis=1)
    new_gqa_k = jnp.stack(gqa_ks, axis=1)
    new_gqa_v = jnp.stack(gqa_vs, axis=1)
    new_ret_fwd = jnp.stack(ret_fs, axis=1)
    new_ret_bwd = jnp.stack(ret_bs, axis=1)
    return (xp, xs, new_na_k, new_na_v, new_gqa_k, new_gqa_v, new_ret_fwd, new_ret_bwd)
```

```python
import functools
import math

import numpy as np
import jax
import jax.numpy as jnp
from jax import lax
from jax.experimental import pallas as pl
from jax.experimental.pallas import tpu as pltpu

F32 = jnp.float32
BF16 = jnp.bfloat16

HEAD_DIM = 128
GRID_W = 64
NA_WIN_R = 8
NA_WIN_C = 16
ROPE_THETA = 10000.0
EC_CAPACITY_FACTOR = 2
LN_EPS = 1e-5
RMS_EPS = 1e-6
NEG_INF = -1e30
ATTN_SCALE = HEAD_DIM ** -0.5
MOD_ROWS = 16
VMEM_LIMIT = 56 * 1024 * 1024

_NT = (((1,), (1,)), ((), ()))
_NN = (((1,), (0,)), ((), ()))


def _params(n_grid, vmem=VMEM_LIMIT):
    return pltpu.CompilerParams(dimension_semantics=("arbitrary",) * n_grid, vmem_limit_bytes=vmem)


def _dot(a, b, dims=_NN):
    return lax.dot_general(a, b, dims, preferred_element_type=F32)


def _split_bf16(x):
    hi = x.astype(BF16)
    lo = (x - hi.astype(F32)).astype(BF16)
    return hi, lo


def _dot3(a, b, dims=_NN):
    ah, al = _split_bf16(a)
    bh, bl = _split_bf16(b)
    return _dot(ah, bh, dims) + _dot(al, bh, dims) + _dot(ah, bl, dims)


def _silu(x):
    return x / (1.0 + jnp.exp(-x))


def _ln(x):
    mu = jnp.mean(x, axis=-1, keepdims=True)
    xc = x - mu
    var = jnp.mean(xc * xc, axis=-1, keepdims=True)
    return xc * lax.rsqrt(var + LN_EPS)


def _rms(x, g):
    return x * lax.rsqrt(jnp.mean(x * x, axis=-1, keepdims=True) + RMS_EPS) * g


def _rope(x, cos, sin_signed):
    lane = lax.broadcasted_iota(jnp.int32, x.shape, 1) & (HEAD_DIM // 2 - 1)
    partner = jnp.where(lane < HEAD_DIM // 4,
                        pltpu.roll(x, HEAD_DIM - HEAD_DIM // 4, 1),
                        pltpu.roll(x, HEAD_DIM // 4, 1))
    return x * cos + partner * sin_signed


def _softmax_pv(scores, values):
    m = functools.reduce(jnp.maximum, [jnp.max(s, axis=-1, keepdims=True) for s in scores])
    ps = [jnp.exp(s - m) for s in scores]
    denom = functools.reduce(lambda a, b: a + b, [jnp.sum(p, axis=-1, keepdims=True) for p in ps])
    o = functools.reduce(lambda a, b: a + b, [_dot(p.astype(BF16), v) for p, v in zip(ps, values)])
    return o / denom


def _mod_kernel(c_ref, w_ref, b_ref, o_ref):
    a = _silu(c_ref[...])
    o_ref[...] = _dot3(a, w_ref[...]) + b_ref[...]


def _modulation(cond, w_mod, b_mod):
    depth, d, n = w_mod.shape
    tn = 768
    return pl.pallas_call(
        _mod_kernel,
        grid=(depth, n // tn),
        in_specs=[pl.BlockSpec((MOD_ROWS, d), lambda l, j: (0, 0)),
                  pl.BlockSpec((None, d, tn), lambda l, j: (l, 0, j)),
                  pl.BlockSpec((None, 1, tn), lambda l, j: (l, 0, j))],
        out_specs=pl.BlockSpec((None, MOD_ROWS, tn), lambda l, j: (l, 0, j)),
        out_shape=jax.ShapeDtypeStruct((depth, MOD_ROWS, n), F32),
        compiler_params=_params(2),
        name="adaln_mod",
    )(cond, w_mod, b_mod.reshape(depth, 1, n))


def _inproj_kernel(x_ref, sh_ref, sc_ref, w_ref, z_ref):
    h = _ln(x_ref[...]) * (1.0 + sc_ref[...]) + sh_ref[...]
    z_ref[...] = _dot(h.astype(BF16), w_ref[...]).astype(z_ref.dtype)


def _mod_row(block, blocks_per_request, per_request):
    return 1 + block // blocks_per_request if per_request else 0


def _in_projection(x, mod3, w_bf16, tokens_per_batch, per_request, out_dtype):
    rows, d = x.shape
    n = w_bf16.shape[1]
    tm = min(512, tokens_per_batch)
    tn = 2560
    per = tokens_per_batch // tm

    def mod_spec(chunk):
        return pl.BlockSpec((None, 1, d), lambda j, i: (_mod_row(i, per, per_request) * 6 + chunk, 0, 0))

    return pl.pallas_call(
        _inproj_kernel,
        grid=(n // tn, rows // tm),
        in_specs=[pl.BlockSpec((tm, d), lambda j, i: (i, 0)),
                  mod_spec(0), mod_spec(1),
                  pl.BlockSpec((d, tn), lambda j, i: (0, j))],
        out_specs=pl.BlockSpec((tm, tn), lambda j, i: (i, j)),
        out_shape=jax.ShapeDtypeStruct((rows, n), out_dtype),
        compiler_params=_params(2),
        name="ln_inproj",
    )(x, mod3, mod3, w_bf16)


def _ctx_attn_kernel(*refs, group, normed):
    if normed:
        q_ref, k_ref, v_ref, qg_ref, kg_ref, o_ref, kn_ref = refs
    else:
        q_ref, k_ref, v_ref, o_ref = refs
    k = k_ref[...]
    if normed:
        k = _rms(k, kg_ref[...])
        kn_ref[...] = k
    kb = k.astype(BF16)
    vb = v_ref[...].astype(BF16)
    for g in range(group):
        q = q_ref[:, g * HEAD_DIM:(g + 1) * HEAD_DIM]
        if normed:
            q = _rms(q, qg_ref[...])
        s = _dot(q.astype(BF16), kb, _NT) * ATTN_SCALE
        o_ref[:, g * HEAD_DIM:(g + 1) * HEAD_DIM] = _softmax_pv([s], [vb]).astype(o_ref.dtype)


def _ctx_attention(z, batch, seq, heads, group, q_col, k_col, v_col, gains=None):
    normed = gains is not None
    qw = group * HEAD_DIM
    in_specs = [pl.BlockSpec((seq, qw), lambda b, h: (b, q_col // group + h)),
                pl.BlockSpec((seq, HEAD_DIM), lambda b, h: (b, k_col + h)),
                pl.BlockSpec((seq, HEAD_DIM), lambda b, h: (b, v_col + h))]
    args = [z, z, z]
    out_specs = [pl.BlockSpec((seq, qw), lambda b, h: (b, h))]
    out_shape = [jax.ShapeDtypeStruct((batch * seq, heads * qw), BF16)]
    if normed:
        in_specs += [pl.BlockSpec((1, HEAD_DIM), lambda b, h: (0, 0))] * 2
        args += list(gains)
        out_specs.append(pl.BlockSpec((None, None, seq, HEAD_DIM), lambda b, h: (b, h, 0, 0)))
        out_shape.append(jax.ShapeDtypeStruct((batch, heads, seq, HEAD_DIM), F32))
    return pl.pallas_call(
        functools.partial(_ctx_attn_kernel, group=group, normed=normed),
        grid=(batch, heads),
        in_specs=in_specs, out_specs=out_specs, out_shape=out_shape,
        compiler_params=_params(2),
        name="ctx_gqa" if normed else "ctx_na",
    )(*args)


def _log_sigmoid(x):
    return -jnp.log1p(jnp.exp(-x))


def _retention_kernel(*refs, seq, tq, with_state_in, with_state_out):
    q_ref, k_ref, v_ref, g_ref, df_ref, db_ref = refs[:6]
    refs = refs[6:]
    if with_state_in:
        sf_ref, sb_ref = refs[:2]
        refs = refs[2:]
    o_ref = refs[0]
    lgf = _log_sigmoid(df_ref[...])
    lgb = _log_sigmoid(db_ref[...])
    t0 = pl.program_id(2) * tq
    q = q_ref[...]
    kb = k_ref[...].astype(BF16)
    vb = v_ref[...].astype(BF16)
    i = t0 + lax.broadcasted_iota(jnp.int32, (tq, seq), 0)
    j = lax.broadcasted_iota(jnp.int32, (tq, seq), 1)
    dist = (i - j).astype(F32)
    decay = jnp.where(dist == 0.0, 2.0, jnp.exp(jnp.where(dist > 0.0, lgf * dist, -lgb * dist)))
    att = _dot(q.astype(BF16), kb, _NT) * (decay * ATTN_SCALE)
    o = _dot(att.astype(BF16), vb)
    if with_state_in:
        t = (t0 + lax.broadcasted_iota(jnp.int32, (tq, 1), 0)).astype(F32)
        qf = q.astype(F32)
        o += _dot((qf * jnp.exp(lgf * (t + 1.0))).astype(BF16), sf_ref[...].astype(BF16))
        o += _dot((qf * jnp.exp(lgb * (seq - t))).astype(BF16), sb_ref[...].astype(BF16))
    o_ref[...] = (_ln(o) * _silu(g_ref[...].astype(F32))).astype(o_ref.dtype)
    if with_state_out:
        nsf_ref, nsb_ref = refs[1:3]
        tj = lax.broadcasted_iota(jnp.int32, (seq, 1), 0).astype(F32)
        kf = k_ref[...].astype(F32) * ATTN_SCALE
        kdf = (kf * jnp.exp(lgf * (seq - 1.0 - tj))).T.astype(BF16)
        kdb = (kf * jnp.exp(lgb * tj)).T.astype(BF16)
        nsf_ref[...] = _dot(kdf, vb)
        nsb_ref[...] = _dot(kdb, vb)


def _retention(z, batch, seq, heads, col0, dec_f, dec_b, states=None, state_out=False):
    tq = min(512, seq)
    nq = seq // tq
    assert not (state_out and nq != 1)
    in_specs = [pl.BlockSpec((tq, HEAD_DIM), lambda b, h, i: (b * nq + i, col0 + h)),
                pl.BlockSpec((seq, HEAD_DIM), lambda b, h, i: (b, col0 + heads + h)),
                pl.BlockSpec((seq, HEAD_DIM), lambda b, h, i: (b, col0 + 2 * heads + h)),
                pl.BlockSpec((tq, HEAD_DIM), lambda b, h, i: (b * nq + i, col0 + 3 * heads + h)),
                pl.BlockSpec((None, 1, 1), lambda b, h, i: (h, 0, 0)),
                pl.BlockSpec((None, 1, 1), lambda b, h, i: (h, 0, 0))]
    args = [z, z, z, z, dec_f.reshape(heads, 1, 1), dec_b.reshape(heads, 1, 1)]
    if states is not None:
        (sf, sb), layer = states
        spec = pl.BlockSpec((None, None, None, HEAD_DIM, HEAD_DIM), lambda b, h, i: (b, layer, h, 0, 0))
        in_specs += [spec, spec]
        args += [sf, sb]
    out_specs = [pl.BlockSpec((tq, HEAD_DIM), lambda b, h, i: (b * nq + i, h))]
    out_shape = [jax.ShapeDtypeStruct((batch * seq, heads * HEAD_DIM), BF16)]
    if state_out:
        spec = pl.BlockSpec((None, None, HEAD_DIM, HEAD_DIM), lambda b, h, i: (b, h, 0, 0))
        out_specs += [spec, spec]
        out_shape += [jax.ShapeDtypeStruct((batch, heads, HEAD_DIM, HEAD_DIM), F32)] * 2
    return pl.pallas_call(
        functools.partial(_retention_kernel, seq=seq, tq=tq, with_state_in=states is not None,
                          with_state_out=state_out),
        grid=(batch, heads, nq),
        in_specs=in_specs, out_specs=out_specs, out_shape=out_shape,
        compiler_params=_params(3),
        name="retention",
    )(*args)


def _na_kernel(q_ref, k_ref, v_ref, ck_ref, cv_ref, bias_ref, o_ref, *, rows, kr):
    ck = ck_ref[...].astype(BF16)
    cv = cv_ref[...].astype(BF16)
    nloc = kr * GRID_W

    def body(r, carry):
        rs = jnp.clip(r - kr // 2, 0, rows - kr)
        q = q_ref[pl.ds(pl.multiple_of(r * GRID_W, GRID_W), GRID_W), :]
        start = pl.multiple_of(rs * GRID_W, GRID_W)
        kl = k_ref[pl.ds(start, nloc), :]
        vl = v_ref[pl.ds(start, nloc), :]
        s_loc = _dot(q, kl, _NT) * ATTN_SCALE + bias_ref[r - rs]
        s_ctx = _dot(q, ck, _NT) * ATTN_SCALE
        o = _softmax_pv([s_loc, s_ctx], [vl, cv])
        o_ref[pl.ds(pl.multiple_of(r * GRID_W, GRID_W), GRID_W), :] = o.astype(o_ref.dtype)
        return carry

    lax.fori_loop(0, rows, body, 0)


def _na_bias(rpb, kr):
    case = np.arange(kr)
    j = np.arange(kr)
    dr = j[None, :] + (NA_WIN_R - 1) - case[:, None]
    cq = np.arange(GRID_W)
    ck = np.arange(GRID_W)
    col_start = np.clip(cq - NA_WIN_C // 2, 0, GRID_W - NA_WIN_C)
    col_ok = (ck[None, :] >= col_start[:, None]) & (ck[None, :] < col_start[:, None] + NA_WIN_C)
    dc = np.clip(ck[None, :] - cq[:, None] + (NA_WIN_C - 1), 0, 2 * NA_WIN_C - 2)
    bias = rpb[:, dr[:, None, :, None], dc[None, :, None, :]].astype(F32)
    bias = jnp.where(col_ok[None, None, :, None, :], bias, NEG_INF)
    return bias.reshape(rpb.shape[0], kr, GRID_W, kr * GRID_W)


def _na_attention(z, batch, seq, heads, cache_k, cache_v, layer, bias):
    rows = seq // GRID_W
    kr = min(NA_WIN_R, rows)
    past = cache_k.shape[3]
    cspec = pl.BlockSpec((None, None, None, past, HEAD_DIM), lambda b, h: (b, layer, h, 0, 0))
    return pl.pallas_call(
        functools.partial(_na_kernel, rows=rows, kr=kr),
        grid=(batch, heads),
        in_specs=[pl.BlockSpec((seq, HEAD_DIM), lambda b, h: (b, h)),
                  pl.BlockSpec((seq, HEAD_DIM), lambda b, h: (b, heads + h)),
                  pl.BlockSpec((seq, HEAD_DIM), lambda b, h: (b, 2 * heads + h)),
                  cspec, cspec,
                  pl.BlockSpec((None, kr, GRID_W, kr * GRID_W), lambda b, h: (h, 0, 0, 0))],
        out_specs=pl.BlockSpec((seq, HEAD_DIM), lambda b, h: (b, h)),
        out_shape=jax.ShapeDtypeStruct((batch * seq, heads * HEAD_DIM), BF16),
        compiler_params=_params(2),
        name="latent_na",
    )(z, z, z, cache_k, cache_v, bias)


def _gqa_kernel(q_ref, k_ref, v_ref, ck_ref, cv_ref, qg_ref, kg_ref, cos_ref, sin_ref, o_ref, kf_ref, vf_ref,
                *, group, past, seq, tq):
    i = pl.program_id(2)

    @pl.when(i == 0)
    def _():
        kf_ref[0:past, :] = ck_ref[...].astype(BF16)
        vf_ref[0:past, :] = cv_ref[...].astype(BF16)
        k = _rope(_rms(k_ref[...].astype(F32), kg_ref[...]), cos_ref[...], sin_ref[...])
        kf_ref[past:past + seq, :] = k.astype(BF16)
        vf_ref[past:past + seq, :] = v_ref[...]

    t0 = pl.multiple_of(i * tq, tq)
    cos = cos_ref[pl.ds(t0, tq), :]
    sin = sin_ref[pl.ds(t0, tq), :]
    kf = kf_ref[...]
    vf = vf_ref[...]
    for g in range(group):
        q = q_ref[:, g * HEAD_DIM:(g + 1) * HEAD_DIM].astype(F32)
        q = _rope(_rms(q, qg_ref[...]), cos, sin)
        s = _dot(q.astype(BF16), kf, _NT) * ATTN_SCALE
        o_ref[:, g * HEAD_DIM:(g + 1) * HEAD_DIM] = _softmax_pv([s], [vf]).astype(o_ref.dtype)


def _rope_tables(seq):
    t = np.arange(seq)
    half = HEAD_DIM // 2
    inv = 1.0 / (ROPE_THETA ** (np.arange(0, half, 2, dtype=np.float32) / half))
    row = (t // GRID_W).astype(np.float32)
    col = (t % GRID_W).astype(np.float32)
    ang = jnp.concatenate([jnp.asarray(row[:, None] * inv)] * 2 + [jnp.asarray(col[:, None] * inv)] * 2, axis=-1)
    sign = np.where((np.arange(HEAD_DIM) % half) < half // 2, -1.0, 1.0).astype(np.float32)
    return jnp.cos(ang), jnp.sin(ang) * sign


def _gqa_attention(z, batch, seq, kv_heads, group, q_col, k_col, v_col, cache_k, cache_v, layer, gains, tables):
    past = cache_k.shape[3]
    tq = min(256, seq)
    nq = seq // tq
    qw = group * HEAD_DIM
    cspec = pl.BlockSpec((None, None, None, past, HEAD_DIM), lambda b, h, i: (b, layer, h, 0, 0))
    gspec = pl.BlockSpec((1, HEAD_DIM), lambda b, h, i: (0, 0))
    tspec = pl.BlockSpec((seq, HEAD_DIM), lambda b, h, i: (0, 0))
    return pl.pallas_call(
        functools.partial(_gqa_kernel, group=group, past=past, seq=seq, tq=tq),
        grid=(batch, kv_heads, nq),
        in_specs=[pl.BlockSpec((tq, qw), lambda b, h, i: (b * nq + i, q_col // group + h)),
                  pl.BlockSpec((seq, HEAD_DIM), lambda b, h, i: (b, k_col + h)),
                  pl.BlockSpec((seq, HEAD_DIM), lambda b, h, i: (b, v_col + h)),
                  cspec, cspec, gspec, gspec, tspec, tspec],
        out_specs=pl.BlockSpec((tq, qw), lambda b, h, i: (b * nq + i, h)),
        out_shape=jax.ShapeDtypeStruct((batch * seq, kv_heads * qw), BF16),
        scratch_shapes=[pltpu.VMEM((past + seq, HEAD_DIM), BF16), pltpu.VMEM((past + seq, HEAD_DIM), BF16)],
        compiler_params=_params(3),
        name="latent_gqa",
    )(z, z, z, cache_k, cache_v, gains[0], gains[1], tables[0], tables[1])


def _outproj_kernel(x_ref, na_ref, gq_ref, rt_ref, w_ref, g1_ref, sh2_ref, sc2_ref, lg_ref, lb_ref, wr_ref,
                    x1_ref, h2_ref, lt_ref, *, alpha):
    n_na = na_ref.shape[1]
    n_gq = gq_ref.shape[1]
    y = _dot(na_ref[...], w_ref[0:n_na, :])
    y += _dot(gq_ref[...], w_ref[n_na:n_na + n_gq, :])
    y += _dot(rt_ref[...], w_ref[n_na + n_gq:, :])
    x1 = _ln(alpha * x_ref[...] + g1_ref[...] * y) * lg_ref[...] + lb_ref[...]
    x1_ref[...] = x1
    h2 = _ln(x1) * (1.0 + sc2_ref[...]) + sh2_ref[...]
    h2_ref[...] = h2.astype(BF16)
    lt_ref[...] = _dot3(wr_ref[...], h2, _NT)


def _out_projection(x, na_o, gq_o, rt_o, w_bf16, mod3, ln_g, ln_b, wr_t, tokens_per_batch, per_request, alpha):
    rows, d = x.shape
    tm = 256
    per = tokens_per_batch // tm
    n_exp = wr_t.shape[0]

    def mod_spec(chunk):
        return pl.BlockSpec((None, 1, d), lambda i: (_mod_row(i, per, per_request) * 6 + chunk, 0, 0))

    def row_spec(width):
        return pl.BlockSpec((tm, width), lambda i: (i, 0))

    vec = pl.BlockSpec((1, d), lambda i: (0, 0))
    return pl.pallas_call(
        functools.partial(_outproj_kernel, alpha=alpha),
        grid=(rows // tm,),
        in_specs=[row_spec(d), row_spec(na_o.shape[1]), row_spec(gq_o.shape[1]), row_spec(rt_o.shape[1]),
                  pl.BlockSpec(w_bf16.shape, lambda i: (0, 0)),
                  mod_spec(2), mod_spec(3), mod_spec(4), vec, vec,
                  pl.BlockSpec(wr_t.shape, lambda i: (0, 0))],
        out_specs=[row_spec(d), row_spec(d), pl.BlockSpec((n_exp, tm), lambda i: (0, i))],
        out_shape=[jax.ShapeDtypeStruct((rows, d), F32), jax.ShapeDtypeStruct((rows, d), BF16),
                   jax.ShapeDtypeStruct((n_exp, rows), F32)],
        compiler_params=_params(1),
        name="outproj_postnorm_router",
    )(x, na_o, gq_o, rt_o, w_bf16, mod3, mod3, mod3, ln_g, ln_b, wr_t)


def _topk_kernel(lt_ref, slot_ref, gate_ref, *, cap, seq):
    logits = lt_ref[...]
    n_exp = logits.shape[0]
    m = jnp.max(logits, axis=0, keepdims=True)
    ex = jnp.exp(logits - m)
    aff = ex / jnp.sum(ex, axis=0, keepdims=True)
    bits = lax.bitcast_convert_type(aff, jnp.int32)

    def count(mask):
        return jnp.sum(jnp.where(mask, 1.0, 0.0), axis=1, keepdims=True)

    def value_step(it, thr):
        cand = thr | jnp.left_shift(jnp.int32(1), 30 - it)
        return jnp.where(count(bits >= cand) >= cap, cand, thr)

    thr = lax.fori_loop(0, 31, value_step, jnp.zeros((n_exp, 1), jnp.int32))
    above = bits > thr
    tied = bits == thr
    need = cap - count(above)
    tok = lax.broadcasted_iota(jnp.int32, (n_exp, seq), 1)
    n_bits = int(seq - 1).bit_length()

    def index_step(it, bound):
        cand = bound | jnp.left_shift(jnp.int32(1), n_bits - 1 - it)
        return jnp.where(count(tied & (tok < cand)) < need, cand, bound)

    bound = lax.fori_loop(0, n_bits, index_step, jnp.zeros((n_exp, 1), jnp.int32))
    sel = above | (tied & (tok <= bound))
    self = jnp.where(sel, 1.0, 0.0)
    lanes = 128
    upper = jnp.where(lax.broadcasted_iota(jnp.int32, (lanes, lanes), 0)
                      < lax.broadcasted_iota(jnp.int32, (lanes, lanes), 1), 1.0, 0.0).astype(BF16)
    running = jnp.zeros((n_exp, 1), F32)
    for blk in range(seq // lanes):
        sl = slice(blk * lanes, (blk + 1) * lanes)
        chunk = self[:, sl]
        pos = _dot(chunk.astype(BF16), upper) + running
        slot_ref[:, sl] = jnp.where(chunk > 0.0, pos.astype(jnp.int32), -1)
        running = running + jnp.sum(chunk, axis=1, keepdims=True)
    gate_ref[...] = aff


def _route(logits_t, batch, seq, cap):
    n_exp = logits_t.shape[0]
    spec = pl.BlockSpec((None, n_exp, seq), lambda b: (b, 0, 0))
    return pl.pallas_call(
        functools.partial(_topk_kernel, cap=cap, seq=seq),
        grid=(batch,),
        in_specs=[pl.BlockSpec((n_exp, seq), lambda b: (0, b))],
        out_specs=[spec, spec],
        out_shape=[jax.ShapeDtypeStruct((batch, n_exp, seq), jnp.int32),
                   jax.ShapeDtypeStruct((batch, n_exp, seq), F32)],
        compiler_params=_params(1),
        name="route_topk",
    )(logits_t)


def _gather_kernel(*refs, n_inner, cap, aliased):
    if aliased:
        refs = refs[2:]
    slot_ref, gate_ref, h_ref, xs_ref, gs_ref = refs
    h = h_ref[...]
    seq = h.shape[0]
    row = lax.broadcasted_iota(jnp.int32, (cap, seq), 0)
    for e in range(n_inner):
        onehot = slot_ref[e:e + 1, :] == row
        xs_ref[e] = _dot(jnp.where(onehot, 1.0, 0.0).astype(BF16), h).astype(BF16)
        gs_ref[e] = jnp.sum(jnp.where(onehot, gate_ref[e:e + 1, :], 0.0), axis=1, keepdims=True)


def _gather(slot, gate, h, batch, seq, cap, n_inner, row_blk0, total_rows, prev=None):
    n_exp = slot.shape[1]
    d = h.shape[1]
    n_outer = n_exp // n_inner
    slot4 = slot.reshape(batch, n_outer, n_inner, seq)
    gate4 = gate.reshape(batch, n_outer, n_inner, seq)
    sspec = pl.BlockSpec((None, None, n_inner, seq), lambda b, e: (b, e, 0, 0))
    in_specs = [sspec, sspec, pl.BlockSpec((seq, d), lambda b, e: (b, 0))]
    args = [slot4, gate4, h]
    aliases = {}
    if prev is not None:
        in_specs = [pl.BlockSpec(memory_space=pl.ANY)] * 2 + in_specs
        args = list(prev) + args
        aliases = {0: 0, 1: 1}
    return pl.pallas_call(
        functools.partial(_gather_kernel, n_inner=n_inner, cap=cap, aliased=prev is not None),
        grid=(batch, n_outer),
        in_specs=in_specs,
        out_specs=[pl.BlockSpec((n_inner, cap, d), lambda b, e: (e, row_blk0 + b, 0)),
                   pl.BlockSpec((n_inner, cap, 1), lambda b, e: (e, row_blk0 + b, 0))],
        out_shape=[jax.ShapeDtypeStruct((n_exp, total_rows, d), BF16),
                   jax.ShapeDtypeStruct((n_exp, total_rows, 1), F32)],
        input_output_aliases=aliases,
        compiler_params=_params(2),
        name="moe_gather",
    )(*args)


def _ffn_kernel(xs_ref, gs_ref, wg_ref, wu_ref, wd_ref, y_ref):
    x = xs_ref[...]
    a = _dot(x, wg_ref[...])
    u = _dot(x, wu_ref[...])
    y = _dot((_silu(a) * u).astype(BF16), wd_ref[...])
    y_ref[...] = (y * gs_ref[...]).astype(y_ref.dtype)


def _expert_ffn(xs, gs, wg, wu, wd):
    n_exp, rows, d = xs.shape
    ff = wg.shape[2]
    tm = math.gcd(rows, 512)
    return pl.pallas_call(
        _ffn_kernel,
        grid=(n_exp, rows // tm),
        in_specs=[pl.BlockSpec((None, tm, d), lambda e, i: (e, i, 0)),
                  pl.BlockSpec((None, tm, 1), lambda e, i: (e, i, 0)),
                  pl.BlockSpec((None, d, ff), lambda e, i: (e, 0, 0)),
                  pl.BlockSpec((None, d, ff), lambda e, i: (e, 0, 0)),
                  pl.BlockSpec((None, ff, d), lambda e, i: (e, 0, 0))],
        out_specs=pl.BlockSpec((None, tm, d), lambda e, i: (e, i, 0)),
        out_shape=jax.ShapeDtypeStruct((n_exp, rows, d), BF16),
        compiler_params=_params(2),
        name="moe_ffn",
    )(xs, gs, wg, wu, wd)


def _combine_kernel(x_ref, y_ref, slot_ref, g2_ref, lg_ref, lb_ref, o_ref, *, alpha, cap):
    n_exp = y_ref.shape[0]
    tt = x_ref.shape[0]
    col = lax.broadcasted_iota(jnp.int32, (tt, cap), 1)
    slot = slot_ref[...]
    acc = jnp.zeros(x_ref.shape, F32)
    for e in range(n_exp):
        onehot = jnp.where(slot[:, e:e + 1] == col, 1.0, 0.0).astype(BF16)
        acc += _dot(onehot, y_ref[e])
    o_ref[...] = _ln(alpha * x_ref[...] + g2_ref[...] * acc) * lg_ref[...] + lb_ref[...]


def _combine(x1, y, slot_t, mod3, ln_g, ln_b, batch, seq, cap, row_blk0, per_request, alpha):
    rows, d = x1.shape
    n_exp = y.shape[0]
    tt = min(256, seq)
    nt = seq // tt
    vec = pl.BlockSpec((1, d), lambda b, i: (0, 0))
    return pl.pallas_call(
        functools.partial(_combine_kernel, alpha=alpha, cap=cap),
        grid=(batch, nt),
        in_specs=[pl.BlockSpec((tt, d), lambda b, i: (b * nt + i, 0)),
                  pl.BlockSpec((n_exp, cap, d), lambda b, i: (0, row_blk0 + b, 0)),
                  pl.BlockSpec((None, tt, n_exp), lambda b, i: (b, i, 0)),
                  pl.BlockSpec((None, 1, d), lambda b, i: (_mod_row(b, 1, per_request) * 6 + 5, 0, 0)),
                  vec, vec],
        out_specs=pl.BlockSpec((tt, d), lambda b, i: (b * nt + i, 0)),
        out_shape=jax.ShapeDtypeStruct((rows, d), F32),
        compiler_params=_params(2),
        name="moe_combine_postnorm",
    )(x1, y, slot_t, mod3, ln_g, ln_b)


def kernel(x_prompt, x_sample, cache_na_k, cache_na_v, cache_gqa_k, cache_gqa_v, state_ret_fwd, state_ret_bwd,
           c, c_ctx, w_in, w_out, w_mod, b_mod, ln1_g, ln1_b, ln2_g, ln2_b, q_norm_g, k_norm_g, na_rpb,
           ret_decay_fwd, ret_decay_bwd, w_router, w_gate, w_up, w_down):
    batch, seq, d = x_prompt.shape
    dbatch, dseq, _ = x_sample.shape
    depth = w_in.shape[0]
    na_heads = cache_na_k.shape[2]
    kv_heads = cache_gqa_k.shape[2]
    ret_heads = state_ret_fwd.shape[2]
    n_exp = w_router.shape[2]
    hd = HEAD_DIM
    gqa_heads = (w_in.shape[2] // hd - 3 * na_heads - 2 * kv_heads - 4 * ret_heads)
    group = gqa_heads // kv_heads
    alpha = float((2 * depth) ** 0.25)
    c_na = 0
    c_gq = 3 * na_heads
    c_gk = c_gq + gqa_heads
    c_gv = c_gk + kv_heads
    c_rt = c_gv + kv_heads
    cap_c = EC_CAPACITY_FACTOR * seq // n_exp
    cap_l = EC_CAPACITY_FACTOR * dseq // n_exp
    assert dbatch + 1 <= MOD_ROWS and (batch * cap_c) % cap_l == 0
    lat_blk0 = batch * cap_c // cap_l
    rows_per_expert = batch * cap_c + dbatch * cap_l

    cond = jnp.zeros((MOD_ROWS, d), F32).at[0].set(c_ctx).at[1:1 + dbatch].set(c)
    mod_all = _modulation(cond, w_mod, b_mod)
    tables = _rope_tables(dseq)

    xp = x_prompt.reshape(batch * seq, d)
    xs = x_sample.reshape(dbatch * dseq, d)
    outs = [[] for _ in range(6)]
    for l in range(depth):
        mod3 = mod_all[l].reshape(MOD_ROWS * 6, 1, d)
        w_in_b = w_in[l].astype(BF16)
        w_out_b = w_out[l].astype(BF16)
        wr_t = w_router[l].T
        wg_b, wu_b, wd_b = w_gate[l].astype(BF16), w_up[l].astype(BF16), w_down[l].astype(BF16)
        gains = (q_norm_g[l].reshape(1, hd), k_norm_g[l].reshape(1, hd))
        l1g, l1b = ln1_g[l].reshape(1, d), ln1_b[l].reshape(1, d)
        l2g, l2b = ln2_g[l].reshape(1, d), ln2_b[l].reshape(1, d)

        zc = _in_projection(xp, mod3, w_in_b, seq, False, F32)
        (na_c,) = _ctx_attention(zc, batch, seq, na_heads, 1, c_na, c_na + na_heads, c_na + 2 * na_heads)
        gq_c, gk_n = _ctx_attention(zc, batch, seq, kv_heads, group, c_gq, c_gk, c_gv, gains)
        rt_c, s_f, s_b = _retention(zc, batch, seq, ret_heads, c_rt, ret_decay_fwd[l], ret_decay_bwd[l],
                                    state_out=True)

        def heads_of(col, n):
            return zc[:, col * hd:(col + n) * hd].reshape(batch, seq, n, hd).transpose(0, 2, 1, 3)

        outs[0].append(heads_of(c_na + na_heads, na_heads))
        outs[1].append(heads_of(c_na + 2 * na_heads, na_heads))
        outs[2].append(gk_n)
        outs[3].append(heads_of(c_gv, kv_heads))
        outs[4].append(s_f)
        outs[5].append(s_b)

        zl = _in_projection(xs, mod3, w_in_b, dseq, True, BF16)
        bias = _na_bias(na_rpb[l], min(NA_WIN_R, dseq // GRID_W))
        (na_l,) = [_na_attention(zl, dbatch, dseq, na_heads, cache_na_k, cache_na_v, l, bias)]
        gq_l = _gqa_attention(zl, dbatch, dseq, kv_heads, group, c_gq, c_gk, c_gv, cache_gqa_k, cache_gqa_v, l,
                              gains, tables)
        (rt_l,) = _retention(zl, dbatch, dseq, ret_heads, c_rt, ret_decay_fwd[l], ret_decay_bwd[l],
                             states=((state_ret_fwd, state_ret_bwd), l))

        xp1, hp, lt_c = _out_projection(xp, na_c, gq_c, rt_c, w_out_b, mod3, l1g, l1b, wr_t, seq, False, alpha)
        xs1, hs, lt_l = _out_projection(xs, na_l, gq_l, rt_l, w_out_b, mod3, l1g, l1b, wr_t, dseq, True, alpha)

        slot_c, gate_c = _route(lt_c, batch, seq, cap_c)
        slot_l, gate_l = _route(lt_l, dbatch, dseq, cap_l)
        gathered = _gather(slot_l, gate_l, hs, dbatch, dseq, cap_l, 2, lat_blk0, rows_per_expert)
        gathered = _gather(slot_c, gate_c, hp, batch, seq, cap_c, n_exp, 0, rows_per_expert, prev=gathered)
        y = _expert_ffn(gathered[0], gathered[1], wg_b, wu_b, wd_b)
        xp = _combine(xp1, y, slot_c.transpose(0, 2, 1), mod3, l2g, l2b, batch, seq, cap_c, 0, False, alpha)
        xs = _combine(xs1, y, slot_l.transpose(0, 2, 1), mod3, l2g, l2b, dbatch, dseq, cap_l, lat_blk0, True, alpha)

    stacked = [jnp.stack(o, axis=1) for o in outs]
    return (xp.reshape(batch, seq, d), xs.reshape(dbatch, dseq, d), *stacked)
```

```python
import functools
import math

import numpy as np
import jax
import jax.numpy as jnp
from jax import lax
from jax.experimental import pallas as pl
from jax.experimental.pallas import tpu as pltpu

F32 = jnp.float32
BF16 = jnp.bfloat16

HEAD_DIM = 128
GRID_W = 64
NA_WIN_R = 8
NA_WIN_C = 16
ROPE_THETA = 10000.0
EC_CAPACITY_FACTOR = 2
LN_EPS = 1e-5
RMS_EPS = 1e-6
NEG_INF = -1e30
ATTN_SCALE = HEAD_DIM ** -0.5
LOG2E = 1.4426950408889634
NA_BLOCK_ROWS = 4
MOD_ROWS = 16
VMEM_LIMIT = 56 * 1024 * 1024

_NT = (((1,), (1,)), ((), ()))
_NN = (((1,), (0,)), ((), ()))


def _params(n_grid, vmem=VMEM_LIMIT):
    return pltpu.CompilerParams(dimension_semantics=("arbitrary",) * n_grid, vmem_limit_bytes=vmem)


def _dot(a, b, dims=_NN):
    return lax.dot_general(a, b, dims, preferred_element_type=F32)


def _split_bf16(x):
    hi = x.astype(BF16)
    lo = (x - hi.astype(F32)).astype(BF16)
    return hi, lo


def _dot3(a, b, dims=_NN):
    ah, al = _split_bf16(a)
    bh, bl = _split_bf16(b)
    return _dot(ah, bh, dims) + _dot(al, bh, dims) + _dot(ah, bl, dims)


def _silu(x):
    return x / (1.0 + jnp.exp(-x))


def _ln(x):
    mu = jnp.mean(x, axis=-1, keepdims=True)
    xc = x - mu
    var = jnp.mean(xc * xc, axis=-1, keepdims=True)
    return xc * lax.rsqrt(var + LN_EPS)


def _rms(x, g):
    return x * lax.rsqrt(jnp.mean(x * x, axis=-1, keepdims=True) + RMS_EPS) * g


def _rope(x, cos, sin_signed):
    lane = lax.broadcasted_iota(jnp.int32, x.shape, 1) & (HEAD_DIM // 2 - 1)
    partner = jnp.where(lane < HEAD_DIM // 4,
                        pltpu.roll(x, HEAD_DIM - HEAD_DIM // 4, 1),
                        pltpu.roll(x, HEAD_DIM // 4, 1))
    return x * cos + partner * sin_signed


def _softmax_pv(scores, values):
    m = functools.reduce(jnp.maximum, [jnp.max(s, axis=-1, keepdims=True) for s in scores])
    ps = [jnp.exp(s - m) for s in scores]
    denom = functools.reduce(lambda a, b: a + b, [jnp.sum(p, axis=-1, keepdims=True) for p in ps])
    o = functools.reduce(lambda a, b: a + b, [_dot(p.astype(BF16), v) for p, v in zip(ps, values)])
    return o / denom


def _mod_kernel(c_ref, w_ref, b_ref, o_ref):
    a = _silu(c_ref[...])
    o_ref[...] = _dot3(a, w_ref[...]) + b_ref[...]


def _modulation(cond, w_mod, b_mod):
    depth, d, n = w_mod.shape
    tn = 768
    return pl.pallas_call(
        _mod_kernel,
        grid=(depth, n // tn),
        in_specs=[pl.BlockSpec((MOD_ROWS, d), lambda l, j: (0, 0)),
                  pl.BlockSpec((None, d, tn), lambda l, j: (l, 0, j)),
                  pl.BlockSpec((None, 1, tn), lambda l, j: (l, 0, j))],
        out_specs=pl.BlockSpec((None, MOD_ROWS, tn), lambda l, j: (l, 0, j)),
        out_shape=jax.ShapeDtypeStruct((depth, MOD_ROWS, n), F32),
        compiler_params=_params(2),
        name="adaln_mod",
    )(cond, w_mod, b_mod.reshape(depth, 1, n))


def _inproj_kernel(x_ref, sh_ref, sc_ref, w_ref, z_ref):
    h = _ln(x_ref[...]) * (1.0 + sc_ref[...]) + sh_ref[...]
    z_ref[...] = _dot(h.astype(BF16), w_ref[...]).astype(z_ref.dtype)


def _mod_row(block, blocks_per_request, per_request):
    return 1 + block // blocks_per_request if per_request else 0


def _in_projection(x, mod3, w_bf16, tokens_per_batch, per_request, out_dtype):
    rows, d = x.shape
    n = w_bf16.shape[1]
    tm = min(512, tokens_per_batch)
    tn = 2560
    per = tokens_per_batch // tm

    def mod_spec(chunk):
        return pl.BlockSpec((None, 1, d), lambda j, i: (_mod_row(i, per, per_request) * 6 + chunk, 0, 0))

    return pl.pallas_call(
        _inproj_kernel,
        grid=(n // tn, rows // tm),
        in_specs=[pl.BlockSpec((tm, d), lambda j, i: (i, 0)),
                  mod_spec(0), mod_spec(1),
                  pl.BlockSpec((d, tn), lambda j, i: (0, j))],
        out_specs=pl.BlockSpec((tm, tn), lambda j, i: (i, j)),
        out_shape=jax.ShapeDtypeStruct((rows, n), out_dtype),
        compiler_params=_params(2),
        name="ln_inproj",
    )(x, mod3, mod3, w_bf16)


def _ctx_attn_kernel(*refs, group, normed):
    if normed:
        q_ref, k_ref, v_ref, qg_ref, kg_ref, o_ref, kn_ref = refs
    else:
        q_ref, k_ref, v_ref, o_ref = refs
    k = k_ref[...]
    if normed:
        k = _rms(k, kg_ref[...])
        kn_ref[...] = k
    kb = k.astype(BF16)
    vb = v_ref[...].astype(BF16)
    for g in range(group):
        q = q_ref[:, g * HEAD_DIM:(g + 1) * HEAD_DIM]
        if normed:
            q = _rms(q, qg_ref[...])
        s = _dot(q.astype(BF16), kb, _NT) * ATTN_SCALE
        o_ref[:, g * HEAD_DIM:(g + 1) * HEAD_DIM] = _softmax_pv([s], [vb]).astype(o_ref.dtype)


def _ctx_attention(z, batch, seq, heads, group, q_col, k_col, v_col, gains=None):
    normed = gains is not None
    qw = group * HEAD_DIM
    in_specs = [pl.BlockSpec((seq, qw), lambda b, h: (b, q_col // group + h)),
                pl.BlockSpec((seq, HEAD_DIM), lambda b, h: (b, k_col + h)),
                pl.BlockSpec((seq, HEAD_DIM), lambda b, h: (b, v_col + h))]
    args = [z, z, z]
    out_specs = [pl.BlockSpec((seq, qw), lambda b, h: (b, h))]
    out_shape = [jax.ShapeDtypeStruct((batch * seq, heads * qw), BF16)]
    if normed:
        in_specs += [pl.BlockSpec((1, HEAD_DIM), lambda b, h: (0, 0))] * 2
        args += list(gains)
        out_specs.append(pl.BlockSpec((None, None, seq, HEAD_DIM), lambda b, h: (b, h, 0, 0)))
        out_shape.append(jax.ShapeDtypeStruct((batch, heads, seq, HEAD_DIM), F32))
    return pl.pallas_call(
        functools.partial(_ctx_attn_kernel, group=group, normed=normed),
        grid=(batch, heads),
        in_specs=in_specs, out_specs=out_specs, out_shape=out_shape,
        compiler_params=_params(2),
        name="ctx_gqa" if normed else "ctx_na",
    )(*args)


def _log_sigmoid(x):
    return -jnp.log1p(jnp.exp(-x))


def _retention_kernel(*refs, seq, tq, with_state_in, with_state_out):
    q_ref, k_ref, v_ref, g_ref, df_ref, db_ref = refs[:6]
    refs = refs[6:]
    if with_state_in:
        sf_ref, sb_ref = refs[:2]
        refs = refs[2:]
    o_ref = refs[0]
    decay_ref = refs[-1]
    lgf = _log_sigmoid(df_ref[...])
    lgb = _log_sigmoid(db_ref[...])
    t0 = pl.program_id(1) * tq

    @pl.when(pl.program_id(2) == 0)
    def _():
        i = t0 + lax.broadcasted_iota(jnp.int32, (tq, seq), 0)
        j = lax.broadcasted_iota(jnp.int32, (tq, seq), 1)
        dist = (i - j).astype(F32)
        decay = jnp.where(dist == 0.0, 2.0, jnp.exp(jnp.where(dist > 0.0, lgf * dist, -lgb * dist)))
        decay_ref[...] = decay * ATTN_SCALE

    q = q_ref[...]
    kb = k_ref[...].astype(BF16)
    vb = v_ref[...].astype(BF16)
    att = _dot(q.astype(BF16), kb, _NT) * decay_ref[...]
    o = _dot(att.astype(BF16), vb)
    if with_state_in:
        t = (t0 + lax.broadcasted_iota(jnp.int32, (tq, 1), 0)).astype(F32)
        qf = q.astype(F32)
        o += _dot((qf * jnp.exp(lgf * (t + 1.0))).astype(BF16), sf_ref[...].astype(BF16))
        o += _dot((qf * jnp.exp(lgb * (seq - t))).astype(BF16), sb_ref[...].astype(BF16))
    o_ref[...] = (_ln(o) * _silu(g_ref[...].astype(F32))).astype(o_ref.dtype)
    if with_state_out:
        nsf_ref, nsb_ref = refs[1:3]
        tj = lax.broadcasted_iota(jnp.int32, (seq, 1), 0).astype(F32)
        kf = k_ref[...].astype(F32) * ATTN_SCALE
        kdf = (kf * jnp.exp(lgf * (seq - 1.0 - tj))).T.astype(BF16)
        kdb = (kf * jnp.exp(lgb * tj)).T.astype(BF16)
        nsf_ref[...] = _dot(kdf, vb)
        nsb_ref[...] = _dot(kdb, vb)


def _retention(z, batch, seq, heads, col0, dec_f, dec_b, states=None, state_out=False):
    tq = min(512, seq)
    nq = seq // tq
    assert not (state_out and nq != 1)
    in_specs = [pl.BlockSpec((tq, HEAD_DIM), lambda h, i, b: (b * nq + i, col0 + h)),
                pl.BlockSpec((seq, HEAD_DIM), lambda h, i, b: (b, col0 + heads + h)),
                pl.BlockSpec((seq, HEAD_DIM), lambda h, i, b: (b, col0 + 2 * heads + h)),
                pl.BlockSpec((tq, HEAD_DIM), lambda h, i, b: (b * nq + i, col0 + 3 * heads + h)),
                pl.BlockSpec((None, 1, 1), lambda h, i, b: (h, 0, 0)),
                pl.BlockSpec((None, 1, 1), lambda h, i, b: (h, 0, 0))]
    args = [z, z, z, z, dec_f.reshape(heads, 1, 1), dec_b.reshape(heads, 1, 1)]
    if states is not None:
        (sf, sb), layer = states
        spec = pl.BlockSpec((None, None, None, HEAD_DIM, HEAD_DIM), lambda h, i, b: (b, layer, h, 0, 0))
        in_specs += [spec, spec]
        args += [sf, sb]
    out_specs = [pl.BlockSpec((tq, HEAD_DIM), lambda h, i, b: (b * nq + i, h))]
    out_shape = [jax.ShapeDtypeStruct((batch * seq, heads * HEAD_DIM), BF16)]
    if state_out:
        spec = pl.BlockSpec((None, None, HEAD_DIM, HEAD_DIM), lambda h, i, b: (b, h, 0, 0))
        out_specs += [spec, spec]
        out_shape += [jax.ShapeDtypeStruct((batch, heads, HEAD_DIM, HEAD_DIM), F32)] * 2
    return pl.pallas_call(
        functools.partial(_retention_kernel, seq=seq, tq=tq, with_state_in=states is not None,
                          with_state_out=state_out),
        grid=(heads, nq, batch),
        in_specs=in_specs, out_specs=out_specs, out_shape=out_shape,
        scratch_shapes=[pltpu.VMEM((tq, seq), F32)],
        compiler_params=_params(3),
        name="retention",
    )(*args)


def _with_ones(v):
    return jnp.concatenate([v, jnp.ones(v.shape, v.dtype)], axis=1)


def _exp2_pv(scores, values_with_ones):
    m = functools.reduce(jnp.maximum, [jnp.max(s, axis=-1, keepdims=True) for s in scores])
    r = functools.reduce(lambda a, b: a + b,
                         [_dot(jnp.exp2(s - m).astype(BF16), v) for s, v in zip(scores, values_with_ones)])
    return r[:, :HEAD_DIM] / r[:, HEAD_DIM:HEAD_DIM + 1]


def _na_plan(rows):
    kr = min(NA_WIN_R, rows)
    nq = min(NA_BLOCK_ROWS, rows)
    nk = min(nq + kr - 1, rows)
    assert rows % nq == 0
    blocks, cases = [], []
    for r0 in range(0, rows, nq):
        start = int(np.clip(r0 - kr // 2, 0, rows - nk))
        win = [int(np.clip(r - kr // 2, 0, rows - kr)) for r in range(r0, r0 + nq)]
        assert all(start <= w and w + kr <= start + nk for w in win)
        key = (start - r0,) + tuple(w - start for w in win)
        if key not in cases:
            cases.append(key)
        blocks.append((r0, start, cases.index(key)))
    return kr, nq, nk, blocks, cases


def _na_kernel(q_ref, k_ref, v_ref, ck_ref, cv_ref, bias_ref, o_ref, vx_ref, cvx_ref, *, blocks, nq, nk):
    vx_ref[...] = _with_ones(v_ref[...])
    cvx_ref[...] = _with_ones(cv_ref[...].astype(BF16))
    ck = ck_ref[...].astype(BF16)
    c = ATTN_SCALE * LOG2E
    for r0, start, case in blocks:
        qs = slice(r0 * GRID_W, (r0 + nq) * GRID_W)
        ks = slice(start * GRID_W, (start + nk) * GRID_W)
        q = q_ref[qs, :]
        s_loc = _dot(q, k_ref[ks, :], _NT) * c + bias_ref[case]
        s_ctx = _dot(q, ck, _NT) * c
        o_ref[qs, :] = _exp2_pv([s_loc, s_ctx], [vx_ref[ks, :], cvx_ref[...]]).astype(o_ref.dtype)


def _na_bias(rpb, rows):
    kr, nq, nk, _, cases = _na_plan(rows)
    dr = np.zeros((len(cases), nq, nk), np.int32)
    row_ok = np.zeros((len(cases), nq, nk), bool)
    for c, key in enumerate(cases):
        rel, offs = key[0], key[1:]
        for ri in range(nq):
            for ju in range(nk):
                row_ok[c, ri, ju] = 0 <= ju - offs[ri] < kr
                dr[c, ri, ju] = np.clip(rel + ju - ri + NA_WIN_R - 1, 0, 2 * NA_WIN_R - 2)
    cq = np.arange(GRID_W)
    ck = np.arange(GRID_W)
    col_start = np.clip(cq - NA_WIN_C // 2, 0, GRID_W - NA_WIN_C)
    col_ok = (ck[None, :] >= col_start[:, None]) & (ck[None, :] < col_start[:, None] + NA_WIN_C)
    dc = np.clip(ck[None, :] - cq[:, None] + (NA_WIN_C - 1), 0, 2 * NA_WIN_C - 2)
    pick_col = (dc.reshape(-1)[None, :] == np.arange(2 * NA_WIN_C - 1)[:, None]).astype(np.float32)
    n_l, n_h = rpb.shape[:2]
    t = jnp.take(rpb.astype(F32), dr.reshape(-1), axis=2)
    t = jnp.einsum("lhxb,bn->lhxn", t, pick_col, precision=lax.Precision.HIGHEST)
    t = t.reshape(n_l, n_h, len(cases), nq, nk, GRID_W, GRID_W).transpose(0, 1, 2, 3, 5, 4, 6)
    ok = row_ok[:, :, None, :, None] & col_ok[None, None, :, None, :]
    t = jnp.where(ok, t * LOG2E, NEG_INF)
    return t.reshape(n_l, n_h, len(cases), nq * GRID_W, nk * GRID_W)


def _na_attention(z, batch, seq, heads, cache_k, cache_v, layer, bias):
    _, nq, nk, blocks, cases = _na_plan(seq // GRID_W)
    past = cache_k.shape[3]
    cspec = pl.BlockSpec((None, None, None, past, HEAD_DIM), lambda b, h: (b, layer, h, 0, 0))
    return pl.pallas_call(
        functools.partial(_na_kernel, blocks=blocks, nq=nq, nk=nk),
        grid=(batch, heads),
        in_specs=[pl.BlockSpec((seq, HEAD_DIM), lambda b, h: (b, h)),
                  pl.BlockSpec((seq, HEAD_DIM), lambda b, h: (b, heads + h)),
                  pl.BlockSpec((seq, HEAD_DIM), lambda b, h: (b, 2 * heads + h)),
                  cspec, cspec,
                  pl.BlockSpec((None, None, len(cases), nq * GRID_W, nk * GRID_W),
                               lambda b, h: (layer, h, 0, 0, 0))],
        out_specs=pl.BlockSpec((seq, HEAD_DIM), lambda b, h: (b, h)),
        out_shape=jax.ShapeDtypeStruct((batch * seq, heads * HEAD_DIM), BF16),
        scratch_shapes=[pltpu.VMEM((seq, 2 * HEAD_DIM), BF16), pltpu.VMEM((past, 2 * HEAD_DIM), BF16)],
        compiler_params=_params(2),
        name="latent_na",
    )(z, z, z, cache_k, cache_v, bias)


def _gqa_kernel(q_ref, k_ref, v_ref, ck_ref, cv_ref, qg_ref, kg_ref, cos_ref, sin_ref, o_ref, kf_ref, vf_ref,
                *, group, past, seq, tq):
    i = pl.program_id(2)

    @pl.when(i == 0)
    def _():
        kf_ref[0:past, :] = ck_ref[...].astype(BF16)
        vf_ref[0:past, :] = _with_ones(cv_ref[...].astype(BF16))
        k = _rope(_rms(k_ref[...].astype(F32), kg_ref[...]), cos_ref[...], sin_ref[...])
        kf_ref[past:past + seq, :] = k.astype(BF16)
        vf_ref[past:past + seq, :] = _with_ones(v_ref[...])

    t0 = pl.multiple_of(i * tq, tq)
    cos = cos_ref[pl.ds(t0, tq), :]
    sin = sin_ref[pl.ds(t0, tq), :]
    kf = kf_ref[...]
    vf = vf_ref[...]
    for g in range(group):
        q = q_ref[:, g * HEAD_DIM:(g + 1) * HEAD_DIM].astype(F32)
        q = _rope(_rms(q, qg_ref[...]), cos, sin) * (ATTN_SCALE * LOG2E)
        s = _dot(q.astype(BF16), kf, _NT)
        o_ref[:, g * HEAD_DIM:(g + 1) * HEAD_DIM] = _exp2_pv([s], [vf]).astype(o_ref.dtype)


def _rope_tables(seq):
    t = np.arange(seq)
    half = HEAD_DIM // 2
    inv = 1.0 / (ROPE_THETA ** (np.arange(0, half, 2, dtype=np.float32) / half))
    row = (t // GRID_W).astype(np.float32)
    col = (t % GRID_W).astype(np.float32)
    ang = jnp.concatenate([jnp.asarray(row[:, None] * inv)] * 2 + [jnp.asarray(col[:, None] * inv)] * 2, axis=-1)
    sign = np.where((np.arange(HEAD_DIM) % half) < half // 2, -1.0, 1.0).astype(np.float32)
    return jnp.cos(ang), jnp.sin(ang) * sign


def _gqa_attention(z, batch, seq, kv_heads, group, q_col, k_col, v_col, cache_k, cache_v, layer, gains, tables):
    past = cache_k.shape[3]
    tq = min(256, seq)
    nq = seq // tq
    qw = group * HEAD_DIM
    cspec = pl.BlockSpec((None, None, None, past, HEAD_DIM), lambda b, h, i: (b, layer, h, 0, 0))
    gspec = pl.BlockSpec((1, HEAD_DIM), lambda b, h, i: (0, 0))
    tspec = pl.BlockSpec((seq, HEAD_DIM), lambda b, h, i: (0, 0))
    return pl.pallas_call(
        functools.partial(_gqa_kernel, group=group, past=past, seq=seq, tq=tq),
        grid=(batch, kv_heads, nq),
        in_specs=[pl.BlockSpec((tq, qw), lambda b, h, i: (b * nq + i, q_col // group + h)),
                  pl.BlockSpec((seq, HEAD_DIM), lambda b, h, i: (b, k_col + h)),
                  pl.BlockSpec((seq, HEAD_DIM), lambda b, h, i: (b, v_col + h)),
                  cspec, cspec, gspec, gspec, tspec, tspec],
        out_specs=pl.BlockSpec((tq, qw), lambda b, h, i: (b * nq + i, h)),
        out_shape=jax.ShapeDtypeStruct((batch * seq, kv_heads * qw), BF16),
        scratch_shapes=[pltpu.VMEM((past + seq, HEAD_DIM), BF16), pltpu.VMEM((past + seq, 2 * HEAD_DIM), BF16)],
        compiler_params=_params(3),
        name="latent_gqa",
    )(z, z, z, cache_k, cache_v, gains[0], gains[1], tables[0], tables[1])


def _outproj_kernel(x_ref, na_ref, gq_ref, rt_ref, w_ref, g1_ref, sh2_ref, sc2_ref, lg_ref, lb_ref, wr_ref,
                    x1_ref, h2_ref, lt_ref, *, alpha, sub):
    n_na = na_ref.shape[1]
    n_gq = gq_ref.shape[1]
    n_exp = lt_ref.shape[1]
    for r in range(x_ref.shape[0] // sub):
        rs = slice(r * sub, (r + 1) * sub)
        y = _dot(na_ref[rs, :], w_ref[0:n_na, :])
        y += _dot(gq_ref[rs, :], w_ref[n_na:n_na + n_gq, :])
        y += _dot(rt_ref[rs, :], w_ref[n_na + n_gq:, :])
        x1 = _ln(alpha * x_ref[rs, :] + g1_ref[...] * y) * lg_ref[...] + lb_ref[...]
        x1_ref[rs, :] = x1
        h2 = _ln(x1) * (1.0 + sc2_ref[...]) + sh2_ref[...]
        hi, lo = _split_bf16(h2)
        h2_ref[rs, :] = hi
        both = _dot(hi, wr_ref[...])
        lt_ref[rs, :] = both[:, :n_exp] + both[:, n_exp:] + _dot(lo, wr_ref[:, 0:n_exp])


def _out_projection(x, na_o, gq_o, rt_o, w_bf16, mod3, ln_g, ln_b, wr_hl, tokens_per_batch, per_request, alpha):
    rows, d = x.shape
    tm = 512
    per = tokens_per_batch // tm if per_request else 1
    n_exp = wr_hl.shape[1] // 2

    def mod_spec(chunk):
        return pl.BlockSpec((None, 1, d), lambda i: (_mod_row(i, per, per_request) * 6 + chunk, 0, 0))

    def row_spec(width):
        return pl.BlockSpec((tm, width), lambda i: (i, 0))

    vec = pl.BlockSpec((1, d), lambda i: (0, 0))
    return pl.pallas_call(
        functools.partial(_outproj_kernel, alpha=alpha, sub=256),
        grid=(rows // tm,),
        in_specs=[row_spec(d), row_spec(na_o.shape[1]), row_spec(gq_o.shape[1]), row_spec(rt_o.shape[1]),
                  pl.BlockSpec(w_bf16.shape, lambda i: (0, 0)),
                  mod_spec(2), mod_spec(3), mod_spec(4), vec, vec,
                  pl.BlockSpec(wr_hl.shape, lambda i: (0, 0))],
        out_specs=[row_spec(d), row_spec(d), row_spec(n_exp)],
        out_shape=[jax.ShapeDtypeStruct((rows, d), F32), jax.ShapeDtypeStruct((rows, d), BF16),
                   jax.ShapeDtypeStruct((rows, n_exp), F32)],
        compiler_params=_params(1),
        name="outproj_postnorm_router",
    )(x, na_o, gq_o, rt_o, w_bf16, mod3, mod3, mod3, ln_g, ln_b, wr_hl)


def _topk_kernel(lt_ref, slot_ref, gate_ref, *, cap, seq):
    logits = lt_ref[...]
    n_exp = logits.shape[0]
    m = jnp.max(logits, axis=0, keepdims=True)
    ex = jnp.exp(logits - m)
    aff = ex / jnp.sum(ex, axis=0, keepdims=True)
    bits = lax.bitcast_convert_type(aff, jnp.int32)

    def count(mask):
        return jnp.sum(jnp.where(mask, 1.0, 0.0), axis=1, keepdims=True)

    def value_step(it, thr):
        cand = thr | jnp.left_shift(jnp.int32(1), 30 - it)
        return jnp.where(count(bits >= cand) >= cap, cand, thr)

    thr = lax.fori_loop(0, 31, value_step, jnp.zeros((n_exp, 1), jnp.int32))
    above = bits > thr
    tied = bits == thr
    need = cap - count(above)
    tok = lax.broadcasted_iota(jnp.int32, (n_exp, seq), 1)
    n_bits = int(seq - 1).bit_length()

    def index_step(it, bound):
        cand = bound | jnp.left_shift(jnp.int32(1), n_bits - 1 - it)
        return jnp.where(count(tied & (tok < cand)) < need, cand, bound)

    bound = lax.fori_loop(0, n_bits, index_step, jnp.zeros((n_exp, 1), jnp.int32))
    sel = above | (tied & (tok <= bound))
    self = jnp.where(sel, 1.0, 0.0)
    lanes = 128
    upper = jnp.where(lax.broadcasted_iota(jnp.int32, (lanes, lanes), 0)
                      < lax.broadcasted_iota(jnp.int32, (lanes, lanes), 1), 1.0, 0.0).astype(BF16)
    running = jnp.zeros((n_exp, 1), F32)
    for blk in range(seq // lanes):
        sl = slice(blk * lanes, (blk + 1) * lanes)
        chunk = self[:, sl]
        pos = _dot(chunk.astype(BF16), upper) + running
        slot_ref[:, sl] = jnp.where(chunk > 0.0, pos.astype(jnp.int32), -1)
        running = running + jnp.sum(chunk, axis=1, keepdims=True)
    gate_ref[...] = aff


def _route(logits_t, batch, seq, cap):
    n_exp = logits_t.shape[0]
    spec = pl.BlockSpec((None, n_exp, seq), lambda b: (b, 0, 0))
    return pl.pallas_call(
        functools.partial(_topk_kernel, cap=cap, seq=seq),
        grid=(batch,),
        in_specs=[pl.BlockSpec((n_exp, seq), lambda b: (0, b))],
        out_specs=[spec, spec],
        out_shape=[jax.ShapeDtypeStruct((batch, n_exp, seq), jnp.int32),
                   jax.ShapeDtypeStruct((batch, n_exp, seq), F32)],
        compiler_params=_params(1),
        name="route_topk",
    )(logits_t)


def _gather_kernel(*refs, n_inner, cap, aliased):
    if aliased:
        refs = refs[2:]
    slot_ref, gate_ref, h_ref, xs_ref, gs_ref = refs
    h = h_ref[...]
    seq = h.shape[0]
    row = lax.broadcasted_iota(jnp.int32, (cap, seq), 0)
    for e in range(n_inner):
        onehot = slot_ref[e:e + 1, :] == row
        xs_ref[e] = _dot(jnp.where(onehot, 1.0, 0.0).astype(BF16), h).astype(BF16)
        gs_ref[e] = jnp.sum(jnp.where(onehot, gate_ref[e:e + 1, :], 0.0), axis=1, keepdims=True)


def _gather(slot, gate, h, batch, seq, cap, n_inner, row_blk0, total_rows, prev=None):
    n_exp = slot.shape[1]
    d = h.shape[1]
    n_outer = n_exp // n_inner
    slot4 = slot.reshape(batch, n_outer, n_inner, seq)
    gate4 = gate.reshape(batch, n_outer, n_inner, seq)
    sspec = pl.BlockSpec((None, None, n_inner, seq), lambda b, e: (b, e, 0, 0))
    in_specs = [sspec, sspec, pl.BlockSpec((seq, d), lambda b, e: (b, 0))]
    args = [slot4, gate4, h]
    aliases = {}
    if prev is not None:
        in_specs = [pl.BlockSpec(memory_space=pl.ANY)] * 2 + in_specs
        args = list(prev) + args
        aliases = {0: 0, 1: 1}
    return pl.pallas_call(
        functools.partial(_gather_kernel, n_inner=n_inner, cap=cap, aliased=prev is not None),
        grid=(batch, n_outer),
        in_specs=in_specs,
        out_specs=[pl.BlockSpec((n_inner, cap, d), lambda b, e: (e, row_blk0 + b, 0)),
                   pl.BlockSpec((n_inner, cap, 1), lambda b, e: (e, row_blk0 + b, 0))],
        out_shape=[jax.ShapeDtypeStruct((n_exp, total_rows, d), BF16),
                   jax.ShapeDtypeStruct((n_exp, total_rows, 1), F32)],
        input_output_aliases=aliases,
        compiler_params=_params(2),
        name="moe_gather",
    )(*args)


def _ffn_kernel(xs_ref, gs_ref, wg_ref, wu_ref, wd_ref, y_ref):
    x = xs_ref[...]
    a = _dot(x, wg_ref[...])
    u = _dot(x, wu_ref[...])
    y = _dot((_silu(a) * u).astype(BF16), wd_ref[...])
    y_ref[...] = (y * gs_ref[...]).astype(y_ref.dtype)


def _expert_ffn(xs, gs, wg, wu, wd):
    n_exp, rows, d = xs.shape
    ff = wg.shape[2]
    tm = math.gcd(rows, 512)
    return pl.pallas_call(
        _ffn_kernel,
        grid=(n_exp, rows // tm),
        in_specs=[pl.BlockSpec((None, tm, d), lambda e, i: (e, i, 0)),
                  pl.BlockSpec((None, tm, 1), lambda e, i: (e, i, 0)),
                  pl.BlockSpec((None, d, ff), lambda e, i: (e, 0, 0)),
                  pl.BlockSpec((None, d, ff), lambda e, i: (e, 0, 0)),
                  pl.BlockSpec((None, ff, d), lambda e, i: (e, 0, 0))],
        out_specs=pl.BlockSpec((None, tm, d), lambda e, i: (e, i, 0)),
        out_shape=jax.ShapeDtypeStruct((n_exp, rows, d), BF16),
        compiler_params=_params(2),
        name="moe_ffn",
    )(xs, gs, wg, wu, wd)


def _combine_kernel(x_ref, y_ref, slot_ref, g2_ref, lg_ref, lb_ref, o_ref, *, alpha, cap):
    n_exp = y_ref.shape[0]
    tt = x_ref.shape[0]
    col = lax.broadcasted_iota(jnp.int32, (tt, cap), 1)
    slot = slot_ref[...]
    acc = jnp.zeros(x_ref.shape, F32)
    for e in range(n_exp):
        onehot = jnp.where(slot[:, e:e + 1] == col, 1.0, 0.0).astype(BF16)
        acc += _dot(onehot, y_ref[e])
    o_ref[...] = _ln(alpha * x_ref[...] + g2_ref[...] * acc) * lg_ref[...] + lb_ref[...]


def _combine(x1, y, slot_t, mod3, ln_g, ln_b, batch, seq, cap, row_blk0, per_request, alpha):
    rows, d = x1.shape
    n_exp = y.shape[0]
    tt = min(256, seq)
    nt = seq // tt
    vec = pl.BlockSpec((1, d), lambda b, i: (0, 0))
    return pl.pallas_call(
        functools.partial(_combine_kernel, alpha=alpha, cap=cap),
        grid=(batch, nt),
        in_specs=[pl.BlockSpec((tt, d), lambda b, i: (b * nt + i, 0)),
                  pl.BlockSpec((n_exp, cap, d), lambda b, i: (0, row_blk0 + b, 0)),
                  pl.BlockSpec((None, tt, n_exp), lambda b, i: (b, i, 0)),
                  pl.BlockSpec((None, 1, d), lambda b, i: (_mod_row(b, 1, per_request) * 6 + 5, 0, 0)),
                  vec, vec],
        out_specs=pl.BlockSpec((tt, d), lambda b, i: (b * nt + i, 0)),
        out_shape=jax.ShapeDtypeStruct((rows, d), F32),
        compiler_params=_params(2),
        name="moe_combine_postnorm",
    )(x1, y, slot_t, mod3, ln_g, ln_b)


def kernel(x_prompt, x_sample, cache_na_k, cache_na_v, cache_gqa_k, cache_gqa_v, state_ret_fwd, state_ret_bwd,
           c, c_ctx, w_in, w_out, w_mod, b_mod, ln1_g, ln1_b, ln2_g, ln2_b, q_norm_g, k_norm_g, na_rpb,
           ret_decay_fwd, ret_decay_bwd, w_router, w_gate, w_up, w_down):
    batch, seq, d = x_prompt.shape
    dbatch, dseq, _ = x_sample.shape
    depth = w_in.shape[0]
    na_heads = cache_na_k.shape[2]
    kv_heads = cache_gqa_k.shape[2]
    ret_heads = state_ret_fwd.shape[2]
    n_exp = w_router.shape[2]
    hd = HEAD_DIM
    gqa_heads = (w_in.shape[2] // hd - 3 * na_heads - 2 * kv_heads - 4 * ret_heads)
    group = gqa_heads // kv_heads
    alpha = float((2 * depth) ** 0.25)
    c_na = 0
    c_gq = 3 * na_heads
    c_gk = c_gq + gqa_heads
    c_gv = c_gk + kv_heads
    c_rt = c_gv + kv_heads
    cap_c = EC_CAPACITY_FACTOR * seq // n_exp
    cap_l = EC_CAPACITY_FACTOR * dseq // n_exp
    assert dbatch + 1 <= MOD_ROWS and (batch * cap_c) % cap_l == 0
    lat_blk0 = batch * cap_c // cap_l
    rows_per_expert = batch * cap_c + dbatch * cap_l

    cond = jnp.zeros((MOD_ROWS, d), F32).at[0].set(c_ctx).at[1:1 + dbatch].set(c)
    mod_all = _modulation(cond, w_mod, b_mod)
    tables = _rope_tables(dseq)
    bias = _na_bias(na_rpb, dseq // GRID_W)

    xp = x_prompt.reshape(batch * seq, d)
    xs = x_sample.reshape(dbatch * dseq, d)
    outs = [[] for _ in range(6)]
    for l in range(depth):
        mod3 = mod_all[l].reshape(MOD_ROWS * 6, 1, d)
        w_in_b = w_in[l].astype(BF16)
        w_out_b = w_out[l].astype(BF16)
        wr_t = jnp.concatenate(_split_bf16(w_router[l]), axis=1)
        wg_b, wu_b, wd_b = w_gate[l].astype(BF16), w_up[l].astype(BF16), w_down[l].astype(BF16)
        gains = (q_norm_g[l].reshape(1, hd), k_norm_g[l].reshape(1, hd))
        l1g, l1b = ln1_g[l].reshape(1, d), ln1_b[l].reshape(1, d)
        l2g, l2b = ln2_g[l].reshape(1, d), ln2_b[l].reshape(1, d)

        zc = _in_projection(xp, mod3, w_in_b, seq, False, F32)
        (na_c,) = _ctx_attention(zc, batch, seq, na_heads, 1, c_na, c_na + na_heads, c_na + 2 * na_heads)
        gq_c, gk_n = _ctx_attention(zc, batch, seq, kv_heads, group, c_gq, c_gk, c_gv, gains)
        rt_c, s_f, s_b = _retention(zc, batch, seq, ret_heads, c_rt, ret_decay_fwd[l], ret_decay_bwd[l],
                                    state_out=True)

        def heads_of(col, n):
            return zc[:, col * hd:(col + n) * hd].reshape(batch, seq, n, hd).transpose(0, 2, 1, 3)

        outs[0].append(heads_of(c_na + na_heads, na_heads))
        outs[1].append(heads_of(c_na + 2 * na_heads, na_heads))
        outs[2].append(gk_n)
        outs[3].append(heads_of(c_gv, kv_heads))
        outs[4].append(s_f)
        outs[5].append(s_b)

        zl = _in_projection(xs, mod3, w_in_b, dseq, True, BF16)
        (na_l,) = [_na_attention(zl, dbatch, dseq, na_heads, cache_na_k, cache_na_v, l, bias)]
        gq_l = _gqa_attention(zl, dbatch, dseq, kv_heads, group, c_gq, c_gk, c_gv, cache_gqa_k, cache_gqa_v, l,
                              gains, tables)
        (rt_l,) = _retention(zl, dbatch, dseq, ret_heads, c_rt, ret_decay_fwd[l], ret_decay_bwd[l],
                             states=((state_ret_fwd, state_ret_bwd), l))

        xp1, hp, lt_c = _out_projection(xp, na_c, gq_c, rt_c, w_out_b, mod3, l1g, l1b, wr_t, seq, False, alpha)
        xs1, hs, lt_l = _out_projection(xs, na_l, gq_l, rt_l, w_out_b, mod3, l1g, l1b, wr_t, dseq, True, alpha)

        slot_c, gate_c = _route(lt_c.T, batch, seq, cap_c)
        slot_l, gate_l = _route(lt_l.T, dbatch, dseq, cap_l)
        gathered = _gather(slot_l, gate_l, hs, dbatch, dseq, cap_l, 2, lat_blk0, rows_per_expert)
        gathered = _gather(slot_c, gate_c, hp, batch, seq, cap_c, n_exp, 0, rows_per_expert, prev=gathered)
        y = _expert_ffn(gathered[0], gathered[1], wg_b, wu_b, wd_b)
        xp = _combine(xp1, y, slot_c.transpose(0, 2, 1), mod3, l2g, l2b, batch, seq, cap_c, 0, False, alpha)
        xs = _combine(xs1, y, slot_l.transpose(0, 2, 1), mod3, l2g, l2b, dbatch, dseq, cap_l, lat_blk0, True, alpha)

    stacked = [jnp.stack(o, axis=1) for o in outs]
    return (xp.reshape(batch, seq, d), xs.reshape(dbatch, dseq, d), *stacked)
```

```python
import functools
import math

import numpy as np
import jax
import jax.numpy as jnp
from jax import lax
from jax.experimental import pallas as pl
from jax.experimental.pallas import tpu as pltpu

F32 = jnp.float32
BF16 = jnp.bfloat16

HEAD_DIM = 128
GRID_W = 64
NA_WIN_R = 8
NA_WIN_C = 16
ROPE_THETA = 10000.0
EC_CAPACITY_FACTOR = 2
LN_EPS = 1e-5
RMS_EPS = 1e-6
NEG_INF = -1e30
ATTN_SCALE = HEAD_DIM ** -0.5
LOG2E = 1.4426950408889634
NA_BLOCK_ROWS = 4
MOD_ROWS = 16
VMEM_LIMIT = 56 * 1024 * 1024

_NT = (((1,), (1,)), ((), ()))
_NN = (((1,), (0,)), ((), ()))


def _params(n_grid, vmem=VMEM_LIMIT):
    return pltpu.CompilerParams(dimension_semantics=("arbitrary",) * n_grid, vmem_limit_bytes=vmem)


def _dot(a, b, dims=_NN):
    return lax.dot_general(a, b, dims, preferred_element_type=F32)


def _split_bf16(x):
    hi = x.astype(BF16)
    lo = (x - hi.astype(F32)).astype(BF16)
    return hi, lo


def _dot3(a, b, dims=_NN):
    ah, al = _split_bf16(a)
    bh, bl = _split_bf16(b)
    return _dot(ah, bh, dims) + _dot(al, bh, dims) + _dot(ah, bl, dims)


def _silu(x):
    return x / (1.0 + jnp.exp(-x))


def _ln(x):
    mu = jnp.mean(x, axis=-1, keepdims=True)
    xc = x - mu
    var = jnp.mean(xc * xc, axis=-1, keepdims=True)
    return xc * lax.rsqrt(var + LN_EPS)


def _rms(x, g):
    return x * lax.rsqrt(jnp.mean(x * x, axis=-1, keepdims=True) + RMS_EPS) * g


def _rope(x, cos, sin_signed):
    lane = lax.broadcasted_iota(jnp.int32, x.shape, 1) & (HEAD_DIM // 2 - 1)
    partner = jnp.where(lane < HEAD_DIM // 4,
                        pltpu.roll(x, HEAD_DIM - HEAD_DIM // 4, 1),
                        pltpu.roll(x, HEAD_DIM // 4, 1))
    return x * cos + partner * sin_signed


def _softmax_pv(scores, values):
    m = functools.reduce(jnp.maximum, [jnp.max(s, axis=-1, keepdims=True) for s in scores])
    ps = [jnp.exp(s - m) for s in scores]
    denom = functools.reduce(lambda a, b: a + b, [jnp.sum(p, axis=-1, keepdims=True) for p in ps])
    o = functools.reduce(lambda a, b: a + b, [_dot(p.astype(BF16), v) for p, v in zip(ps, values)])
    return o / denom


def _mod_kernel(c_ref, w_ref, b_ref, o_ref):
    a = _silu(c_ref[...])
    o_ref[...] = _dot3(a, w_ref[...]) + b_ref[...]


def _modulation(cond, w_mod, b_mod):
    depth, d, n = w_mod.shape
    tn = 768
    return pl.pallas_call(
        _mod_kernel,
        grid=(depth, n // tn),
        in_specs=[pl.BlockSpec((MOD_ROWS, d), lambda l, j: (0, 0)),
                  pl.BlockSpec((None, d, tn), lambda l, j: (l, 0, j)),
                  pl.BlockSpec((None, 1, tn), lambda l, j: (l, 0, j))],
        out_specs=pl.BlockSpec((None, MOD_ROWS, tn), lambda l, j: (l, 0, j)),
        out_shape=jax.ShapeDtypeStruct((depth, MOD_ROWS, n), F32),
        compiler_params=_params(2),
        name="adaln_mod",
    )(cond, w_mod, b_mod.reshape(depth, 1, n))


def _inproj_kernel(x_ref, sh_ref, sc_ref, w_ref, z_ref, *, sub):
    for r in range(x_ref.shape[0] // sub):
        rs = slice(r * sub, (r + 1) * sub)
        h = _ln(x_ref[rs, :]) * (1.0 + sc_ref[...]) + sh_ref[...]
        z_ref[rs, :] = _dot(h.astype(BF16), w_ref[...]).astype(z_ref.dtype)


def _mod_row(block, blocks_per_request, per_request):
    return 1 + block // blocks_per_request if per_request else 0


def _in_projection(x, mod3, w_bf16, layer, tokens_per_batch, per_request, out_dtype):
    rows, d = x.shape
    n = w_bf16.shape[2]
    tm = min(512, tokens_per_batch)
    tn = 2560
    per = tokens_per_batch // tm

    def mod_spec(chunk):
        return pl.BlockSpec((None, 1, d), lambda j, i: (_mod_row(i, per, per_request) * 6 + chunk, 0, 0))

    return pl.pallas_call(
        functools.partial(_inproj_kernel, sub=min(256, tm)),
        grid=(n // tn, rows // tm),
        in_specs=[pl.BlockSpec((tm, d), lambda j, i: (i, 0)),
                  mod_spec(0), mod_spec(1),
                  pl.BlockSpec((None, d, tn), lambda j, i: (layer, 0, j))],
        out_specs=pl.BlockSpec((tm, tn), lambda j, i: (i, j)),
        out_shape=jax.ShapeDtypeStruct((rows, n), out_dtype),
        compiler_params=_params(2),
        name="ln_inproj",
    )(x, mod3, mod3, w_bf16)


def _ctx_attn_kernel(*refs, group, normed):
    if normed:
        q_ref, k_ref, v_ref, qg_ref, kg_ref, o_ref, kn_ref = refs
    else:
        q_ref, k_ref, v_ref, o_ref = refs
    k = k_ref[...]
    if normed:
        k = _rms(k, kg_ref[...])
        kn_ref[...] = k
    kb = k.astype(BF16)
    vb = v_ref[...].astype(BF16)
    for g in range(group):
        q = q_ref[:, g * HEAD_DIM:(g + 1) * HEAD_DIM]
        if normed:
            q = _rms(q, qg_ref[...])
        s = _dot(q.astype(BF16), kb, _NT) * ATTN_SCALE
        o_ref[:, g * HEAD_DIM:(g + 1) * HEAD_DIM] = _softmax_pv([s], [vb]).astype(o_ref.dtype)


def _ctx_attention(z, batch, seq, heads, group, q_col, k_col, v_col, gains=None):
    normed = gains is not None
    qw = group * HEAD_DIM
    in_specs = [pl.BlockSpec((seq, qw), lambda b, h: (b, q_col // group + h)),
                pl.BlockSpec((seq, HEAD_DIM), lambda b, h: (b, k_col + h)),
                pl.BlockSpec((seq, HEAD_DIM), lambda b, h: (b, v_col + h))]
    args = [z, z, z]
    out_specs = [pl.BlockSpec((seq, qw), lambda b, h: (b, h))]
    out_shape = [jax.ShapeDtypeStruct((batch * seq, heads * qw), BF16)]
    if normed:
        in_specs += [pl.BlockSpec((1, HEAD_DIM), lambda b, h: (0, 0))] * 2
        args += list(gains)
        out_specs.append(pl.BlockSpec((None, None, seq, HEAD_DIM), lambda b, h: (b, h, 0, 0)))
        out_shape.append(jax.ShapeDtypeStruct((batch, heads, seq, HEAD_DIM), F32))
    return pl.pallas_call(
        functools.partial(_ctx_attn_kernel, group=group, normed=normed),
        grid=(batch, heads),
        in_specs=in_specs, out_specs=out_specs, out_shape=out_shape,
        compiler_params=_params(2),
        name="ctx_gqa" if normed else "ctx_na",
    )(*args)


def _log_sigmoid(x):
    return -jnp.log1p(jnp.exp(-x))


def _retention_kernel(*refs, seq, tq, with_state_in, with_state_out):
    q_ref, k_ref, v_ref, g_ref, df_ref, db_ref = refs[:6]
    refs = refs[6:]
    if with_state_in:
        sf_ref, sb_ref = refs[:2]
        refs = refs[2:]
    o_ref = refs[0]
    decay_ref = refs[-1]
    lgf = _log_sigmoid(df_ref[...])
    lgb = _log_sigmoid(db_ref[...])
    t0 = pl.program_id(1) * tq

    @pl.when(pl.program_id(2) == 0)
    def _():
        i = t0 + lax.broadcasted_iota(jnp.int32, (tq, seq), 0)
        j = lax.broadcasted_iota(jnp.int32, (tq, seq), 1)
        dist = (i - j).astype(F32)
        decay = jnp.where(dist == 0.0, 2.0, jnp.exp(jnp.where(dist > 0.0, lgf * dist, -lgb * dist)))
        decay_ref[...] = decay * ATTN_SCALE

    q = q_ref[...]
    kb = k_ref[...].astype(BF16)
    vb = v_ref[...].astype(BF16)
    att = _dot(q.astype(BF16), kb, _NT) * decay_ref[...]
    o = _dot(att.astype(BF16), vb)
    if with_state_in:
        t = (t0 + lax.broadcasted_iota(jnp.int32, (tq, 1), 0)).astype(F32)
        qf = q.astype(F32)
        o += _dot((qf * jnp.exp(lgf * (t + 1.0))).astype(BF16), sf_ref[...].astype(BF16))
        o += _dot((qf * jnp.exp(lgb * (seq - t))).astype(BF16), sb_ref[...].astype(BF16))
    o_ref[...] = (_ln(o) * _silu(g_ref[...].astype(F32))).astype(o_ref.dtype)
    if with_state_out:
        nsf_ref, nsb_ref = refs[1:3]
        tj = lax.broadcasted_iota(jnp.int32, (seq, 1), 0).astype(F32)
        kf = k_ref[...].astype(F32) * ATTN_SCALE
        kdf = (kf * jnp.exp(lgf * (seq - 1.0 - tj))).T.astype(BF16)
        kdb = (kf * jnp.exp(lgb * tj)).T.astype(BF16)
        nsf_ref[...] = _dot(kdf, vb)
        nsb_ref[...] = _dot(kdb, vb)


def _retention(z, batch, seq, heads, col0, dec_f, dec_b, states=None, state_out=False):
    tq = min(512, seq)
    nq = seq // tq
    assert not (state_out and nq != 1)
    in_specs = [pl.BlockSpec((tq, HEAD_DIM), lambda h, i, b: (b * nq + i, col0 + h)),
                pl.BlockSpec((seq, HEAD_DIM), lambda h, i, b: (b, col0 + heads + h)),
                pl.BlockSpec((seq, HEAD_DIM), lambda h, i, b: (b, col0 + 2 * heads + h)),
                pl.BlockSpec((tq, HEAD_DIM), lambda h, i, b: (b * nq + i, col0 + 3 * heads + h)),
                pl.BlockSpec((None, 1, 1), lambda h, i, b: (h, 0, 0)),
                pl.BlockSpec((None, 1, 1), lambda h, i, b: (h, 0, 0))]
    args = [z, z, z, z, dec_f.reshape(heads, 1, 1), dec_b.reshape(heads, 1, 1)]
    if states is not None:
        (sf, sb), layer = states
        spec = pl.BlockSpec((None, None, None, HEAD_DIM, HEAD_DIM), lambda h, i, b: (b, layer, h, 0, 0))
        in_specs += [spec, spec]
        args += [sf, sb]
    out_specs = [pl.BlockSpec((tq, HEAD_DIM), lambda h, i, b: (b * nq + i, h))]
    out_shape = [jax.ShapeDtypeStruct((batch * seq, heads * HEAD_DIM), BF16)]
    if state_out:
        spec = pl.BlockSpec((None, None, HEAD_DIM, HEAD_DIM), lambda h, i, b: (b, h, 0, 0))
        out_specs += [spec, spec]
        out_shape += [jax.ShapeDtypeStruct((batch, heads, HEAD_DIM, HEAD_DIM), F32)] * 2
    return pl.pallas_call(
        functools.partial(_retention_kernel, seq=seq, tq=tq, with_state_in=states is not None,
                          with_state_out=state_out),
        grid=(heads, nq, batch),
        in_specs=in_specs, out_specs=out_specs, out_shape=out_shape,
        scratch_shapes=[pltpu.VMEM((tq, seq), F32)],
        compiler_params=_params(3),
        name="retention",
    )(*args)


def _with_ones(v):
    return jnp.concatenate([v, jnp.ones(v.shape, v.dtype)], axis=1)


def _exp2_pv(scores, values_with_ones):
    m = functools.reduce(jnp.maximum, [jnp.max(s, axis=-1, keepdims=True) for s in scores])
    r = functools.reduce(lambda a, b: a + b,
                         [_dot(jnp.exp2(s - m).astype(BF16), v) for s, v in zip(scores, values_with_ones)])
    return r[:, :HEAD_DIM] / r[:, HEAD_DIM:HEAD_DIM + 1]


def _na_plan(rows):
    kr = min(NA_WIN_R, rows)
    nq = min(NA_BLOCK_ROWS, rows)
    nk = min(nq + kr - 1, rows)
    assert rows % nq == 0
    blocks, cases = [], []
    for r0 in range(0, rows, nq):
        start = int(np.clip(r0 - kr // 2, 0, rows - nk))
        win = [int(np.clip(r - kr // 2, 0, rows - kr)) for r in range(r0, r0 + nq)]
        assert all(start <= w and w + kr <= start + nk for w in win)
        key = (start - r0,) + tuple(w - start for w in win)
        if key not in cases:
            cases.append(key)
        blocks.append((r0, start, cases.index(key)))
    return kr, nq, nk, blocks, cases


def _na_kernel(q_ref, k_ref, v_ref, ck_ref, cv_ref, bias_ref, o_ref, vx_ref, cvx_ref, *, blocks, nq, nk):
    vx_ref[...] = _with_ones(v_ref[...])
    cvx_ref[...] = _with_ones(cv_ref[...].astype(BF16))
    ck = ck_ref[...].astype(BF16)
    c = ATTN_SCALE * LOG2E
    for r0, start, case in blocks:
        qs = slice(r0 * GRID_W, (r0 + nq) * GRID_W)
        ks = slice(start * GRID_W, (start + nk) * GRID_W)
        q = q_ref[qs, :]
        s_loc = _dot(q, k_ref[ks, :], _NT) * c + bias_ref[case]
        s_ctx = _dot(q, ck, _NT) * c
        o_ref[qs, :] = _exp2_pv([s_loc, s_ctx], [vx_ref[ks, :], cvx_ref[...]]).astype(o_ref.dtype)


def _na_bias(rpb, rows):
    kr, nq, nk, _, cases = _na_plan(rows)
    dr = np.zeros((len(cases), nq, nk), np.int32)
    row_ok = np.zeros((len(cases), nq, nk), bool)
    for c, key in enumerate(cases):
        rel, offs = key[0], key[1:]
        for ri in range(nq):
            for ju in range(nk):
                row_ok[c, ri, ju] = 0 <= ju - offs[ri] < kr
                dr[c, ri, ju] = np.clip(rel + ju - ri + NA_WIN_R - 1, 0, 2 * NA_WIN_R - 2)
    cq = np.arange(GRID_W)
    ck = np.arange(GRID_W)
    col_start = np.clip(cq - NA_WIN_C // 2, 0, GRID_W - NA_WIN_C)
    col_ok = (ck[None, :] >= col_start[:, None]) & (ck[None, :] < col_start[:, None] + NA_WIN_C)
    dc = np.clip(ck[None, :] - cq[:, None] + (NA_WIN_C - 1), 0, 2 * NA_WIN_C - 2)
    pick_col = (dc.reshape(-1)[None, :] == np.arange(2 * NA_WIN_C - 1)[:, None]).astype(np.float32)
    n_l, n_h = rpb.shape[:2]
    t = jnp.take(rpb.astype(F32), dr.reshape(-1), axis=2)
    t = jnp.einsum("lhxb,bn->lhxn", t, pick_col, precision=lax.Precision.HIGHEST)
    t = t.reshape(n_l, n_h, len(cases), nq, nk, GRID_W, GRID_W).transpose(0, 1, 2, 3, 5, 4, 6)
    ok = row_ok[:, :, None, :, None] & col_ok[None, None, :, None, :]
    t = jnp.where(ok, t * LOG2E, NEG_INF)
    return t.reshape(n_l, n_h, len(cases), nq * GRID_W, nk * GRID_W)


def _na_attention(z, batch, seq, heads, cache_k, cache_v, layer, bias):
    _, nq, nk, blocks, cases = _na_plan(seq // GRID_W)
    past = cache_k.shape[3]
    cspec = pl.BlockSpec((None, None, None, past, HEAD_DIM), lambda b, h: (b, layer, h, 0, 0))
    return pl.pallas_call(
        functools.partial(_na_kernel, blocks=blocks, nq=nq, nk=nk),
        grid=(batch, heads),
        in_specs=[pl.BlockSpec((seq, HEAD_DIM), lambda b, h: (b, h)),
                  pl.BlockSpec((seq, HEAD_DIM), lambda b, h: (b, heads + h)),
                  pl.BlockSpec((seq, HEAD_DIM), lambda b, h: (b, 2 * heads + h)),
                  cspec, cspec,
                  pl.BlockSpec((None, None, len(cases), nq * GRID_W, nk * GRID_W),
                               lambda b, h: (layer, h, 0, 0, 0))],
        out_specs=pl.BlockSpec((seq, HEAD_DIM), lambda b, h: (b, h)),
        out_shape=jax.ShapeDtypeStruct((batch * seq, heads * HEAD_DIM), BF16),
        scratch_shapes=[pltpu.VMEM((seq, 2 * HEAD_DIM), BF16), pltpu.VMEM((past, 2 * HEAD_DIM), BF16)],
        compiler_params=_params(2),
        name="latent_na",
    )(z, z, z, cache_k, cache_v, bias)


def _gqa_kernel(q_ref, k_ref, v_ref, ck_ref, cv_ref, qg_ref, kg_ref, cos_ref, sin_ref, o_ref, kf_ref, vf_ref,
                *, group, past, seq, tq):
    i = pl.program_id(2)

    @pl.when(i == 0)
    def _():
        kf_ref[0:past, :] = ck_ref[...].astype(BF16)
        vf_ref[0:past, :] = _with_ones(cv_ref[...].astype(BF16))
        k = _rope(_rms(k_ref[...].astype(F32), kg_ref[...]), cos_ref[...], sin_ref[...])
        kf_ref[past:past + seq, :] = k.astype(BF16)
        vf_ref[past:past + seq, :] = _with_ones(v_ref[...])

    t0 = pl.multiple_of(i * tq, tq)
    cos = cos_ref[pl.ds(t0, tq), :]
    sin = sin_ref[pl.ds(t0, tq), :]
    kf = kf_ref[...]
    vf = vf_ref[...]
    def scores(g):
        q = q_ref[:, g * HEAD_DIM:(g + 1) * HEAD_DIM].astype(F32)
        q = _rope(_rms(q, qg_ref[...]), cos, sin) * (ATTN_SCALE * LOG2E)
        return _dot(q.astype(BF16), kf, _NT)

    s = scores(0)
    for g in range(group):
        s_next = scores(g + 1) if g + 1 < group else None
        o_ref[:, g * HEAD_DIM:(g + 1) * HEAD_DIM] = _exp2_pv([s], [vf]).astype(o_ref.dtype)
        s = s_next


def _rope_tables(seq):
    t = np.arange(seq)
    half = HEAD_DIM // 2
    inv = 1.0 / (ROPE_THETA ** (np.arange(0, half, 2, dtype=np.float32) / half))
    row = (t // GRID_W).astype(np.float32)
    col = (t % GRID_W).astype(np.float32)
    ang = jnp.concatenate([jnp.asarray(row[:, None] * inv)] * 2 + [jnp.asarray(col[:, None] * inv)] * 2, axis=-1)
    sign = np.where((np.arange(HEAD_DIM) % half) < half // 2, -1.0, 1.0).astype(np.float32)
    return jnp.cos(ang), jnp.sin(ang) * sign


def _gqa_attention(z, batch, seq, kv_heads, group, q_col, k_col, v_col, cache_k, cache_v, layer, gains, tables):
    past = cache_k.shape[3]
    tq = min(256, seq)
    nq = seq // tq
    qw = group * HEAD_DIM
    cspec = pl.BlockSpec((None, None, None, past, HEAD_DIM), lambda b, h, i: (b, layer, h, 0, 0))
    gspec = pl.BlockSpec((1, HEAD_DIM), lambda b, h, i: (0, 0))
    tspec = pl.BlockSpec((seq, HEAD_DIM), lambda b, h, i: (0, 0))
    return pl.pallas_call(
        functools.partial(_gqa_kernel, group=group, past=past, seq=seq, tq=tq),
        grid=(batch, kv_heads, nq),
        in_specs=[pl.BlockSpec((tq, qw), lambda b, h, i: (b * nq + i, q_col // group + h)),
                  pl.BlockSpec((seq, HEAD_DIM), lambda b, h, i: (b, k_col + h)),
                  pl.BlockSpec((seq, HEAD_DIM), lambda b, h, i: (b, v_col + h)),
                  cspec, cspec, gspec, gspec, tspec, tspec],
        out_specs=pl.BlockSpec((tq, qw), lambda b, h, i: (b * nq + i, h)),
        out_shape=jax.ShapeDtypeStruct((batch * seq, kv_heads * qw), BF16),
        scratch_shapes=[pltpu.VMEM((past + seq, HEAD_DIM), BF16), pltpu.VMEM((past + seq, 2 * HEAD_DIM), BF16)],
        compiler_params=_params(3),
        name="latent_gqa",
    )(z, z, z, cache_k, cache_v, gains[0], gains[1], tables[0], tables[1])


def _outproj_kernel(x_ref, na_ref, gq_ref, rt_ref, w_ref, g1_ref, sh2_ref, sc2_ref, lg_ref, lb_ref, wr_ref,
                    x1_ref, h2_ref, lt_ref, *, alpha, sub):
    n_na = na_ref.shape[1]
    n_gq = gq_ref.shape[1]
    n_exp = lt_ref.shape[1]
    for r in range(x_ref.shape[0] // sub):
        rs = slice(r * sub, (r + 1) * sub)
        y = _dot(na_ref[rs, :], w_ref[0:n_na, :])
        y += _dot(gq_ref[rs, :], w_ref[n_na:n_na + n_gq, :])
        y += _dot(rt_ref[rs, :], w_ref[n_na + n_gq:, :])
        x1 = _ln(alpha * x_ref[rs, :] + g1_ref[...] * y) * lg_ref[...] + lb_ref[...]
        x1_ref[rs, :] = x1
        h2 = _ln(x1) * (1.0 + sc2_ref[...]) + sh2_ref[...]
        hi, lo = _split_bf16(h2)
        h2_ref[rs, :] = hi
        both = _dot(hi, wr_ref[...])
        lt_ref[rs, :] = both[:, :n_exp] + both[:, n_exp:] + _dot(lo, wr_ref[:, 0:n_exp])


def _out_projection(x, na_o, gq_o, rt_o, w_bf16, layer, mod3, ln_g, ln_b, wr_hl, tokens_per_batch, per_request,
                    alpha):
    rows, d = x.shape
    tm = 512
    per = tokens_per_batch // tm if per_request else 1
    n_exp = wr_hl.shape[1] // 2

    def mod_spec(chunk):
        return pl.BlockSpec((None, 1, d), lambda i: (_mod_row(i, per, per_request) * 6 + chunk, 0, 0))

    def row_spec(width):
        return pl.BlockSpec((tm, width), lambda i: (i, 0))

    vec = pl.BlockSpec((1, d), lambda i: (0, 0))
    return pl.pallas_call(
        functools.partial(_outproj_kernel, alpha=alpha, sub=256),
        grid=(rows // tm,),
        in_specs=[row_spec(d), row_spec(na_o.shape[1]), row_spec(gq_o.shape[1]), row_spec(rt_o.shape[1]),
                  pl.BlockSpec((None,) + w_bf16.shape[1:], lambda i: (layer, 0, 0)),
                  mod_spec(2), mod_spec(3), mod_spec(4), vec, vec,
                  pl.BlockSpec(wr_hl.shape, lambda i: (0, 0))],
        out_specs=[row_spec(d), row_spec(d), row_spec(n_exp)],
        out_shape=[jax.ShapeDtypeStruct((rows, d), F32), jax.ShapeDtypeStruct((rows, d), BF16),
                   jax.ShapeDtypeStruct((rows, n_exp), F32)],
        compiler_params=_params(1),
        name="outproj_postnorm_router",
    )(x, na_o, gq_o, rt_o, w_bf16, mod3, mod3, mod3, ln_g, ln_b, wr_hl)


def _topk_kernel(lt_ref, slot_ref, gate_ref, *, cap, seq):
    logits = lt_ref[...]
    n_exp = logits.shape[0]
    m = jnp.max(logits, axis=0, keepdims=True)
    ex = jnp.exp(logits - m)
    aff = ex / jnp.sum(ex, axis=0, keepdims=True)
    bits = lax.bitcast_convert_type(aff, jnp.int32)

    def count(mask):
        return jnp.sum(jnp.where(mask, 1.0, 0.0), axis=1, keepdims=True)

    def value_step(it, thr):
        cand = thr | jnp.left_shift(jnp.int32(1), 30 - it)
        return jnp.where(count(bits >= cand) >= cap, cand, thr)

    thr = lax.fori_loop(0, 31, value_step, jnp.zeros((n_exp, 1), jnp.int32))
    above = bits > thr
    tied = bits == thr
    need = cap - count(above)
    tok = lax.broadcasted_iota(jnp.int32, (n_exp, seq), 1)
    n_bits = int(seq - 1).bit_length()

    def index_step(it, bound):
        cand = bound | jnp.left_shift(jnp.int32(1), n_bits - 1 - it)
        return jnp.where(count(tied & (tok < cand)) < need, cand, bound)

    bound = lax.fori_loop(0, n_bits, index_step, jnp.zeros((n_exp, 1), jnp.int32))
    sel = above | (tied & (tok <= bound))
    self = jnp.where(sel, 1.0, 0.0)
    lanes = 128
    upper = jnp.where(lax.broadcasted_iota(jnp.int32, (lanes, lanes), 0)
                      < lax.broadcasted_iota(jnp.int32, (lanes, lanes), 1), 1.0, 0.0).astype(BF16)
    running = jnp.zeros((n_exp, 1), F32)
    for blk in range(seq // lanes):
        sl = slice(blk * lanes, (blk + 1) * lanes)
        chunk = self[:, sl]
        pos = _dot(chunk.astype(BF16), upper) + running
        slot_ref[:, sl] = jnp.where(chunk > 0.0, pos.astype(jnp.int32), -1)
        running = running + jnp.sum(chunk, axis=1, keepdims=True)
    gate_ref[...] = aff


def _route(logits_t, batch, seq, cap):
    n_exp = logits_t.shape[0]
    spec = pl.BlockSpec((None, n_exp, seq), lambda b: (b, 0, 0))
    return pl.pallas_call(
        functools.partial(_topk_kernel, cap=cap, seq=seq),
        grid=(batch,),
        in_specs=[pl.BlockSpec((n_exp, seq), lambda b: (0, b))],
        out_specs=[spec, spec],
        out_shape=[jax.ShapeDtypeStruct((batch, n_exp, seq), jnp.int32),
                   jax.ShapeDtypeStruct((batch, n_exp, seq), F32)],
        compiler_params=_params(1),
        name="route_topk",
    )(logits_t)


def _gather_kernel(*refs, n_inner, cap, aliased):
    if aliased:
        refs = refs[2:]
    slot_ref, gate_ref, h_ref, xs_ref, gs_ref = refs
    h = h_ref[...]
    seq = h.shape[0]
    row = lax.broadcasted_iota(jnp.int32, (cap, seq), 0)
    for e in range(n_inner):
        onehot = slot_ref[e:e + 1, :] == row
        xs_ref[e] = _dot(jnp.where(onehot, 1.0, 0.0).astype(BF16), h).astype(BF16)
        gs_ref[e] = jnp.sum(jnp.where(onehot, gate_ref[e:e + 1, :], 0.0), axis=1, keepdims=True)


def _gather(slot, gate, h, batch, seq, cap, n_inner, row_blk0, total_rows, prev=None):
    n_exp = slot.shape[1]
    d = h.shape[1]
    n_outer = n_exp // n_inner
    slot4 = slot.reshape(batch, n_outer, n_inner, seq)
    gate4 = gate.reshape(batch, n_outer, n_inner, seq)
    sspec = pl.BlockSpec((None, None, n_inner, seq), lambda b, e: (b, e, 0, 0))
    in_specs = [sspec, sspec, pl.BlockSpec((seq, d), lambda b, e: (b, 0))]
    args = [slot4, gate4, h]
    aliases = {}
    if prev is not None:
        in_specs = [pl.BlockSpec(memory_space=pl.ANY)] * 2 + in_specs
        args = list(prev) + args
        aliases = {0: 0, 1: 1}
    return pl.pallas_call(
        functools.partial(_gather_kernel, n_inner=n_inner, cap=cap, aliased=prev is not None),
        grid=(batch, n_outer),
        in_specs=in_specs,
        out_specs=[pl.BlockSpec((n_inner, cap, d), lambda b, e: (e, row_blk0 + b, 0)),
                   pl.BlockSpec((n_inner, cap, 1), lambda b, e: (e, row_blk0 + b, 0))],
        out_shape=[jax.ShapeDtypeStruct((n_exp, total_rows, d), BF16),
                   jax.ShapeDtypeStruct((n_exp, total_rows, 1), F32)],
        input_output_aliases=aliases,
        compiler_params=_params(2),
        name="moe_gather",
    )(*args)


def _ffn_kernel(xs_ref, gs_ref, wg_ref, wu_ref, wd_ref, y_ref):
    x = xs_ref[...]
    a = _dot(x, wg_ref[...])
    u = _dot(x, wu_ref[...])
    y = _dot((_silu(a) * u).astype(BF16), wd_ref[...])
    y_ref[...] = (y * gs_ref[...]).astype(y_ref.dtype)


def _expert_ffn(xs, gs, wg, wu, wd, layer):
    n_exp, rows, d = xs.shape
    ff = wg.shape[3]
    tm = math.gcd(rows, 512)
    return pl.pallas_call(
        _ffn_kernel,
        grid=(n_exp, rows // tm),
        in_specs=[pl.BlockSpec((None, tm, d), lambda e, i: (e, i, 0)),
                  pl.BlockSpec((None, tm, 1), lambda e, i: (e, i, 0)),
                  pl.BlockSpec((None, None, d, ff), lambda e, i: (layer, e, 0, 0)),
                  pl.BlockSpec((None, None, d, ff), lambda e, i: (layer, e, 0, 0)),
                  pl.BlockSpec((None, None, ff, d), lambda e, i: (layer, e, 0, 0))],
        out_specs=pl.BlockSpec((None, tm, d), lambda e, i: (e, i, 0)),
        out_shape=jax.ShapeDtypeStruct((n_exp, rows, d), BF16),
        compiler_params=_params(2),
        name="moe_ffn",
    )(xs, gs, wg, wu, wd)


def _combine_kernel(x_ref, y_ref, slot_ref, g2_ref, lg_ref, lb_ref, o_ref, *, alpha, cap):
    n_exp = y_ref.shape[0]
    tt = x_ref.shape[0]
    col = lax.broadcasted_iota(jnp.int32, (tt, cap), 1)
    slot = slot_ref[...]
    acc = jnp.zeros(x_ref.shape, F32)
    for e in range(n_exp):
        onehot = jnp.where(slot[:, e:e + 1] == col, 1.0, 0.0).astype(BF16)
        acc += _dot(onehot, y_ref[e])
    o_ref[...] = _ln(alpha * x_ref[...] + g2_ref[...] * acc) * lg_ref[...] + lb_ref[...]


def _combine(x1, y, slot_t, mod3, ln_g, ln_b, batch, seq, cap, row_blk0, per_request, alpha):
    rows, d = x1.shape
    n_exp = y.shape[0]
    tt = min(256, seq)
    nt = seq // tt
    vec = pl.BlockSpec((1, d), lambda b, i: (0, 0))
    return pl.pallas_call(
        functools.partial(_combine_kernel, alpha=alpha, cap=cap),
        grid=(batch, nt),
        in_specs=[pl.BlockSpec((tt, d), lambda b, i: (b * nt + i, 0)),
                  pl.BlockSpec((n_exp, cap, d), lambda b, i: (0, row_blk0 + b, 0)),
                  pl.BlockSpec((None, tt, n_exp), lambda b, i: (b, i, 0)),
                  pl.BlockSpec((None, 1, d), lambda b, i: (_mod_row(b, 1, per_request) * 6 + 5, 0, 0)),
                  vec, vec],
        out_specs=pl.BlockSpec((tt, d), lambda b, i: (b * nt + i, 0)),
        out_shape=jax.ShapeDtypeStruct((rows, d), F32),
        compiler_params=_params(2),
        name="moe_combine_postnorm",
    )(x1, y, slot_t, mod3, ln_g, ln_b)


def kernel(x_prompt, x_sample, cache_na_k, cache_na_v, cache_gqa_k, cache_gqa_v, state_ret_fwd, state_ret_bwd,
           c, c_ctx, w_in, w_out, w_mod, b_mod, ln1_g, ln1_b, ln2_g, ln2_b, q_norm_g, k_norm_g, na_rpb,
           ret_decay_fwd, ret_decay_bwd, w_router, w_gate, w_up, w_down):
    batch, seq, d = x_prompt.shape
    dbatch, dseq, _ = x_sample.shape
    depth = w_in.shape[0]
    na_heads = cache_na_k.shape[2]
    kv_heads = cache_gqa_k.shape[2]
    ret_heads = state_ret_fwd.shape[2]
    n_exp = w_router.shape[2]
    hd = HEAD_DIM
    gqa_heads = (w_in.shape[2] // hd - 3 * na_heads - 2 * kv_heads - 4 * ret_heads)
    group = gqa_heads // kv_heads
    alpha = float((2 * depth) ** 0.25)
    c_na = 0
    c_gq = 3 * na_heads
    c_gk = c_gq + gqa_heads
    c_gv = c_gk + kv_heads
    c_rt = c_gv + kv_heads
    cap_c = EC_CAPACITY_FACTOR * seq // n_exp
    cap_l = EC_CAPACITY_FACTOR * dseq // n_exp
    assert dbatch + 1 <= MOD_ROWS and (batch * cap_c) % cap_l == 0
    lat_blk0 = batch * cap_c // cap_l
    rows_per_expert = batch * cap_c + dbatch * cap_l

    cond = jnp.zeros((MOD_ROWS, d), F32).at[0].set(c_ctx).at[1:1 + dbatch].set(c)
    mod_all = _modulation(cond, w_mod, b_mod)
    tables = _rope_tables(dseq)
    bias = _na_bias(na_rpb, dseq // GRID_W)

    w_in_b, w_out_b = w_in.astype(BF16), w_out.astype(BF16)
    wg_b, wu_b, wd_b = w_gate.astype(BF16), w_up.astype(BF16), w_down.astype(BF16)

    xp = x_prompt.reshape(batch * seq, d)
    xs = x_sample.reshape(dbatch * dseq, d)
    outs = [[] for _ in range(6)]
    for l in range(depth):
        mod3 = mod_all[l].reshape(MOD_ROWS * 6, 1, d)
        wr_t = jnp.concatenate(_split_bf16(w_router[l]), axis=1)
        gains = (q_norm_g[l].reshape(1, hd), k_norm_g[l].reshape(1, hd))
        l1g, l1b = ln1_g[l].reshape(1, d), ln1_b[l].reshape(1, d)
        l2g, l2b = ln2_g[l].reshape(1, d), ln2_b[l].reshape(1, d)

        zc = _in_projection(xp, mod3, w_in_b, l, seq, False, F32)
        (na_c,) = _ctx_attention(zc, batch, seq, na_heads, 1, c_na, c_na + na_heads, c_na + 2 * na_heads)
        gq_c, gk_n = _ctx_attention(zc, batch, seq, kv_heads, group, c_gq, c_gk, c_gv, gains)
        rt_c, s_f, s_b = _retention(zc, batch, seq, ret_heads, c_rt, ret_decay_fwd[l], ret_decay_bwd[l],
                                    state_out=True)

        def heads_of(col, n):
            return zc[:, col * hd:(col + n) * hd].reshape(batch, seq, n, hd).transpose(0, 2, 1, 3)

        outs[0].append(heads_of(c_na + na_heads, na_heads))
        outs[1].append(heads_of(c_na + 2 * na_heads, na_heads))
        outs[2].append(gk_n)
        outs[3].append(heads_of(c_gv, kv_heads))
        outs[4].append(s_f)
        outs[5].append(s_b)

        zl = _in_projection(xs, mod3, w_in_b, l, dseq, True, BF16)
        (na_l,) = [_na_attention(zl, dbatch, dseq, na_heads, cache_na_k, cache_na_v, l, bias)]
        gq_l = _gqa_attention(zl, dbatch, dseq, kv_heads, group, c_gq, c_gk, c_gv, cache_gqa_k, cache_gqa_v, l,
                              gains, tables)
        (rt_l,) = _retention(zl, dbatch, dseq, ret_heads, c_rt, ret_decay_fwd[l], ret_decay_bwd[l],
                             states=((state_ret_fwd, state_ret_bwd), l))

        xp1, hp, lt_c = _out_projection(xp, na_c, gq_c, rt_c, w_out_b, l, mod3, l1g, l1b, wr_t, seq, False, alpha)
        xs1, hs, lt_l = _out_projection(xs, na_l, gq_l, rt_l, w_out_b, l, mod3, l1g, l1b, wr_t, dseq, True, alpha)

        slot_c, gate_c = _route(lt_c.T, batch, seq, cap_c)
        slot_l, gate_l = _route(lt_l.T, dbatch, dseq, cap_l)
        gathered = _gather(slot_l, gate_l, hs, dbatch, dseq, cap_l, 2, lat_blk0, rows_per_expert)
        gathered = _gather(slot_c, gate_c, hp, batch, seq, cap_c, n_exp, 0, rows_per_expert, prev=gathered)
        y = _expert_ffn(gathered[0], gathered[1], wg_b, wu_b, wd_b, l)
        xp = _combine(xp1, y, slot_c.transpose(0, 2, 1), mod3, l2g, l2b, batch, seq, cap_c, 0, False, alpha)
        xs = _combine(xs1, y, slot_l.transpose(0, 2, 1), mod3, l2g, l2b, dbatch, dseq, cap_l, lat_blk0, True, alpha)

    stacked = [jnp.stack(o, axis=1) for o in outs]
    return (xp.reshape(batch, seq, d), xs.reshape(dbatch, dseq, d), *stacked)
```

```python
import functools
import math

import numpy as np
import jax
import jax.numpy as jnp
from jax import lax
from jax.experimental import pallas as pl
from jax.experimental.pallas import tpu as pltpu

F32 = jnp.float32
BF16 = jnp.bfloat16

HEAD_DIM = 128
GRID_W = 64
NA_WIN_R = 8
NA_WIN_C = 16
ROPE_THETA = 10000.0
EC_CAPACITY_FACTOR = 2
LN_EPS = 1e-5
RMS_EPS = 1e-6
NEG_INF = -1e30
ATTN_SCALE = HEAD_DIM ** -0.5
LOG2E = 1.4426950408889634
NA_BLOCK_ROWS = 4
MOE_CHUNK = 256
MOE_WINDOW = 64
MOD_ROWS = 16
VMEM_LIMIT = 56 * 1024 * 1024

_NT = (((1,), (1,)), ((), ()))
_NN = (((1,), (0,)), ((), ()))


def _params(n_grid, vmem=VMEM_LIMIT):
    return pltpu.CompilerParams(dimension_semantics=("arbitrary",) * n_grid, vmem_limit_bytes=vmem)


def _dot(a, b, dims=_NN):
    return lax.dot_general(a, b, dims, preferred_element_type=F32)


def _split_bf16(x):
    hi = x.astype(BF16)
    lo = (x - hi.astype(F32)).astype(BF16)
    return hi, lo


def _dot3(a, b, dims=_NN):
    ah, al = _split_bf16(a)
    bh, bl = _split_bf16(b)
    return _dot(ah, bh, dims) + _dot(al, bh, dims) + _dot(ah, bl, dims)


def _silu(x):
    return x / (1.0 + jnp.exp(-x))


def _ln(x):
    mu = jnp.mean(x, axis=-1, keepdims=True)
    xc = x - mu
    var = jnp.mean(xc * xc, axis=-1, keepdims=True)
    return xc * lax.rsqrt(var + LN_EPS)


def _rms(x, g):
    return x * lax.rsqrt(jnp.mean(x * x, axis=-1, keepdims=True) + RMS_EPS) * g


def _rope(x, cos, sin_signed):
    lane = lax.broadcasted_iota(jnp.int32, x.shape, 1) & (HEAD_DIM // 2 - 1)
    partner = jnp.where(lane < HEAD_DIM // 4,
                        pltpu.roll(x, HEAD_DIM - HEAD_DIM // 4, 1),
                        pltpu.roll(x, HEAD_DIM // 4, 1))
    return x * cos + partner * sin_signed


def _softmax_pv(scores, values):
    m = functools.reduce(jnp.maximum, [jnp.max(s, axis=-1, keepdims=True) for s in scores])
    ps = [jnp.exp(s - m) for s in scores]
    denom = functools.reduce(lambda a, b: a + b, [jnp.sum(p, axis=-1, keepdims=True) for p in ps])
    o = functools.reduce(lambda a, b: a + b, [_dot(p.astype(BF16), v) for p, v in zip(ps, values)])
    return o / denom


def _mod_kernel(c_ref, w_ref, b_ref, o_ref):
    a = _silu(c_ref[...])
    o_ref[...] = _dot3(a, w_ref[...]) + b_ref[...]


def _modulation(cond, w_mod, b_mod):
    depth, d, n = w_mod.shape
    tn = 768
    return pl.pallas_call(
        _mod_kernel,
        grid=(depth, n // tn),
        in_specs=[pl.BlockSpec((MOD_ROWS, d), lambda l, j: (0, 0)),
                  pl.BlockSpec((None, d, tn), lambda l, j: (l, 0, j)),
                  pl.BlockSpec((None, 1, tn), lambda l, j: (l, 0, j))],
        out_specs=pl.BlockSpec((None, MOD_ROWS, tn), lambda l, j: (l, 0, j)),
        out_shape=jax.ShapeDtypeStruct((depth, MOD_ROWS, n), F32),
        compiler_params=_params(2),
        name="adaln_mod",
    )(cond, w_mod, b_mod.reshape(depth, 1, n))


def _inproj_kernel(x_ref, sh_ref, sc_ref, w_ref, z_ref, *, sub):
    for r in range(x_ref.shape[0] // sub):
        rs = slice(r * sub, (r + 1) * sub)
        h = _ln(x_ref[rs, :]) * (1.0 + sc_ref[...]) + sh_ref[...]
        z_ref[rs, :] = _dot(h.astype(BF16), w_ref[...]).astype(z_ref.dtype)


def _mod_row(block, blocks_per_request, per_request):
    return 1 + block // blocks_per_request if per_request else 0


def _in_projection(x, mod3, w_bf16, layer, tokens_per_batch, per_request, out_dtype):
    rows, d = x.shape
    n = w_bf16.shape[2]
    tm = min(512, tokens_per_batch)
    tn = 2560
    per = tokens_per_batch // tm

    def mod_spec(chunk):
        return pl.BlockSpec((None, 1, d), lambda j, i: (_mod_row(i, per, per_request) * 6 + chunk, 0, 0))

    return pl.pallas_call(
        functools.partial(_inproj_kernel, sub=min(256, tm)),
        grid=(n // tn, rows // tm),
        in_specs=[pl.BlockSpec((tm, d), lambda j, i: (i, 0)),
                  mod_spec(0), mod_spec(1),
                  pl.BlockSpec((None, d, tn), lambda j, i: (layer, 0, j))],
        out_specs=pl.BlockSpec((tm, tn), lambda j, i: (i, j)),
        out_shape=jax.ShapeDtypeStruct((rows, n), out_dtype),
        compiler_params=_params(2),
        name="ln_inproj",
    )(x, mod3, mod3, w_bf16)


def _ctx_attn_kernel(*refs, group, normed):
    if normed:
        q_ref, k_ref, v_ref, qg_ref, kg_ref, o_ref, kn_ref = refs
    else:
        q_ref, k_ref, v_ref, o_ref = refs
    k = k_ref[...]
    if normed:
        k = _rms(k, kg_ref[...])
        kn_ref[...] = k
    kb = k.astype(BF16)
    vb = v_ref[...].astype(BF16)
    for g in range(group):
        q = q_ref[:, g * HEAD_DIM:(g + 1) * HEAD_DIM]
        if normed:
            q = _rms(q, qg_ref[...])
        s = _dot(q.astype(BF16), kb, _NT) * ATTN_SCALE
        o_ref[:, g * HEAD_DIM:(g + 1) * HEAD_DIM] = _softmax_pv([s], [vb]).astype(o_ref.dtype)


def _ctx_attention(z, batch, seq, heads, group, q_col, k_col, v_col, gains=None):
    normed = gains is not None
    qw = group * HEAD_DIM
    in_specs = [pl.BlockSpec((seq, qw), lambda b, h: (b, q_col // group + h)),
                pl.BlockSpec((seq, HEAD_DIM), lambda b, h: (b, k_col + h)),
                pl.BlockSpec((seq, HEAD_DIM), lambda b, h: (b, v_col + h))]
    args = [z, z, z]
    out_specs = [pl.BlockSpec((seq, qw), lambda b, h: (b, h))]
    out_shape = [jax.ShapeDtypeStruct((batch * seq, heads * qw), BF16)]
    if normed:
        in_specs += [pl.BlockSpec((1, HEAD_DIM), lambda b, h: (0, 0))] * 2
        args += list(gains)
        out_specs.append(pl.BlockSpec((None, None, seq, HEAD_DIM), lambda b, h: (b, h, 0, 0)))
        out_shape.append(jax.ShapeDtypeStruct((batch, heads, seq, HEAD_DIM), F32))
    return pl.pallas_call(
        functools.partial(_ctx_attn_kernel, group=group, normed=normed),
        grid=(batch, heads),
        in_specs=in_specs, out_specs=out_specs, out_shape=out_shape,
        compiler_params=_params(2),
        name="ctx_gqa" if normed else "ctx_na",
    )(*args)


def _log_sigmoid(x):
    return -jnp.log1p(jnp.exp(-x))


def _retention_kernel(*refs, seq, tq, with_state_in, with_state_out):
    q_ref, k_ref, v_ref, g_ref, df_ref, db_ref = refs[:6]
    refs = refs[6:]
    if with_state_in:
        sf_ref, sb_ref = refs[:2]
        refs = refs[2:]
    o_ref = refs[0]
    decay_ref = refs[-1]
    lgf = _log_sigmoid(df_ref[...])
    lgb = _log_sigmoid(db_ref[...])
    t0 = pl.program_id(1) * tq

    @pl.when(pl.program_id(2) == 0)
    def _():
        i = t0 + lax.broadcasted_iota(jnp.int32, (tq, seq), 0)
        j = lax.broadcasted_iota(jnp.int32, (tq, seq), 1)
        dist = (i - j).astype(F32)
        decay = jnp.where(dist == 0.0, 2.0, jnp.exp(jnp.where(dist > 0.0, lgf * dist, -lgb * dist)))
        decay_ref[...] = decay * ATTN_SCALE

    q = q_ref[...]
    kb = k_ref[...].astype(BF16)
    vb = v_ref[...].astype(BF16)
    att = _dot(q.astype(BF16), kb, _NT) * decay_ref[...]
    o = _dot(att.astype(BF16), vb)
    if with_state_in:
        t = (t0 + lax.broadcasted_iota(jnp.int32, (tq, 1), 0)).astype(F32)
        qf = q.astype(F32)
        o += _dot((qf * jnp.exp(lgf * (t + 1.0))).astype(BF16), sf_ref[...].astype(BF16))
        o += _dot((qf * jnp.exp(lgb * (seq - t))).astype(BF16), sb_ref[...].astype(BF16))
    o_ref[...] = (_ln(o) * _silu(g_ref[...].astype(F32))).astype(o_ref.dtype)
    if with_state_out:
        nsf_ref, nsb_ref = refs[1:3]
        tj = lax.broadcasted_iota(jnp.int32, (seq, 1), 0).astype(F32)
        kf = k_ref[...].astype(F32) * ATTN_SCALE
        kdf = (kf * jnp.exp(lgf * (seq - 1.0 - tj))).T.astype(BF16)
        kdb = (kf * jnp.exp(lgb * tj)).T.astype(BF16)
        nsf_ref[...] = _dot(kdf, vb)
        nsb_ref[...] = _dot(kdb, vb)


def _retention(z, batch, seq, heads, col0, dec_f, dec_b, states=None, state_out=False):
    tq = min(512, seq)
    nq = seq // tq
    assert not (state_out and nq != 1)
    in_specs = [pl.BlockSpec((tq, HEAD_DIM), lambda h, i, b: (b * nq + i, col0 + h)),
                pl.BlockSpec((seq, HEAD_DIM), lambda h, i, b: (b, col0 + heads + h)),
                pl.BlockSpec((seq, HEAD_DIM), lambda h, i, b: (b, col0 + 2 * heads + h)),
                pl.BlockSpec((tq, HEAD_DIM), lambda h, i, b: (b * nq + i, col0 + 3 * heads + h)),
                pl.BlockSpec((None, 1, 1), lambda h, i, b: (h, 0, 0)),
                pl.BlockSpec((None, 1, 1), lambda h, i, b: (h, 0, 0))]
    args = [z, z, z, z, dec_f.reshape(heads, 1, 1), dec_b.reshape(heads, 1, 1)]
    if states is not None:
        (sf, sb), layer = states
        spec = pl.BlockSpec((None, None, None, HEAD_DIM, HEAD_DIM), lambda h, i, b: (b, layer, h, 0, 0))
        in_specs += [spec, spec]
        args += [sf, sb]
    out_specs = [pl.BlockSpec((tq, HEAD_DIM), lambda h, i, b: (b * nq + i, h))]
    out_shape = [jax.ShapeDtypeStruct((batch * seq, heads * HEAD_DIM), BF16)]
    if state_out:
        spec = pl.BlockSpec((None, None, HEAD_DIM, HEAD_DIM), lambda h, i, b: (b, h, 0, 0))
        out_specs += [spec, spec]
        out_shape += [jax.ShapeDtypeStruct((batch, heads, HEAD_DIM, HEAD_DIM), F32)] * 2
    return pl.pallas_call(
        functools.partial(_retention_kernel, seq=seq, tq=tq, with_state_in=states is not None,
                          with_state_out=state_out),
        grid=(heads, nq, batch),
        in_specs=in_specs, out_specs=out_specs, out_shape=out_shape,
        scratch_shapes=[pltpu.VMEM((tq, seq), F32)],
        compiler_params=_params(3),
        name="retention",
    )(*args)


def _with_ones(v):
    return jnp.concatenate([v, jnp.ones(v.shape, v.dtype)], axis=1)


def _exp2_pv(scores, values_with_ones):
    m = functools.reduce(jnp.maximum, [jnp.max(s, axis=-1, keepdims=True) for s in scores])
    r = functools.reduce(lambda a, b: a + b,
                         [_dot(jnp.exp2(s - m).astype(BF16), v) for s, v in zip(scores, values_with_ones)])
    return r[:, :HEAD_DIM] / r[:, HEAD_DIM:HEAD_DIM + 1]


def _na_plan(rows):
    kr = min(NA_WIN_R, rows)
    nq = min(NA_BLOCK_ROWS, rows)
    nk = min(nq + kr - 1, rows)
    assert rows % nq == 0
    blocks, cases = [], []
    for r0 in range(0, rows, nq):
        start = int(np.clip(r0 - kr // 2, 0, rows - nk))
        win = [int(np.clip(r - kr // 2, 0, rows - kr)) for r in range(r0, r0 + nq)]
        assert all(start <= w and w + kr <= start + nk for w in win)
        key = (start - r0,) + tuple(w - start for w in win)
        if key not in cases:
            cases.append(key)
        blocks.append((r0, start, cases.index(key)))
    return kr, nq, nk, blocks, cases


def _na_kernel(q_ref, k_ref, v_ref, ck_ref, cv_ref, bias_ref, o_ref, vx_ref, cvx_ref, *, blocks, nq, nk):
    vx_ref[...] = _with_ones(v_ref[...])
    cvx_ref[...] = _with_ones(cv_ref[...].astype(BF16))
    ck = ck_ref[...].astype(BF16)
    c = ATTN_SCALE * LOG2E
    for r0, start, case in blocks:
        qs = slice(r0 * GRID_W, (r0 + nq) * GRID_W)
        ks = slice(start * GRID_W, (start + nk) * GRID_W)
        q = q_ref[qs, :]
        s_loc = _dot(q, k_ref[ks, :], _NT) * c + bias_ref[case]
        s_ctx = _dot(q, ck, _NT) * c
        o_ref[qs, :] = _exp2_pv([s_loc, s_ctx], [vx_ref[ks, :], cvx_ref[...]]).astype(o_ref.dtype)


def _na_bias(rpb, rows):
    kr, nq, nk, _, cases = _na_plan(rows)
    dr = np.zeros((len(cases), nq, nk), np.int32)
    row_ok = np.zeros((len(cases), nq, nk), bool)
    for c, key in enumerate(cases):
        rel, offs = key[0], key[1:]
        for ri in range(nq):
            for ju in range(nk):
                row_ok[c, ri, ju] = 0 <= ju - offs[ri] < kr
                dr[c, ri, ju] = np.clip(rel + ju - ri + NA_WIN_R - 1, 0, 2 * NA_WIN_R - 2)
    cq = np.arange(GRID_W)
    ck = np.arange(GRID_W)
    col_start = np.clip(cq - NA_WIN_C // 2, 0, GRID_W - NA_WIN_C)
    col_ok = (ck[None, :] >= col_start[:, None]) & (ck[None, :] < col_start[:, None] + NA_WIN_C)
    dc = np.clip(ck[None, :] - cq[:, None] + (NA_WIN_C - 1), 0, 2 * NA_WIN_C - 2)
    pick_col = (dc.reshape(-1)[None, :] == np.arange(2 * NA_WIN_C - 1)[:, None]).astype(np.float32)
    n_l, n_h = rpb.shape[:2]
    t = jnp.take(rpb.astype(F32), dr.reshape(-1), axis=2)
    t = jnp.einsum("lhxb,bn->lhxn", t, pick_col, precision=lax.Precision.HIGHEST)
    t = t.reshape(n_l, n_h, len(cases), nq, nk, GRID_W, GRID_W).transpose(0, 1, 2, 3, 5, 4, 6)
    ok = row_ok[:, :, None, :, None] & col_ok[None, None, :, None, :]
    t = jnp.where(ok, t * LOG2E, NEG_INF)
    return t.reshape(n_l, n_h, len(cases), nq * GRID_W, nk * GRID_W)


def _na_attention(z, batch, seq, heads, cache_k, cache_v, layer, bias):
    _, nq, nk, blocks, cases = _na_plan(seq // GRID_W)
    past = cache_k.shape[3]
    cspec = pl.BlockSpec((None, None, None, past, HEAD_DIM), lambda b, h: (b, layer, h, 0, 0))
    return pl.pallas_call(
        functools.partial(_na_kernel, blocks=blocks, nq=nq, nk=nk),
        grid=(batch, heads),
        in_specs=[pl.BlockSpec((seq, HEAD_DIM), lambda b, h: (b, h)),
                  pl.BlockSpec((seq, HEAD_DIM), lambda b, h: (b, heads + h)),
                  pl.BlockSpec((seq, HEAD_DIM), lambda b, h: (b, 2 * heads + h)),
                  cspec, cspec,
                  pl.BlockSpec((None, None, len(cases), nq * GRID_W, nk * GRID_W),
                               lambda b, h: (layer, h, 0, 0, 0))],
        out_specs=pl.BlockSpec((seq, HEAD_DIM), lambda b, h: (b, h)),
        out_shape=jax.ShapeDtypeStruct((batch * seq, heads * HEAD_DIM), BF16),
        scratch_shapes=[pltpu.VMEM((seq, 2 * HEAD_DIM), BF16), pltpu.VMEM((past, 2 * HEAD_DIM), BF16)],
        compiler_params=_params(2),
        name="latent_na",
    )(z, z, z, cache_k, cache_v, bias)


def _gqa_kernel(q_ref, k_ref, v_ref, ck_ref, cv_ref, qg_ref, kg_ref, cos_ref, sin_ref, o_ref, kf_ref, vf_ref,
                *, group, past, seq, tq):
    i = pl.program_id(2)

    @pl.when(i == 0)
    def _():
        kf_ref[0:past, :] = ck_ref[...].astype(BF16)
        vf_ref[0:past, :] = _with_ones(cv_ref[...].astype(BF16))
        k = _rope(_rms(k_ref[...].astype(F32), kg_ref[...]), cos_ref[...], sin_ref[...])
        kf_ref[past:past + seq, :] = k.astype(BF16)
        vf_ref[past:past + seq, :] = _with_ones(v_ref[...])

    t0 = pl.multiple_of(i * tq, tq)
    cos = cos_ref[pl.ds(t0, tq), :]
    sin = sin_ref[pl.ds(t0, tq), :]
    kf = kf_ref[...]
    vf = vf_ref[...]
    def scores(g):
        q = q_ref[:, g * HEAD_DIM:(g + 1) * HEAD_DIM].astype(F32)
        q = _rope(_rms(q, qg_ref[...]), cos, sin) * (ATTN_SCALE * LOG2E)
        return _dot(q.astype(BF16), kf, _NT)

    s = scores(0)
    for g in range(group):
        s_next = scores(g + 1) if g + 1 < group else None
        o_ref[:, g * HEAD_DIM:(g + 1) * HEAD_DIM] = _exp2_pv([s], [vf]).astype(o_ref.dtype)
        s = s_next


def _rope_tables(seq):
    t = np.arange(seq)
    half = HEAD_DIM // 2
    inv = 1.0 / (ROPE_THETA ** (np.arange(0, half, 2, dtype=np.float32) / half))
    row = (t // GRID_W).astype(np.float32)
    col = (t % GRID_W).astype(np.float32)
    ang = jnp.concatenate([jnp.asarray(row[:, None] * inv)] * 2 + [jnp.asarray(col[:, None] * inv)] * 2, axis=-1)
    sign = np.where((np.arange(HEAD_DIM) % half) < half // 2, -1.0, 1.0).astype(np.float32)
    return jnp.cos(ang), jnp.sin(ang) * sign


def _gqa_attention(z, batch, seq, kv_heads, group, q_col, k_col, v_col, cache_k, cache_v, layer, gains, tables):
    past = cache_k.shape[3]
    tq = min(256, seq)
    nq = seq // tq
    qw = group * HEAD_DIM
    cspec = pl.BlockSpec((None, None, None, past, HEAD_DIM), lambda b, h, i: (b, layer, h, 0, 0))
    gspec = pl.BlockSpec((1, HEAD_DIM), lambda b, h, i: (0, 0))
    tspec = pl.BlockSpec((seq, HEAD_DIM), lambda b, h, i: (0, 0))
    return pl.pallas_call(
        functools.partial(_gqa_kernel, group=group, past=past, seq=seq, tq=tq),
        grid=(batch, kv_heads, nq),
        in_specs=[pl.BlockSpec((tq, qw), lambda b, h, i: (b * nq + i, q_col // group + h)),
                  pl.BlockSpec((seq, HEAD_DIM), lambda b, h, i: (b, k_col + h)),
                  pl.BlockSpec((seq, HEAD_DIM), lambda b, h, i: (b, v_col + h)),
                  cspec, cspec, gspec, gspec, tspec, tspec],
        out_specs=pl.BlockSpec((tq, qw), lambda b, h, i: (b * nq + i, h)),
        out_shape=jax.ShapeDtypeStruct((batch * seq, kv_heads * qw), BF16),
        scratch_shapes=[pltpu.VMEM((past + seq, HEAD_DIM), BF16), pltpu.VMEM((past + seq, 2 * HEAD_DIM), BF16)],
        compiler_params=_params(3),
        name="latent_gqa",
    )(z, z, z, cache_k, cache_v, gains[0], gains[1], tables[0], tables[1])


def _outproj_kernel(x_ref, na_ref, gq_ref, rt_ref, w_ref, g1_ref, sh2_ref, sc2_ref, lg_ref, lb_ref, wr_ref,
                    x1_ref, h2_ref, lt_ref, *, alpha, sub):
    n_na = na_ref.shape[1]
    n_gq = gq_ref.shape[1]
    n_exp = lt_ref.shape[1]
    for r in range(x_ref.shape[0] // sub):
        rs = slice(r * sub, (r + 1) * sub)
        y = _dot(na_ref[rs, :], w_ref[0:n_na, :])
        y += _dot(gq_ref[rs, :], w_ref[n_na:n_na + n_gq, :])
        y += _dot(rt_ref[rs, :], w_ref[n_na + n_gq:, :])
        x1 = _ln(alpha * x_ref[rs, :] + g1_ref[...] * y) * lg_ref[...] + lb_ref[...]
        x1_ref[rs, :] = x1
        h2 = _ln(x1) * (1.0 + sc2_ref[...]) + sh2_ref[...]
        hi, lo = _split_bf16(h2)
        h2_ref[rs, :] = hi
        both = _dot(hi, wr_ref[...])
        lt_ref[rs, :] = both[:, :n_exp] + both[:, n_exp:] + _dot(lo, wr_ref[:, 0:n_exp])


def _out_projection(x, na_o, gq_o, rt_o, w_bf16, layer, mod3, ln_g, ln_b, wr_hl, tokens_per_batch, per_request,
                    alpha):
    rows, d = x.shape
    tm = 512
    per = tokens_per_batch // tm if per_request else 1
    n_exp = wr_hl.shape[1] // 2

    def mod_spec(chunk):
        return pl.BlockSpec((None, 1, d), lambda i: (_mod_row(i, per, per_request) * 6 + chunk, 0, 0))

    def row_spec(width):
        return pl.BlockSpec((tm, width), lambda i: (i, 0))

    vec = pl.BlockSpec((1, d), lambda i: (0, 0))
    return pl.pallas_call(
        functools.partial(_outproj_kernel, alpha=alpha, sub=256),
        grid=(rows // tm,),
        in_specs=[row_spec(d), row_spec(na_o.shape[1]), row_spec(gq_o.shape[1]), row_spec(rt_o.shape[1]),
                  pl.BlockSpec((None,) + w_bf16.shape[1:], lambda i: (layer, 0, 0)),
                  mod_spec(2), mod_spec(3), mod_spec(4), vec, vec,
                  pl.BlockSpec(wr_hl.shape, lambda i: (0, 0))],
        out_specs=[row_spec(d), row_spec(d), row_spec(n_exp)],
        out_shape=[jax.ShapeDtypeStruct((rows, d), F32), jax.ShapeDtypeStruct((rows, d), BF16),
                   jax.ShapeDtypeStruct((rows, n_exp), F32)],
        compiler_params=_params(1),
        name="outproj_postnorm_router",
    )(x, na_o, gq_o, rt_o, w_bf16, mod3, mod3, mod3, ln_g, ln_b, wr_hl)


def _topk_kernel(lt_ref, slot_ref, gate_ref, before_ref, *, cap, seq):
    logits = lt_ref[...]
    n_exp = logits.shape[0]
    m = jnp.max(logits, axis=0, keepdims=True)
    ex = jnp.exp(logits - m)
    aff = ex / jnp.sum(ex, axis=0, keepdims=True)
    bits = lax.bitcast_convert_type(aff, jnp.int32)

    def count(mask):
        return jnp.sum(jnp.where(mask, 1.0, 0.0), axis=1, keepdims=True)

    def value_step(it, thr):
        cand = thr | jnp.left_shift(jnp.int32(1), 30 - it)
        return jnp.where(count(bits >= cand) >= cap, cand, thr)

    thr = lax.fori_loop(0, 31, value_step, jnp.zeros((n_exp, 1), jnp.int32))
    above = bits > thr
    tied = bits == thr
    need = cap - count(above)
    tok = lax.broadcasted_iota(jnp.int32, (n_exp, seq), 1)
    n_bits = int(seq - 1).bit_length()

    def index_step(it, bound):
        cand = bound | jnp.left_shift(jnp.int32(1), n_bits - 1 - it)
        return jnp.where(count(tied & (tok < cand)) < need, cand, bound)

    bound = lax.fori_loop(0, n_bits, index_step, jnp.zeros((n_exp, 1), jnp.int32))
    sel = above | (tied & (tok <= bound))
    self = jnp.where(sel, 1.0, 0.0)
    lanes = 128
    upper = jnp.where(lax.broadcasted_iota(jnp.int32, (lanes, lanes), 0)
                      < lax.broadcasted_iota(jnp.int32, (lanes, lanes), 1), 1.0, 0.0).astype(BF16)
    running = jnp.zeros((n_exp, 1), F32)
    lane = lax.broadcasted_iota(jnp.int32, (n_exp, lanes), 1)
    before = jnp.zeros((n_exp, lanes), F32)
    for blk in range(seq // lanes):
        sl = slice(blk * lanes, (blk + 1) * lanes)
        chunk = self[:, sl]
        before = jnp.where(lane == blk, running, before)
        pos = _dot(chunk.astype(BF16), upper) + running
        slot_ref[:, sl] = jnp.where(chunk > 0.0, pos.astype(jnp.int32), -1)
        running = running + jnp.sum(chunk, axis=1, keepdims=True)
    gate_ref[...] = aff
    before_ref[...] = before.astype(jnp.int32)


def _route(logits_t, batch, seq, cap):
    n_exp = logits_t.shape[0]
    assert seq // 128 <= 128
    spec = pl.BlockSpec((None, n_exp, seq), lambda b: (b, 0, 0))
    return pl.pallas_call(
        functools.partial(_topk_kernel, cap=cap, seq=seq),
        grid=(batch,),
        in_specs=[pl.BlockSpec((n_exp, seq), lambda b: (0, b))],
        out_specs=[spec, spec, pl.BlockSpec((None, n_exp, 128), lambda b: (b, 0, 0))],
        out_shape=[jax.ShapeDtypeStruct((batch, n_exp, seq), jnp.int32),
                   jax.ShapeDtypeStruct((batch, n_exp, seq), F32),
                   jax.ShapeDtypeStruct((batch, n_exp, 128), jnp.int32)],
        compiler_params=_params(1),
        name="route_topk",
    )(logits_t)


def _window_plan(before, cap, seq):
    per = MOE_CHUNK // 128
    lo = before[:, :, 0:seq // 128:per]
    hi = jnp.concatenate([lo[:, :, 1:], jnp.full(lo.shape[:2] + (1,), cap, jnp.int32)], axis=2)
    base = jnp.minimum(lo // 16 * 16, cap - MOE_WINDOW)
    overflow = jnp.any(hi - base > MOE_WINDOW, axis=1)
    return base.transpose(0, 2, 1).reshape(-1), overflow.astype(jnp.int32).reshape(-1)


def _gather_kernel(*refs, n_inner, cap, aliased):
    if aliased:
        refs = refs[2:]
    slot_ref, gate_ref, h_ref, xs_ref, gs_ref = refs
    h = h_ref[...]
    seq = h.shape[0]
    row = lax.broadcasted_iota(jnp.int32, (cap, seq), 0)
    for e in range(n_inner):
        onehot = slot_ref[e:e + 1, :] == row
        xs_ref[e] = _dot(jnp.where(onehot, 1.0, 0.0).astype(BF16), h).astype(BF16)
        gs_ref[e] = jnp.sum(jnp.where(onehot, gate_ref[e:e + 1, :], 0.0), axis=1, keepdims=True)


def _gather(slot, gate, h, batch, seq, cap, n_inner, row_blk0, total_rows, prev=None):
    n_exp = slot.shape[1]
    d = h.shape[1]
    n_outer = n_exp // n_inner
    slot4 = slot.reshape(batch, n_outer, n_inner, seq)
    gate4 = gate.reshape(batch, n_outer, n_inner, seq)
    sspec = pl.BlockSpec((None, None, n_inner, seq), lambda b, e: (b, e, 0, 0))
    in_specs = [sspec, sspec, pl.BlockSpec((seq, d), lambda b, e: (b, 0))]
    args = [slot4, gate4, h]
    aliases = {}
    if prev is not None:
        in_specs = [pl.BlockSpec(memory_space=pl.ANY)] * 2 + in_specs
        args = list(prev) + args
        aliases = {0: 0, 1: 1}
    return pl.pallas_call(
        functools.partial(_gather_kernel, n_inner=n_inner, cap=cap, aliased=prev is not None),
        grid=(batch, n_outer),
        in_specs=in_specs,
        out_specs=[pl.BlockSpec((n_inner, cap, d), lambda b, e: (e, row_blk0 + b, 0)),
                   pl.BlockSpec((n_inner, cap, 1), lambda b, e: (e, row_blk0 + b, 0))],
        out_shape=[jax.ShapeDtypeStruct((n_exp, total_rows, d), BF16),
                   jax.ShapeDtypeStruct((n_exp, total_rows, 1), F32)],
        input_output_aliases=aliases,
        compiler_params=_params(2),
        name="moe_gather",
    )(*args)


def _gather_win_kernel(base_ref, flag_ref, slot_ref, gate_ref, h_ref, xs_ref, gs_ref, *, cap):
    k = pl.program_id(1)
    step = pl.program_id(0) * pl.num_programs(1) + k
    n_exp, chunk = slot_ref.shape
    win = MOE_WINDOW

    @pl.when(k == 0)
    def _():
        xs_ref[...] = jnp.zeros(xs_ref.shape, xs_ref.dtype)
        gs_ref[...] = jnp.zeros(gs_ref.shape, gs_ref.dtype)

    h = h_ref[...]

    @pl.when(flag_ref[step] == 0)
    def _():
        row = lax.broadcasted_iota(jnp.int32, (win, chunk), 0)
        bases = [pl.multiple_of(base_ref[step * n_exp + e], 16) for e in range(n_exp)]
        hots = [(slot_ref[e:e + 1, :] - bases[e]) == row for e in range(n_exp)]
        stacked = jnp.concatenate([jnp.where(hot, 1.0, 0.0).astype(BF16) for hot in hots], axis=0)
        picked = _dot(stacked, h)
        for e in range(n_exp):
            rows = pl.ds(bases[e], win)
            xs_ref[e, rows, :] += picked[e * win:(e + 1) * win].astype(BF16)
            gs_ref[e, rows, :] += jnp.sum(jnp.where(hots[e], gate_ref[e:e + 1, :], 0.0), axis=1, keepdims=True)

    @pl.when(flag_ref[step] != 0)
    def _():
        row = lax.broadcasted_iota(jnp.int32, (cap, chunk), 0)
        for e in range(n_exp):
            hot = slot_ref[e:e + 1, :] == row
            xs_ref[e] += _dot(jnp.where(hot, 1.0, 0.0).astype(BF16), h).astype(BF16)
            gs_ref[e] += jnp.sum(jnp.where(hot, gate_ref[e:e + 1, :], 0.0), axis=1, keepdims=True)


def _gather_windowed(slot, gate, h, base, flag, batch, seq, cap, row_blk0, total_rows):
    n_exp = slot.shape[1]
    d = h.shape[1]
    nk = seq // MOE_CHUNK
    sspec = pl.BlockSpec((None, n_exp, MOE_CHUNK), lambda b, k, *_: (b, 0, k))
    return pl.pallas_call(
        functools.partial(_gather_win_kernel, cap=cap),
        grid_spec=pltpu.PrefetchScalarGridSpec(
            num_scalar_prefetch=2,
            grid=(batch, nk),
            in_specs=[sspec, sspec, pl.BlockSpec((MOE_CHUNK, d), lambda b, k, *_: (b * nk + k, 0))],
            out_specs=[pl.BlockSpec((n_exp, cap, d), lambda b, k, *_: (0, row_blk0 + b, 0)),
                       pl.BlockSpec((n_exp, cap, 1), lambda b, k, *_: (0, row_blk0 + b, 0))]),
        out_shape=[jax.ShapeDtypeStruct((n_exp, total_rows, d), BF16),
                   jax.ShapeDtypeStruct((n_exp, total_rows, 1), F32)],
        compiler_params=_params(2),
        name="moe_gather_windowed",
    )(base, flag, slot, gate, h)


def _ffn_kernel(xs_ref, gs_ref, wg_ref, wu_ref, wd_ref, y_ref):
    x = xs_ref[...]
    a = _dot(x, wg_ref[...])
    u = _dot(x, wu_ref[...])
    y = _dot((_silu(a) * u).astype(BF16), wd_ref[...])
    y_ref[...] = (y * gs_ref[...]).astype(y_ref.dtype)


def _expert_ffn(xs, gs, wg, wu, wd, layer):
    n_exp, rows, d = xs.shape
    ff = wg.shape[3]
    tm = math.gcd(rows, 512)
    return pl.pallas_call(
        _ffn_kernel,
        grid=(n_exp, rows // tm),
        in_specs=[pl.BlockSpec((None, tm, d), lambda e, i: (e, i, 0)),
                  pl.BlockSpec((None, tm, 1), lambda e, i: (e, i, 0)),
                  pl.BlockSpec((None, None, d, ff), lambda e, i: (layer, e, 0, 0)),
                  pl.BlockSpec((None, None, d, ff), lambda e, i: (layer, e, 0, 0)),
                  pl.BlockSpec((None, None, ff, d), lambda e, i: (layer, e, 0, 0))],
        out_specs=pl.BlockSpec((None, tm, d), lambda e, i: (e, i, 0)),
        out_shape=jax.ShapeDtypeStruct((n_exp, rows, d), BF16),
        compiler_params=_params(2),
        name="moe_ffn",
    )(xs, gs, wg, wu, wd)


def _combine_kernel(x_ref, y_ref, slot_ref, g2_ref, lg_ref, lb_ref, o_ref, *, alpha, cap):
    n_exp = y_ref.shape[0]
    tt = x_ref.shape[0]
    slot = slot_ref[...]
    lanes = 128
    if cap < lanes and lanes % cap == 0 and n_exp % (lanes // cap) == 0:
        per = lanes // cap
        lane = lax.broadcasted_iota(jnp.int32, (tt, lanes), 1)
        hots = []
        for g in range(n_exp // per):
            hit = None
            for j in range(per):
                s = slot[:, g * per + j:g * per + j + 1]
                match = jnp.where(s >= 0, s + j * cap, -1) == lane
                hit = match if hit is None else hit | match
            hots.append(jnp.where(hit, 1.0, 0.0).astype(BF16))
        acc = _dot(jnp.concatenate(hots, axis=1), y_ref[...].reshape(n_exp * cap, y_ref.shape[2]))
    else:
        col = lax.broadcasted_iota(jnp.int32, (tt, cap), 1)
        acc = jnp.zeros(x_ref.shape, F32)
        for e in range(n_exp):
            onehot = jnp.where(slot[:, e:e + 1] == col, 1.0, 0.0).astype(BF16)
            acc += _dot(onehot, y_ref[e])
    o_ref[...] = _ln(alpha * x_ref[...] + g2_ref[...] * acc) * lg_ref[...] + lb_ref[...]


def _combine(x1, y, slot_t, mod3, ln_g, ln_b, batch, seq, cap, row_blk0, per_request, alpha):
    rows, d = x1.shape
    n_exp = y.shape[0]
    tt = min(256, seq)
    nt = seq // tt
    vec = pl.BlockSpec((1, d), lambda b, i: (0, 0))
    return pl.pallas_call(
        functools.partial(_combine_kernel, alpha=alpha, cap=cap),
        grid=(batch, nt),
        in_specs=[pl.BlockSpec((tt, d), lambda b, i: (b * nt + i, 0)),
                  pl.BlockSpec((n_exp, cap, d), lambda b, i: (0, row_blk0 + b, 0)),
                  pl.BlockSpec((None, tt, n_exp), lambda b, i: (b, i, 0)),
                  pl.BlockSpec((None, 1, d), lambda b, i: (_mod_row(b, 1, per_request) * 6 + 5, 0, 0)),
                  vec, vec],
        out_specs=pl.BlockSpec((tt, d), lambda b, i: (b * nt + i, 0)),
        out_shape=jax.ShapeDtypeStruct((rows, d), F32),
        compiler_params=_params(2),
        name="moe_combine_postnorm",
    )(x1, y, slot_t, mod3, ln_g, ln_b)


def _combine_win_kernel(base_ref, flag_ref, x_ref, y_ref, slot_ref, g2_ref, lg_ref, lb_ref, o_ref, acc_ref,
                        *, alpha, cap):
    step = pl.program_id(0) * pl.num_programs(1) + pl.program_id(1)
    n_exp = y_ref.shape[0]
    tt = x_ref.shape[0]
    win = MOE_WINDOW
    slot = slot_ref[...]

    @pl.when(flag_ref[step] == 0)
    def _():
        lane = lax.broadcasted_iota(jnp.int32, (tt, 2 * win), 1)
        hots, ys = [], []
        for pair in range(n_exp // 2):
            picks = []
            for half in range(2):
                e = 2 * pair + half
                base = pl.multiple_of(base_ref[step * n_exp + e], 16)
                rel = slot[:, e:e + 1] - base
                picks.append(jnp.where((rel >= 0) & (rel < win), rel + half * win, -1))
                ys.append(y_ref[e, pl.ds(base, win), :])
            hots.append(jnp.where((picks[0] == lane) | (picks[1] == lane), 1.0, 0.0).astype(BF16))
        acc_ref[...] = _dot(jnp.concatenate(hots, axis=1), jnp.concatenate(ys, axis=0))

    @pl.when(flag_ref[step] != 0)
    def _():
        col = lax.broadcasted_iota(jnp.int32, (tt, cap), 1)
        acc = jnp.zeros(x_ref.shape, F32)
        for e in range(n_exp):
            acc += _dot(jnp.where(slot[:, e:e + 1] == col, 1.0, 0.0).astype(BF16), y_ref[e])
        acc_ref[...] = acc

    o_ref[...] = _ln(alpha * x_ref[...] + g2_ref[...] * acc_ref[...]) * lg_ref[...] + lb_ref[...]


def _combine_windowed(x1, y, slot_t, base, flag, mod3, ln_g, ln_b, batch, seq, cap, row_blk0, alpha):
    rows, d = x1.shape
    n_exp = y.shape[0]
    tt = MOE_CHUNK
    nt = seq // tt
    assert 2 * MOE_WINDOW == 128 and n_exp % 2 == 0
    vec = pl.BlockSpec((1, d), lambda b, i, *_: (0, 0))
    return pl.pallas_call(
        functools.partial(_combine_win_kernel, alpha=alpha, cap=cap),
        grid_spec=pltpu.PrefetchScalarGridSpec(
            num_scalar_prefetch=2,
            grid=(batch, nt),
            in_specs=[pl.BlockSpec((tt, d), lambda b, i, *_: (b * nt + i, 0)),
                      pl.BlockSpec((n_exp, cap, d), lambda b, i, *_: (0, row_blk0 + b, 0)),
                      pl.BlockSpec((None, tt, n_exp), lambda b, i, *_: (b, i, 0)),
                      pl.BlockSpec((None, 1, d), lambda b, i, *_: (_mod_row(b, 1, True) * 6 + 5, 0, 0)),
                      vec, vec],
            out_specs=pl.BlockSpec((tt, d), lambda b, i, *_: (b * nt + i, 0)),
            scratch_shapes=[pltpu.VMEM((tt, d), F32)]),
        out_shape=jax.ShapeDtypeStruct((rows, d), F32),
        compiler_params=_params(2),
        name="moe_combine_windowed_postnorm",
    )(base, flag, x1, y, slot_t, mod3, ln_g, ln_b)


def kernel(x_prompt, x_sample, cache_na_k, cache_na_v, cache_gqa_k, cache_gqa_v, state_ret_fwd, state_ret_bwd,
           c, c_ctx, w_in, w_out, w_mod, b_mod, ln1_g, ln1_b, ln2_g, ln2_b, q_norm_g, k_norm_g, na_rpb,
           ret_decay_fwd, ret_decay_bwd, w_router, w_gate, w_up, w_down):
    batch, seq, d = x_prompt.shape
    dbatch, dseq, _ = x_sample.shape
    depth = w_in.shape[0]
    na_heads = cache_na_k.shape[2]
    kv_heads = cache_gqa_k.shape[2]
    ret_heads = state_ret_fwd.shape[2]
    n_exp = w_router.shape[2]
    hd = HEAD_DIM
    gqa_heads = (w_in.shape[2] // hd - 3 * na_heads - 2 * kv_heads - 4 * ret_heads)
    group = gqa_heads // kv_heads
    alpha = float((2 * depth) ** 0.25)
    c_na = 0
    c_gq = 3 * na_heads
    c_gk = c_gq + gqa_heads
    c_gv = c_gk + kv_heads
    c_rt = c_gv + kv_heads
    cap_c = EC_CAPACITY_FACTOR * seq // n_exp
    cap_l = EC_CAPACITY_FACTOR * dseq // n_exp
    assert dbatch + 1 <= MOD_ROWS and (batch * cap_c) % cap_l == 0
    lat_blk0 = batch * cap_c // cap_l
    rows_per_expert = batch * cap_c + dbatch * cap_l

    cond = jnp.zeros((MOD_ROWS, d), F32).at[0].set(c_ctx).at[1:1 + dbatch].set(c)
    mod_all = _modulation(cond, w_mod, b_mod)
    tables = _rope_tables(dseq)
    bias = _na_bias(na_rpb, dseq // GRID_W)

    w_in_b, w_out_b = w_in.astype(BF16), w_out.astype(BF16)
    wg_b, wu_b, wd_b = w_gate.astype(BF16), w_up.astype(BF16), w_down.astype(BF16)

    xp = x_prompt.reshape(batch * seq, d)
    xs = x_sample.reshape(dbatch * dseq, d)
    outs = [[] for _ in range(6)]
    for l in range(depth):
        mod3 = mod_all[l].reshape(MOD_ROWS * 6, 1, d)
        wr_t = jnp.concatenate(_split_bf16(w_router[l]), axis=1)
        gains = (q_norm_g[l].reshape(1, hd), k_norm_g[l].reshape(1, hd))
        l1g, l1b = ln1_g[l].reshape(1, d), ln1_b[l].reshape(1, d)
        l2g, l2b = ln2_g[l].reshape(1, d), ln2_b[l].reshape(1, d)

        zc = _in_projection(xp, mod3, w_in_b, l, seq, False, F32)
        (na_c,) = _ctx_attention(zc, batch, seq, na_heads, 1, c_na, c_na + na_heads, c_na + 2 * na_heads)
        gq_c, gk_n = _ctx_attention(zc, batch, seq, kv_heads, group, c_gq, c_gk, c_gv, gains)
        rt_c, s_f, s_b = _retention(zc, batch, seq, ret_heads, c_rt, ret_decay_fwd[l], ret_decay_bwd[l],
                                    state_out=True)

        def heads_of(col, n):
            return zc[:, col * hd:(col + n) * hd].reshape(batch, seq, n, hd).transpose(0, 2, 1, 3)

        outs[0].append(heads_of(c_na + na_heads, na_heads))
        outs[1].append(heads_of(c_na + 2 * na_heads, na_heads))
        outs[2].append(gk_n)
        outs[3].append(heads_of(c_gv, kv_heads))
        outs[4].append(s_f)
        outs[5].append(s_b)

        zl = _in_projection(xs, mod3, w_in_b, l, dseq, True, BF16)
        (na_l,) = [_na_attention(zl, dbatch, dseq, na_heads, cache_na_k, cache_na_v, l, bias)]
        gq_l = _gqa_attention(zl, dbatch, dseq, kv_heads, group, c_gq, c_gk, c_gv, cache_gqa_k, cache_gqa_v, l,
                              gains, tables)
        (rt_l,) = _retention(zl, dbatch, dseq, ret_heads, c_rt, ret_decay_fwd[l], ret_decay_bwd[l],
                             states=((state_ret_fwd, state_ret_bwd), l))

        xp1, hp, lt_c = _out_projection(xp, na_c, gq_c, rt_c, w_out_b, l, mod3, l1g, l1b, wr_t, seq, False, alpha)
        xs1, hs, lt_l = _out_projection(xs, na_l, gq_l, rt_l, w_out_b, l, mod3, l1g, l1b, wr_t, dseq, True, alpha)

        slot_c, gate_c, _ = _route(lt_c.T, batch, seq, cap_c)
        slot_l, gate_l, before_l = _route(lt_l.T, dbatch, dseq, cap_l)
        windowed = cap_l > MOE_WINDOW and dseq % MOE_CHUNK == 0
        if windowed:
            base_l, flag_l = _window_plan(before_l, cap_l, dseq)
            gathered = _gather_windowed(slot_l, gate_l, hs, base_l, flag_l, dbatch, dseq, cap_l, lat_blk0,
                                        rows_per_expert)
        else:
            gathered = _gather(slot_l, gate_l, hs, dbatch, dseq, cap_l, 2, lat_blk0, rows_per_expert)
        gathered = _gather(slot_c, gate_c, hp, batch, seq, cap_c, n_exp, 0, rows_per_expert, prev=gathered)
        y = _expert_ffn(gathered[0], gathered[1], wg_b, wu_b, wd_b, l)
        xp = _combine(xp1, y, slot_c.transpose(0, 2, 1), mod3, l2g, l2b, batch, seq, cap_c, 0, False, alpha)
        if windowed:
            xs = _combine_windowed(xs1, y, slot_l.transpose(0, 2, 1), base_l, flag_l, mod3, l2g, l2b, dbatch, dseq,
                                   cap_l, lat_blk0, alpha)
        else:
            xs = _combine(xs1, y, slot_l.transpose(0, 2, 1), mod3, l2g, l2b, dbatch, dseq, cap_l, lat_blk0, True,
                          alpha)

    stacked = [jnp.stack(o, axis=1) for o in outs]
    return (xp.reshape(batch, seq, d), xs.reshape(dbatch, dseq, d), *stacked)
```

```python
import functools
import math

import numpy as np
import jax
import jax.numpy as jnp
from jax import lax
from jax.experimental import pallas as pl
from jax.experimental.pallas import tpu as pltpu

F32 = jnp.float32
BF16 = jnp.bfloat16

HEAD_DIM = 128
GRID_W = 64
NA_WIN_R = 8
NA_WIN_C = 16
ROPE_THETA = 10000.0
EC_CAPACITY_FACTOR = 2
LN_EPS = 1e-5
RMS_EPS = 1e-6
NEG_INF = -1e30
ATTN_SCALE = HEAD_DIM ** -0.5
LOG2E = 1.4426950408889634
NA_BLOCK_ROWS = 4
MOE_CHUNK = 256
MOE_WINDOW = 64
MOD_ROWS = 16
VMEM_LIMIT = 56 * 1024 * 1024

_NT = (((1,), (1,)), ((), ()))
_NN = (((1,), (0,)), ((), ()))


def _params(n_grid, vmem=VMEM_LIMIT):
    return pltpu.CompilerParams(dimension_semantics=("arbitrary",) * n_grid, vmem_limit_bytes=vmem)


def _dot(a, b, dims=_NN):
    return lax.dot_general(a, b, dims, preferred_element_type=F32)


def _split_bf16(x):
    hi = x.astype(BF16)
    lo = (x - hi.astype(F32)).astype(BF16)
    return hi, lo


def _dot3(a, b, dims=_NN):
    ah, al = _split_bf16(a)
    bh, bl = _split_bf16(b)
    return _dot(ah, bh, dims) + _dot(al, bh, dims) + _dot(ah, bl, dims)


def _silu(x):
    return x / (1.0 + jnp.exp(-x))


def _ln(x):
    mu = jnp.mean(x, axis=-1, keepdims=True)
    xc = x - mu
    var = jnp.mean(xc * xc, axis=-1, keepdims=True)
    return xc * lax.rsqrt(var + LN_EPS)


def _rms(x, g):
    return x * lax.rsqrt(jnp.mean(x * x, axis=-1, keepdims=True) + RMS_EPS) * g


def _rope(x, cos, sin_signed):
    lane = lax.broadcasted_iota(jnp.int32, x.shape, 1) & (HEAD_DIM // 2 - 1)
    partner = jnp.where(lane < HEAD_DIM // 4,
                        pltpu.roll(x, HEAD_DIM - HEAD_DIM // 4, 1),
                        pltpu.roll(x, HEAD_DIM // 4, 1))
    return x * cos + partner * sin_signed


def _softmax_pv(scores, values):
    m = functools.reduce(jnp.maximum, [jnp.max(s, axis=-1, keepdims=True) for s in scores])
    ps = [jnp.exp(s - m) for s in scores]
    denom = functools.reduce(lambda a, b: a + b, [jnp.sum(p, axis=-1, keepdims=True) for p in ps])
    o = functools.reduce(lambda a, b: a + b, [_dot(p.astype(BF16), v) for p, v in zip(ps, values)])
    return o / denom


def _mod_kernel(c_ref, w_ref, b_ref, o_ref):
    a = _silu(c_ref[...])
    o_ref[...] = _dot3(a, w_ref[...]) + b_ref[...]


def _modulation(cond, w_mod, b_mod):
    depth, d, n = w_mod.shape
    tn = 768
    return pl.pallas_call(
        _mod_kernel,
        grid=(depth, n // tn),
        in_specs=[pl.BlockSpec((MOD_ROWS, d), lambda l, j: (0, 0)),
                  pl.BlockSpec((None, d, tn), lambda l, j: (l, 0, j)),
                  pl.BlockSpec((None, 1, tn), lambda l, j: (l, 0, j))],
        out_specs=pl.BlockSpec((None, MOD_ROWS, tn), lambda l, j: (l, 0, j)),
        out_shape=jax.ShapeDtypeStruct((depth, MOD_ROWS, n), F32),
        compiler_params=_params(2),
        name="adaln_mod",
    )(cond, w_mod, b_mod.reshape(depth, 1, n))


def _inproj_kernel(x_ref, sh_ref, sc_ref, w_ref, z_ref, *, sub):
    for r in range(x_ref.shape[0] // sub):
        rs = slice(r * sub, (r + 1) * sub)
        h = _ln(x_ref[rs, :]) * (1.0 + sc_ref[...]) + sh_ref[...]
        z_ref[rs, :] = _dot(h.astype(BF16), w_ref[...]).astype(z_ref.dtype)


def _mod_row(block, blocks_per_request, per_request):
    return 1 + block // blocks_per_request if per_request else 0


def _in_projection(x, mod3, w_bf16, layer, tokens_per_batch, per_request, out_dtype):
    rows, d = x.shape
    n = w_bf16.shape[2]
    tm = min(512, tokens_per_batch)
    tn = 2560
    per = tokens_per_batch // tm

    def mod_spec(chunk):
        return pl.BlockSpec((None, 1, d), lambda j, i: (_mod_row(i, per, per_request) * 6 + chunk, 0, 0))

    return pl.pallas_call(
        functools.partial(_inproj_kernel, sub=min(256, tm)),
        grid=(n // tn, rows // tm),
        in_specs=[pl.BlockSpec((tm, d), lambda j, i: (i, 0)),
                  mod_spec(0), mod_spec(1),
                  pl.BlockSpec((None, d, tn), lambda j, i: (layer, 0, j))],
        out_specs=pl.BlockSpec((tm, tn), lambda j, i: (i, j)),
        out_shape=jax.ShapeDtypeStruct((rows, n), out_dtype),
        compiler_params=_params(2),
        name="ln_inproj",
    )(x, mod3, mod3, w_bf16)


def _ctx_attn_kernel(*refs, group, normed):
    if normed:
        q_ref, k_ref, v_ref, qg_ref, kg_ref, o_ref, kn_ref = refs
    else:
        q_ref, k_ref, v_ref, o_ref = refs
    k = k_ref[...]
    if normed:
        k = _rms(k, kg_ref[...])
        kn_ref[...] = k
    kb = k.astype(BF16)
    vb = v_ref[...].astype(BF16)
    for g in range(group):
        q = q_ref[:, g * HEAD_DIM:(g + 1) * HEAD_DIM]
        if normed:
            q = _rms(q, qg_ref[...])
        s = _dot(q.astype(BF16), kb, _NT) * ATTN_SCALE
        o_ref[:, g * HEAD_DIM:(g + 1) * HEAD_DIM] = _softmax_pv([s], [vb]).astype(o_ref.dtype)


def _ctx_attention(z, batch, seq, heads, group, q_col, k_col, v_col, gains=None):
    normed = gains is not None
    qw = group * HEAD_DIM
    in_specs = [pl.BlockSpec((seq, qw), lambda b, h: (b, q_col // group + h)),
                pl.BlockSpec((seq, HEAD_DIM), lambda b, h: (b, k_col + h)),
                pl.BlockSpec((seq, HEAD_DIM), lambda b, h: (b, v_col + h))]
    args = [z, z, z]
    out_specs = [pl.BlockSpec((seq, qw), lambda b, h: (b, h))]
    out_shape = [jax.ShapeDtypeStruct((batch * seq, heads * qw), BF16)]
    if normed:
        in_specs += [pl.BlockSpec((1, HEAD_DIM), lambda b, h: (0, 0))] * 2
        args += list(gains)
        out_specs.append(pl.BlockSpec((None, None, seq, HEAD_DIM), lambda b, h: (b, h, 0, 0)))
        out_shape.append(jax.ShapeDtypeStruct((batch, heads, seq, HEAD_DIM), F32))
    return pl.pallas_call(
        functools.partial(_ctx_attn_kernel, group=group, normed=normed),
        grid=(batch, heads),
        in_specs=in_specs, out_specs=out_specs, out_shape=out_shape,
        compiler_params=_params(2),
        name="ctx_gqa" if normed else "ctx_na",
    )(*args)


def _log_sigmoid(x):
    return -jnp.log1p(jnp.exp(-x))


def _retention_kernel(*refs, seq, tq, with_state_in, with_state_out):
    q_ref, k_ref, v_ref, g_ref, df_ref, db_ref = refs[:6]
    refs = refs[6:]
    if with_state_in:
        sf_ref, sb_ref = refs[:2]
        refs = refs[2:]
    o_ref = refs[0]
    decay_ref = refs[-1]
    lgf = _log_sigmoid(df_ref[...])
    lgb = _log_sigmoid(db_ref[...])
    t0 = pl.program_id(1) * tq

    @pl.when(pl.program_id(2) == 0)
    def _():
        i = t0 + lax.broadcasted_iota(jnp.int32, (tq, seq), 0)
        j = lax.broadcasted_iota(jnp.int32, (tq, seq), 1)
        dist = (i - j).astype(F32)
        decay = jnp.where(dist == 0.0, 2.0, jnp.exp(jnp.where(dist > 0.0, lgf * dist, -lgb * dist)))
        decay_ref[...] = decay * ATTN_SCALE

    q = q_ref[...]
    kb = k_ref[...].astype(BF16)
    vb = v_ref[...].astype(BF16)
    att = _dot(q.astype(BF16), kb, _NT) * decay_ref[...]
    o = _dot(att.astype(BF16), vb)
    if with_state_in:
        t = (t0 + lax.broadcasted_iota(jnp.int32, (tq, 1), 0)).astype(F32)
        qf = q.astype(F32)
        o += _dot((qf * jnp.exp(lgf * (t + 1.0))).astype(BF16), sf_ref[...].astype(BF16))
        o += _dot((qf * jnp.exp(lgb * (seq - t))).astype(BF16), sb_ref[...].astype(BF16))
    o_ref[...] = (_ln(o) * _silu(g_ref[...].astype(F32))).astype(o_ref.dtype)
    if with_state_out:
        nsf_ref, nsb_ref = refs[1:3]
        tj = lax.broadcasted_iota(jnp.int32, (seq, 1), 0).astype(F32)
        kf = k_ref[...].astype(F32) * ATTN_SCALE
        kdf = (kf * jnp.exp(lgf * (seq - 1.0 - tj))).T.astype(BF16)
        kdb = (kf * jnp.exp(lgb * tj)).T.astype(BF16)
        nsf_ref[...] = _dot(kdf, vb)
        nsb_ref[...] = _dot(kdb, vb)


def _retention(z, batch, seq, heads, col0, dec_f, dec_b, states=None, state_out=False):
    tq = min(512, seq)
    nq = seq // tq
    assert not (state_out and nq != 1)
    in_specs = [pl.BlockSpec((tq, HEAD_DIM), lambda h, i, b: (b * nq + i, col0 + h)),
                pl.BlockSpec((seq, HEAD_DIM), lambda h, i, b: (b, col0 + heads + h)),
                pl.BlockSpec((seq, HEAD_DIM), lambda h, i, b: (b, col0 + 2 * heads + h)),
                pl.BlockSpec((tq, HEAD_DIM), lambda h, i, b: (b * nq + i, col0 + 3 * heads + h)),
                pl.BlockSpec((None, 1, 1), lambda h, i, b: (h, 0, 0)),
                pl.BlockSpec((None, 1, 1), lambda h, i, b: (h, 0, 0))]
    args = [z, z, z, z, dec_f.reshape(heads, 1, 1), dec_b.reshape(heads, 1, 1)]
    if states is not None:
        (sf, sb), layer = states
        spec = pl.BlockSpec((None, None, None, HEAD_DIM, HEAD_DIM), lambda h, i, b: (b, layer, h, 0, 0))
        in_specs += [spec, spec]
        args += [sf, sb]
    out_specs = [pl.BlockSpec((tq, HEAD_DIM), lambda h, i, b: (b * nq + i, h))]
    out_shape = [jax.ShapeDtypeStruct((batch * seq, heads * HEAD_DIM), BF16)]
    if state_out:
        spec = pl.BlockSpec((None, None, HEAD_DIM, HEAD_DIM), lambda h, i, b: (b, h, 0, 0))
        out_specs += [spec, spec]
        out_shape += [jax.ShapeDtypeStruct((batch, heads, HEAD_DIM, HEAD_DIM), F32)] * 2
    return pl.pallas_call(
        functools.partial(_retention_kernel, seq=seq, tq=tq, with_state_in=states is not None,
                          with_state_out=state_out),
        grid=(heads, nq, batch),
        in_specs=in_specs, out_specs=out_specs, out_shape=out_shape,
        scratch_shapes=[pltpu.VMEM((tq, seq), F32)],
        compiler_params=_params(3),
        name="retention",
    )(*args)


def _with_ones(v):
    return jnp.concatenate([v, jnp.ones(v.shape, v.dtype)], axis=1)


def _exp2_pv(scores, values_with_ones):
    m = functools.reduce(jnp.maximum, [jnp.max(s, axis=-1, keepdims=True) for s in scores])
    r = functools.reduce(lambda a, b: a + b,
                         [_dot(jnp.exp2(s - m).astype(BF16), v) for s, v in zip(scores, values_with_ones)])
    return r[:, :HEAD_DIM] / r[:, HEAD_DIM:HEAD_DIM + 1]


def _na_plan(rows):
    kr = min(NA_WIN_R, rows)
    nq = min(NA_BLOCK_ROWS, rows)
    nk = min(nq + kr - 1, rows)
    assert rows % nq == 0
    blocks, cases = [], []
    for r0 in range(0, rows, nq):
        start = int(np.clip(r0 - kr // 2, 0, rows - nk))
        win = [int(np.clip(r - kr // 2, 0, rows - kr)) for r in range(r0, r0 + nq)]
        assert all(start <= w and w + kr <= start + nk for w in win)
        key = (start - r0,) + tuple(w - start for w in win)
        if key not in cases:
            cases.append(key)
        blocks.append((r0, start, cases.index(key)))
    return kr, nq, nk, blocks, cases


def _na_kernel(q_ref, k_ref, v_ref, ck_ref, cv_ref, bias_ref, o_ref, vx_ref, cvx_ref, *, blocks, nq, nk):
    vx_ref[...] = _with_ones(v_ref[...])
    cvx_ref[...] = _with_ones(cv_ref[...].astype(BF16))
    ck = ck_ref[...].astype(BF16)
    c = ATTN_SCALE * LOG2E
    for r0, start, case in blocks:
        qs = slice(r0 * GRID_W, (r0 + nq) * GRID_W)
        ks = slice(start * GRID_W, (start + nk) * GRID_W)
        q = q_ref[qs, :]
        s_loc = _dot(q, k_ref[ks, :], _NT) * c + bias_ref[case]
        s_ctx = _dot(q, ck, _NT) * c
        o_ref[qs, :] = _exp2_pv([s_loc, s_ctx], [vx_ref[ks, :], cvx_ref[...]]).astype(o_ref.dtype)


def _na_bias(rpb, rows):
    kr, nq, nk, _, cases = _na_plan(rows)
    dr = np.zeros((len(cases), nq, nk), np.int32)
    row_ok = np.zeros((len(cases), nq, nk), bool)
    for c, key in enumerate(cases):
        rel, offs = key[0], key[1:]
        for ri in range(nq):
            for ju in range(nk):
                row_ok[c, ri, ju] = 0 <= ju - offs[ri] < kr
                dr[c, ri, ju] = np.clip(rel + ju - ri + NA_WIN_R - 1, 0, 2 * NA_WIN_R - 2)
    cq = np.arange(GRID_W)
    ck = np.arange(GRID_W)
    col_start = np.clip(cq - NA_WIN_C // 2, 0, GRID_W - NA_WIN_C)
    col_ok = (ck[None, :] >= col_start[:, None]) & (ck[None, :] < col_start[:, None] + NA_WIN_C)
    dc = np.clip(ck[None, :] - cq[:, None] + (NA_WIN_C - 1), 0, 2 * NA_WIN_C - 2)
    pick_col = (dc.reshape(-1)[None, :] == np.arange(2 * NA_WIN_C - 1)[:, None]).astype(np.float32)
    n_l, n_h = rpb.shape[:2]
    t = jnp.take(rpb.astype(F32), dr.reshape(-1), axis=2)
    t = jnp.einsum("lhxb,bn->lhxn", t, pick_col, precision=lax.Precision.HIGHEST)
    t = t.reshape(n_l, n_h, len(cases), nq, nk, GRID_W, GRID_W).transpose(0, 1, 2, 3, 5, 4, 6)
    ok = row_ok[:, :, None, :, None] & col_ok[None, None, :, None, :]
    t = jnp.where(ok, t * LOG2E, NEG_INF)
    return t.reshape(n_l, n_h, len(cases), nq * GRID_W, nk * GRID_W)


def _na_attention(z, batch, seq, heads, cache_k, cache_v, layer, bias):
    _, nq, nk, blocks, cases = _na_plan(seq // GRID_W)
    past = cache_k.shape[3]
    cspec = pl.BlockSpec((None, None, None, past, HEAD_DIM), lambda b, h: (b, layer, h, 0, 0))
    return pl.pallas_call(
        functools.partial(_na_kernel, blocks=blocks, nq=nq, nk=nk),
        grid=(batch, heads),
        in_specs=[pl.BlockSpec((seq, HEAD_DIM), lambda b, h: (b, h)),
                  pl.BlockSpec((seq, HEAD_DIM), lambda b, h: (b, heads + h)),
                  pl.BlockSpec((seq, HEAD_DIM), lambda b, h: (b, 2 * heads + h)),
                  cspec, cspec,
                  pl.BlockSpec((None, None, len(cases), nq * GRID_W, nk * GRID_W),
                               lambda b, h: (layer, h, 0, 0, 0))],
        out_specs=pl.BlockSpec((seq, HEAD_DIM), lambda b, h: (b, h)),
        out_shape=jax.ShapeDtypeStruct((batch * seq, heads * HEAD_DIM), BF16),
        scratch_shapes=[pltpu.VMEM((seq, 2 * HEAD_DIM), BF16), pltpu.VMEM((past, 2 * HEAD_DIM), BF16)],
        compiler_params=_params(2),
        name="latent_na",
    )(z, z, z, cache_k, cache_v, bias)


def _gqa_kernel(q_ref, k_ref, v_ref, ck_ref, cv_ref, qg_ref, kg_ref, cos_ref, sin_ref, o_ref, kf_ref, vf_ref,
                *, group, past, seq, tq):
    i = pl.program_id(2)

    @pl.when(i == 0)
    def _():
        kf_ref[0:past, :] = ck_ref[...].astype(BF16)
        vf_ref[0:past, :] = _with_ones(cv_ref[...].astype(BF16))
        k = _rope(_rms(k_ref[...].astype(F32), kg_ref[...]), cos_ref[...], sin_ref[...])
        kf_ref[past:past + seq, :] = k.astype(BF16)
        vf_ref[past:past + seq, :] = _with_ones(v_ref[...])

    t0 = pl.multiple_of(i * tq, tq)
    cos = cos_ref[pl.ds(t0, tq), :]
    sin = sin_ref[pl.ds(t0, tq), :]
    kf = kf_ref[...]
    vf = vf_ref[...]
    def scores(g):
        q = q_ref[:, g * HEAD_DIM:(g + 1) * HEAD_DIM].astype(F32)
        q = _rope(_rms(q, qg_ref[...]), cos, sin) * (ATTN_SCALE * LOG2E)
        return _dot(q.astype(BF16), kf, _NT)

    s = scores(0)
    for g in range(group):
        s_next = scores(g + 1) if g + 1 < group else None
        o_ref[:, g * HEAD_DIM:(g + 1) * HEAD_DIM] = _exp2_pv([s], [vf]).astype(o_ref.dtype)
        s = s_next


def _rope_tables(seq):
    t = np.arange(seq)
    half = HEAD_DIM // 2
    inv = 1.0 / (ROPE_THETA ** (np.arange(0, half, 2, dtype=np.float32) / half))
    row = (t // GRID_W).astype(np.float32)
    col = (t % GRID_W).astype(np.float32)
    ang = jnp.concatenate([jnp.asarray(row[:, None] * inv)] * 2 + [jnp.asarray(col[:, None] * inv)] * 2, axis=-1)
    sign = np.where((np.arange(HEAD_DIM) % half) < half // 2, -1.0, 1.0).astype(np.float32)
    return jnp.cos(ang), jnp.sin(ang) * sign


def _gqa_attention(z, batch, seq, kv_heads, group, q_col, k_col, v_col, cache_k, cache_v, layer, gains, tables):
    past = cache_k.shape[3]
    tq = min(256, seq)
    nq = seq // tq
    qw = group * HEAD_DIM
    cspec = pl.BlockSpec((None, None, None, past, HEAD_DIM), lambda b, h, i: (b, layer, h, 0, 0))
    gspec = pl.BlockSpec((1, HEAD_DIM), lambda b, h, i: (0, 0))
    tspec = pl.BlockSpec((seq, HEAD_DIM), lambda b, h, i: (0, 0))
    return pl.pallas_call(
        functools.partial(_gqa_kernel, group=group, past=past, seq=seq, tq=tq),
        grid=(batch, kv_heads, nq),
        in_specs=[pl.BlockSpec((tq, qw), lambda b, h, i: (b * nq + i, q_col // group + h)),
                  pl.BlockSpec((seq, HEAD_DIM), lambda b, h, i: (b, k_col + h)),
                  pl.BlockSpec((seq, HEAD_DIM), lambda b, h, i: (b, v_col + h)),
                  cspec, cspec, gspec, gspec, tspec, tspec],
        out_specs=pl.BlockSpec((tq, qw), lambda b, h, i: (b * nq + i, h)),
        out_shape=jax.ShapeDtypeStruct((batch * seq, kv_heads * qw), BF16),
        scratch_shapes=[pltpu.VMEM((past + seq, HEAD_DIM), BF16), pltpu.VMEM((past + seq, 2 * HEAD_DIM), BF16)],
        compiler_params=_params(3),
        name="latent_gqa",
    )(z, z, z, cache_k, cache_v, gains[0], gains[1], tables[0], tables[1])


def _outproj_kernel(x_ref, na_ref, gq_ref, rt_ref, w_ref, g1_ref, sh2_ref, sc2_ref, lg_ref, lb_ref, wr_ref,
                    x1_ref, h2_ref, lt_ref, *, alpha, sub):
    n_na = na_ref.shape[1]
    n_gq = gq_ref.shape[1]
    n_exp = lt_ref.shape[1]
    for r in range(x_ref.shape[0] // sub):
        rs = slice(r * sub, (r + 1) * sub)
        y = _dot(na_ref[rs, :], w_ref[0:n_na, :])
        y += _dot(gq_ref[rs, :], w_ref[n_na:n_na + n_gq, :])
        y += _dot(rt_ref[rs, :], w_ref[n_na + n_gq:, :])
        x1 = _ln(alpha * x_ref[rs, :] + g1_ref[...] * y) * lg_ref[...] + lb_ref[...]
        x1_ref[rs, :] = x1
        h2 = _ln(x1) * (1.0 + sc2_ref[...]) + sh2_ref[...]
        hi, lo = _split_bf16(h2)
        h2_ref[rs, :] = hi
        both = _dot(hi, wr_ref[...])
        lt_ref[rs, :] = both[:, :n_exp] + both[:, n_exp:] + _dot(lo, wr_ref[:, 0:n_exp])


def _out_projection(x, na_o, gq_o, rt_o, w_bf16, layer, mod3, ln_g, ln_b, wr_hl, tokens_per_batch, per_request,
                    alpha):
    rows, d = x.shape
    tm = 512
    per = tokens_per_batch // tm if per_request else 1
    n_exp = wr_hl.shape[1] // 2

    def mod_spec(chunk):
        return pl.BlockSpec((None, 1, d), lambda i: (_mod_row(i, per, per_request) * 6 + chunk, 0, 0))

    def row_spec(width):
        return pl.BlockSpec((tm, width), lambda i: (i, 0))

    vec = pl.BlockSpec((1, d), lambda i: (0, 0))
    return pl.pallas_call(
        functools.partial(_outproj_kernel, alpha=alpha, sub=256),
        grid=(rows // tm,),
        in_specs=[row_spec(d), row_spec(na_o.shape[1]), row_spec(gq_o.shape[1]), row_spec(rt_o.shape[1]),
                  pl.BlockSpec((None,) + w_bf16.shape[1:], lambda i: (layer, 0, 0)),
                  mod_spec(2), mod_spec(3), mod_spec(4), vec, vec,
                  pl.BlockSpec(wr_hl.shape, lambda i: (0, 0))],
        out_specs=[row_spec(d), row_spec(d), row_spec(n_exp)],
        out_shape=[jax.ShapeDtypeStruct((rows, d), F32), jax.ShapeDtypeStruct((rows, d), BF16),
                   jax.ShapeDtypeStruct((rows, n_exp), F32)],
        compiler_params=_params(1),
        name="outproj_postnorm_router",
    )(x, na_o, gq_o, rt_o, w_bf16, mod3, mod3, mod3, ln_g, ln_b, wr_hl)


def _topk_kernel(lt_ref, slot_ref, gate_ref, before_ref, *, cap, seq):
    logits = lt_ref[...]
    n_exp = logits.shape[0]
    m = jnp.max(logits, axis=0, keepdims=True)
    ex = jnp.exp(logits - m)
    aff = ex / jnp.sum(ex, axis=0, keepdims=True)
    bits = lax.bitcast_convert_type(aff, jnp.int32)

    def count(mask):
        return jnp.sum(jnp.where(mask, 1.0, 0.0), axis=1, keepdims=True)

    def value_step(it, thr):
        cand = thr | jnp.left_shift(jnp.int32(1), 30 - it)
        return jnp.where(count(bits >= cand) >= cap, cand, thr)

    thr = lax.fori_loop(0, 31, value_step, jnp.zeros((n_exp, 1), jnp.int32))
    above = bits > thr
    tied = bits == thr
    need = cap - count(above)
    tok = lax.broadcasted_iota(jnp.int32, (n_exp, seq), 1)
    n_bits = int(seq - 1).bit_length()

    def index_step(it, bound):
        cand = bound | jnp.left_shift(jnp.int32(1), n_bits - 1 - it)
        return jnp.where(count(tied & (tok < cand)) < need, cand, bound)

    bound = lax.fori_loop(0, n_bits, index_step, jnp.zeros((n_exp, 1), jnp.int32))
    sel = above | (tied & (tok <= bound))
    self = jnp.where(sel, 1.0, 0.0)
    lanes = 128
    upper = jnp.where(lax.broadcasted_iota(jnp.int32, (lanes, lanes), 0)
                      < lax.broadcasted_iota(jnp.int32, (lanes, lanes), 1), 1.0, 0.0).astype(BF16)
    running = jnp.zeros((n_exp, 1), F32)
    lane = lax.broadcasted_iota(jnp.int32, (n_exp, lanes), 1)
    before = jnp.zeros((n_exp, lanes), F32)
    for blk in range(seq // lanes):
        sl = slice(blk * lanes, (blk + 1) * lanes)
        chunk = self[:, sl]
        before = jnp.where(lane == blk, running, before)
        pos = _dot(chunk.astype(BF16), upper) + running
        slot_ref[:, sl] = jnp.where(chunk > 0.0, pos.astype(jnp.int32), -1)
        running = running + jnp.sum(chunk, axis=1, keepdims=True)
    gate_ref[...] = aff
    before_ref[...] = before.astype(jnp.int32)


def _route(logits_t, batch, seq, cap):
    n_exp = logits_t.shape[0]
    assert seq // 128 <= 128
    spec = pl.BlockSpec((None, n_exp, seq), lambda b: (b, 0, 0))
    return pl.pallas_call(
        functools.partial(_topk_kernel, cap=cap, seq=seq),
        grid=(batch,),
        in_specs=[pl.BlockSpec((n_exp, seq), lambda b: (0, b))],
        out_specs=[spec, spec, pl.BlockSpec((None, n_exp, 128), lambda b: (b, 0, 0))],
        out_shape=[jax.ShapeDtypeStruct((batch, n_exp, seq), jnp.int32),
                   jax.ShapeDtypeStruct((batch, n_exp, seq), F32),
                   jax.ShapeDtypeStruct((batch, n_exp, 128), jnp.int32)],
        compiler_params=_params(1),
        name="route_topk",
    )(logits_t)


def _window_plan(before, cap, seq):
    per = MOE_CHUNK // 128
    lo = before[:, :, 0:seq // 128:per]
    hi = jnp.concatenate([lo[:, :, 1:], jnp.full(lo.shape[:2] + (1,), cap, jnp.int32)], axis=2)
    base = jnp.minimum(lo // 16 * 16, cap - MOE_WINDOW)
    overflow = jnp.any(hi - base > MOE_WINDOW, axis=1)
    return base.transpose(0, 2, 1).reshape(-1), overflow.astype(jnp.int32).reshape(-1)


def _gather_kernel(slot_ref, gate_ref, h_ref, xs_ref, gs_ref, *, n_inner, cap):
    h = h_ref[...]
    seq = h.shape[0]
    row = lax.broadcasted_iota(jnp.int32, (cap, seq), 0)
    for e in range(n_inner):
        onehot = slot_ref[e:e + 1, :] == row
        xs_ref[e] = _dot(jnp.where(onehot, 1.0, 0.0).astype(BF16), h).astype(BF16)
        gs_ref[e] = jnp.sum(jnp.where(onehot, gate_ref[e:e + 1, :], 0.0), axis=1, keepdims=True)


def _gather(slot, gate, h, batch, seq, cap, n_inner):
    n_exp = slot.shape[1]
    d = h.shape[1]
    n_outer = n_exp // n_inner
    slot4 = slot.reshape(batch, n_outer, n_inner, seq)
    gate4 = gate.reshape(batch, n_outer, n_inner, seq)
    sspec = pl.BlockSpec((None, None, n_inner, seq), lambda b, e: (b, e, 0, 0))
    return pl.pallas_call(
        functools.partial(_gather_kernel, n_inner=n_inner, cap=cap),
        grid=(batch, n_outer),
        in_specs=[sspec, sspec, pl.BlockSpec((seq, d), lambda b, e: (b, 0))],
        out_specs=[pl.BlockSpec((n_inner, cap, d), lambda b, e: (e, b, 0)),
                   pl.BlockSpec((n_inner, cap, 1), lambda b, e: (e, b, 0))],
        out_shape=[jax.ShapeDtypeStruct((n_exp, batch * cap, d), BF16),
                   jax.ShapeDtypeStruct((n_exp, batch * cap, 1), F32)],
        compiler_params=_params(2),
        name="moe_gather",
    )(slot4, gate4, h)


def _gather_win_kernel(base_ref, flag_ref, slot_ref, gate_ref, h_ref, xs_ref, gs_ref, *, cap):
    k = pl.program_id(1)
    step = pl.program_id(0) * pl.num_programs(1) + k
    n_exp, chunk = slot_ref.shape
    win = MOE_WINDOW

    @pl.when(k == 0)
    def _():
        xs_ref[...] = jnp.zeros(xs_ref.shape, xs_ref.dtype)
        gs_ref[...] = jnp.zeros(gs_ref.shape, gs_ref.dtype)

    h = h_ref[...]

    @pl.when(flag_ref[step] == 0)
    def _():
        row = lax.broadcasted_iota(jnp.int32, (win, chunk), 0)
        bases = [pl.multiple_of(base_ref[step * n_exp + e], 16) for e in range(n_exp)]
        hots = [(slot_ref[e:e + 1, :] - bases[e]) == row for e in range(n_exp)]
        stacked = jnp.concatenate([jnp.where(hot, 1.0, 0.0).astype(BF16) for hot in hots], axis=0)
        picked = _dot(stacked, h)
        for e in range(n_exp):
            rows = pl.ds(bases[e], win)
            xs_ref[e, rows, :] += picked[e * win:(e + 1) * win].astype(BF16)
            gs_ref[e, rows, :] += jnp.sum(jnp.where(hots[e], gate_ref[e:e + 1, :], 0.0), axis=1, keepdims=True)

    @pl.when(flag_ref[step] != 0)
    def _():
        row = lax.broadcasted_iota(jnp.int32, (cap, chunk), 0)
        for e in range(n_exp):
            hot = slot_ref[e:e + 1, :] == row
            xs_ref[e] += _dot(jnp.where(hot, 1.0, 0.0).astype(BF16), h).astype(BF16)
            gs_ref[e] += jnp.sum(jnp.where(hot, gate_ref[e:e + 1, :], 0.0), axis=1, keepdims=True)


def _gather_windowed(slot, gate, h, base, flag, batch, seq, cap):
    n_exp = slot.shape[1]
    d = h.shape[1]
    nk = seq // MOE_CHUNK
    sspec = pl.BlockSpec((None, n_exp, MOE_CHUNK), lambda b, k, *_: (b, 0, k))
    return pl.pallas_call(
        functools.partial(_gather_win_kernel, cap=cap),
        grid_spec=pltpu.PrefetchScalarGridSpec(
            num_scalar_prefetch=2,
            grid=(batch, nk),
            in_specs=[sspec, sspec, pl.BlockSpec((MOE_CHUNK, d), lambda b, k, *_: (b * nk + k, 0))],
            out_specs=[pl.BlockSpec((n_exp, cap, d), lambda b, k, *_: (0, b, 0)),
                       pl.BlockSpec((n_exp, cap, 1), lambda b, k, *_: (0, b, 0))]),
        out_shape=[jax.ShapeDtypeStruct((n_exp, batch * cap, d), BF16),
                   jax.ShapeDtypeStruct((n_exp, batch * cap, 1), F32)],
        compiler_params=_params(2),
        name="moe_gather_windowed",
    )(base, flag, slot, gate, h)


def _ffn_kernel(xc_ref, gc_ref, xl_ref, gl_ref, wg_hbm, wu_hbm, wd_hbm, y_ref, wg_s, wu_s, wd_s, stg_g, stg_u,
                stg_d, sems, *, layer, n_chunks, ctx_steps):
    e = pl.program_id(0)
    m = pl.program_id(1)
    slot = e % 2
    rows_gu = wg_s.shape[1] // n_chunks
    rows_d = wd_s.shape[1] // n_chunks

    def chunk_copies(expert, c):
        gu = pl.ds(pl.multiple_of(c * rows_gu, rows_gu), rows_gu)
        dn = pl.ds(pl.multiple_of(c * rows_d, rows_d), rows_d)
        return (pltpu.make_async_copy(wg_hbm.at[layer, expert, gu, :], stg_g, sems.at[0]),
                pltpu.make_async_copy(wu_hbm.at[layer, expert, gu, :], stg_u, sems.at[1]),
                pltpu.make_async_copy(wd_hbm.at[layer, expert, dn, :], stg_d, sems.at[2]))

    def land(dst_slot, c, copies):
        for cp in copies:
            cp.wait()
        gu = pl.ds(pl.multiple_of(c * rows_gu, rows_gu), rows_gu)
        dn = pl.ds(pl.multiple_of(c * rows_d, rows_d), rows_d)
        wg_s[dst_slot, gu, :] = stg_g[...].astype(BF16)
        wu_s[dst_slot, gu, :] = stg_u[...].astype(BF16)
        wd_s[dst_slot, dn, :] = stg_d[...].astype(BF16)

    @pl.when((e == 0) & (m == 0))
    def _():
        for c in range(n_chunks):
            copies = chunk_copies(0, c)
            for cp in copies:
                cp.start()
            land(0, c, copies)

    prefetch = (m < n_chunks) & (e + 1 < pl.num_programs(0))

    @pl.when(prefetch)
    def _():
        for cp in chunk_copies(e + 1, m):
            cp.start()

    def swiglu(x_ref, g_ref):
        x = x_ref[...]
        a = _dot(x, wg_s[slot])
        u = _dot(x, wu_s[slot])
        y = _dot((_silu(a) * u).astype(BF16), wd_s[slot])
        y_ref[...] = (y * g_ref[...]).astype(y_ref.dtype)

    @pl.when(m < ctx_steps)
    def _():
        swiglu(xc_ref, gc_ref)

    @pl.when(m >= ctx_steps)
    def _():
        swiglu(xl_ref, gl_ref)

    @pl.when(prefetch)
    def _():
        land(1 - slot, m, chunk_copies(e + 1, m))


def _expert_ffn(xs_c, gs_c, xs_l, gs_l, wg, wu, wd, layer):
    n_exp, rows_c, d = xs_c.shape
    rows_l = xs_l.shape[1]
    ff = wg.shape[3]
    tm = math.gcd(math.gcd(rows_c, rows_l), 512)
    ctx_steps = rows_c // tm
    steps = ctx_steps + rows_l // tm
    n_chunks = 4 if steps >= 4 else (2 if steps >= 2 else 1)
    any_spec = pl.BlockSpec(memory_space=pl.ANY)

    def ctx_spec(width):
        return pl.BlockSpec((None, tm, width), lambda e, i: (e, jnp.minimum(i, ctx_steps - 1), 0))

    def lat_spec(width):
        return pl.BlockSpec((None, tm, width), lambda e, i: (e, jnp.maximum(i - ctx_steps, 0), 0))

    return pl.pallas_call(
        functools.partial(_ffn_kernel, layer=layer, n_chunks=n_chunks, ctx_steps=ctx_steps),
        grid=(n_exp, steps),
        in_specs=[ctx_spec(d), ctx_spec(1), lat_spec(d), lat_spec(1), any_spec, any_spec, any_spec],
        out_specs=pl.BlockSpec((None, tm, d), lambda e, i: (e, i, 0)),
        out_shape=jax.ShapeDtypeStruct((n_exp, rows_c + rows_l, d), BF16),
        scratch_shapes=[pltpu.VMEM((2, d, ff), BF16), pltpu.VMEM((2, d, ff), BF16), pltpu.VMEM((2, ff, d), BF16),
                        pltpu.VMEM((d // n_chunks, ff), F32), pltpu.VMEM((d // n_chunks, ff), F32),
                        pltpu.VMEM((ff // n_chunks, d), F32), pltpu.SemaphoreType.DMA((3,))],
        compiler_params=_params(2),
        name="moe_ffn",
    )(xs_c, gs_c, xs_l, gs_l, wg, wu, wd)


def _combine_kernel(x_ref, y_ref, slot_ref, g2_ref, lg_ref, lb_ref, o_ref, *, alpha, cap):
    n_exp = y_ref.shape[0]
    tt = x_ref.shape[0]
    slot = slot_ref[...]
    lanes = 128
    if cap < lanes and lanes % cap == 0 and n_exp % (lanes // cap) == 0:
        per = lanes // cap
        lane = lax.broadcasted_iota(jnp.int32, (tt, lanes), 1)
        hots = []
        for g in range(n_exp // per):
            hit = None
            for j in range(per):
                s = slot[:, g * per + j:g * per + j + 1]
                match = jnp.where(s >= 0, s + j * cap, -1) == lane
                hit = match if hit is None else hit | match
            hots.append(jnp.where(hit, 1.0, 0.0).astype(BF16))
        acc = _dot(jnp.concatenate(hots, axis=1), y_ref[...].reshape(n_exp * cap, y_ref.shape[2]))
    else:
        col = lax.broadcasted_iota(jnp.int32, (tt, cap), 1)
        acc = jnp.zeros(x_ref.shape, F32)
        for e in range(n_exp):
            onehot = jnp.where(slot[:, e:e + 1] == col, 1.0, 0.0).astype(BF16)
            acc += _dot(onehot, y_ref[e])
    o_ref[...] = _ln(alpha * x_ref[...] + g2_ref[...] * acc) * lg_ref[...] + lb_ref[...]


def _combine(x1, y, slot_t, mod3, ln_g, ln_b, batch, seq, cap, row_blk0, per_request, alpha):
    rows, d = x1.shape
    n_exp = y.shape[0]
    tt = min(256, seq)
    nt = seq // tt
    vec = pl.BlockSpec((1, d), lambda b, i: (0, 0))
    return pl.pallas_call(
        functools.partial(_combine_kernel, alpha=alpha, cap=cap),
        grid=(batch, nt),
        in_specs=[pl.BlockSpec((tt, d), lambda b, i: (b * nt + i, 0)),
                  pl.BlockSpec((n_exp, cap, d), lambda b, i: (0, row_blk0 + b, 0)),
                  pl.BlockSpec((None, tt, n_exp), lambda b, i: (b, i, 0)),
                  pl.BlockSpec((None, 1, d), lambda b, i: (_mod_row(b, 1, per_request) * 6 + 5, 0, 0)),
                  vec, vec],
        out_specs=pl.BlockSpec((tt, d), lambda b, i: (b * nt + i, 0)),
        out_shape=jax.ShapeDtypeStruct((rows, d), F32),
        compiler_params=_params(2),
        name="moe_combine_postnorm",
    )(x1, y, slot_t, mod3, ln_g, ln_b)


def _combine_win_kernel(base_ref, flag_ref, x_ref, y_ref, slot_ref, g2_ref, lg_ref, lb_ref, o_ref, acc_ref,
                        *, alpha, cap):
    step = pl.program_id(0) * pl.num_programs(1) + pl.program_id(1)
    n_exp = y_ref.shape[0]
    tt = x_ref.shape[0]
    win = MOE_WINDOW
    slot = slot_ref[...]

    @pl.when(flag_ref[step] == 0)
    def _():
        lane = lax.broadcasted_iota(jnp.int32, (tt, 2 * win), 1)
        hots, ys = [], []
        for pair in range(n_exp // 2):
            picks = []
            for half in range(2):
                e = 2 * pair + half
                base = pl.multiple_of(base_ref[step * n_exp + e], 16)
                rel = slot[:, e:e + 1] - base
                picks.append(jnp.where((rel >= 0) & (rel < win), rel + half * win, -1))
                ys.append(y_ref[e, pl.ds(base, win), :])
            hots.append(jnp.where((picks[0] == lane) | (picks[1] == lane), 1.0, 0.0).astype(BF16))
        acc_ref[...] = _dot(jnp.concatenate(hots, axis=1), jnp.concatenate(ys, axis=0))

    @pl.when(flag_ref[step] != 0)
    def _():
        col = lax.broadcasted_iota(jnp.int32, (tt, cap), 1)
        acc = jnp.zeros(x_ref.shape, F32)
        for e in range(n_exp):
            acc += _dot(jnp.where(slot[:, e:e + 1] == col, 1.0, 0.0).astype(BF16), y_ref[e])
        acc_ref[...] = acc

    o_ref[...] = _ln(alpha * x_ref[...] + g2_ref[...] * acc_ref[...]) * lg_ref[...] + lb_ref[...]


def _combine_windowed(x1, y, slot_t, base, flag, mod3, ln_g, ln_b, batch, seq, cap, row_blk0, alpha):
    rows, d = x1.shape
    n_exp = y.shape[0]
    tt = MOE_CHUNK
    nt = seq // tt
    assert 2 * MOE_WINDOW == 128 and n_exp % 2 == 0
    vec = pl.BlockSpec((1, d), lambda b, i, *_: (0, 0))
    return pl.pallas_call(
        functools.partial(_combine_win_kernel, alpha=alpha, cap=cap),
        grid_spec=pltpu.PrefetchScalarGridSpec(
            num_scalar_prefetch=2,
            grid=(batch, nt),
            in_specs=[pl.BlockSpec((tt, d), lambda b, i, *_: (b * nt + i, 0)),
                      pl.BlockSpec((n_exp, cap, d), lambda b, i, *_: (0, row_blk0 + b, 0)),
                      pl.BlockSpec((None, tt, n_exp), lambda b, i, *_: (b, i, 0)),
                      pl.BlockSpec((None, 1, d), lambda b, i, *_: (_mod_row(b, 1, True) * 6 + 5, 0, 0)),
                      vec, vec],
            out_specs=pl.BlockSpec((tt, d), lambda b, i, *_: (b * nt + i, 0)),
            scratch_shapes=[pltpu.VMEM((tt, d), F32)]),
        out_shape=jax.ShapeDtypeStruct((rows, d), F32),
        compiler_params=_params(2),
        name="moe_combine_windowed_postnorm",
    )(base, flag, x1, y, slot_t, mod3, ln_g, ln_b)


def kernel(x_prompt, x_sample, cache_na_k, cache_na_v, cache_gqa_k, cache_gqa_v, state_ret_fwd, state_ret_bwd,
           c, c_ctx, w_in, w_out, w_mod, b_mod, ln1_g, ln1_b, ln2_g, ln2_b, q_norm_g, k_norm_g, na_rpb,
           ret_decay_fwd, ret_decay_bwd, w_router, w_gate, w_up, w_down):
    batch, seq, d = x_prompt.shape
    dbatch, dseq, _ = x_sample.shape
    depth = w_in.shape[0]
    na_heads = cache_na_k.shape[2]
    kv_heads = cache_gqa_k.shape[2]
    ret_heads = state_ret_fwd.shape[2]
    n_exp = w_router.shape[2]
    hd = HEAD_DIM
    gqa_heads = (w_in.shape[2] // hd - 3 * na_heads - 2 * kv_heads - 4 * ret_heads)
    group = gqa_heads // kv_heads
    alpha = float((2 * depth) ** 0.25)
    c_na = 0
    c_gq = 3 * na_heads
    c_gk = c_gq + gqa_heads
    c_gv = c_gk + kv_heads
    c_rt = c_gv + kv_heads
    cap_c = EC_CAPACITY_FACTOR * seq // n_exp
    cap_l = EC_CAPACITY_FACTOR * dseq // n_exp
    assert dbatch + 1 <= MOD_ROWS and (batch * cap_c) % cap_l == 0
    lat_blk0 = batch * cap_c // cap_l

    cond = jnp.zeros((MOD_ROWS, d), F32).at[0].set(c_ctx).at[1:1 + dbatch].set(c)
    mod_all = _modulation(cond, w_mod, b_mod)
    tables = _rope_tables(dseq)
    bias = _na_bias(na_rpb, dseq // GRID_W)

    w_in_b, w_out_b = w_in.astype(BF16), w_out.astype(BF16)

    xp = x_prompt.reshape(batch * seq, d)
    xs = x_sample.reshape(dbatch * dseq, d)
    outs = [[] for _ in range(6)]
    for l in range(depth):
        mod3 = mod_all[l].reshape(MOD_ROWS * 6, 1, d)
        wr_t = jnp.concatenate(_split_bf16(w_router[l]), axis=1)
        gains = (q_norm_g[l].reshape(1, hd), k_norm_g[l].reshape(1, hd))
        l1g, l1b = ln1_g[l].reshape(1, d), ln1_b[l].reshape(1, d)
        l2g, l2b = ln2_g[l].reshape(1, d), ln2_b[l].reshape(1, d)

        zc = _in_projection(xp, mod3, w_in_b, l, seq, False, F32)
        (na_c,) = _ctx_attention(zc, batch, seq, na_heads, 1, c_na, c_na + na_heads, c_na + 2 * na_heads)
        gq_c, gk_n = _ctx_attention(zc, batch, seq, kv_heads, group, c_gq, c_gk, c_gv, gains)
        rt_c, s_f, s_b = _retention(zc, batch, seq, ret_heads, c_rt, ret_decay_fwd[l], ret_decay_bwd[l],
                                    state_out=True)

        def heads_of(col, n):
            return zc[:, col * hd:(col + n) * hd].reshape(batch, seq, n, hd).transpose(0, 2, 1, 3)

        outs[0].append(heads_of(c_na + na_heads, na_heads))
        outs[1].append(heads_of(c_na + 2 * na_heads, na_heads))
        outs[2].append(gk_n)
        outs[3].append(heads_of(c_gv, kv_heads))
        outs[4].append(s_f)
        outs[5].append(s_b)

        zl = _in_projection(xs, mod3, w_in_b, l, dseq, True, BF16)
        (na_l,) = [_na_attention(zl, dbatch, dseq, na_heads, cache_na_k, cache_na_v, l, bias)]
        gq_l = _gqa_attention(zl, dbatch, dseq, kv_heads, group, c_gq, c_gk, c_gv, cache_gqa_k, cache_gqa_v, l,
                              gains, tables)
        (rt_l,) = _retention(zl, dbatch, dseq, ret_heads, c_rt, ret_decay_fwd[l], ret_decay_bwd[l],
                             states=((state_ret_fwd, state_ret_bwd), l))

        xp1, hp, lt_c = _out_projection(xp, na_c, gq_c, rt_c, w_out_b, l, mod3, l1g, l1b, wr_t, seq, False, alpha)
        xs1, hs, lt_l = _out_projection(xs, na_l, gq_l, rt_l, w_out_b, l, mod3, l1g, l1b, wr_t, dseq, True, alpha)

        slot_c, gate_c, _ = _route(lt_c.T, batch, seq, cap_c)
        slot_l, gate_l, before_l = _route(lt_l.T, dbatch, dseq, cap_l)
        windowed = cap_l > MOE_WINDOW and dseq % MOE_CHUNK == 0
        if windowed:
            base_l, flag_l = _window_plan(before_l, cap_l, dseq)
            rows_l, gates_l = _gather_windowed(slot_l, gate_l, hs, base_l, flag_l, dbatch, dseq, cap_l)
        else:
            rows_l, gates_l = _gather(slot_l, gate_l, hs, dbatch, dseq, cap_l, 2)
        rows_c, gates_c = _gather(slot_c, gate_c, hp, batch, seq, cap_c, n_exp)
        y = _expert_ffn(rows_c, gates_c, rows_l, gates_l, w_gate, w_up, w_down, l)
        xp = _combine(xp1, y, slot_c.transpose(0, 2, 1), mod3, l2g, l2b, batch, seq, cap_c, 0, False, alpha)
        if windowed:
            xs = _combine_windowed(xs1, y, slot_l.transpose(0, 2, 1), base_l, flag_l, mod3, l2g, l2b, dbatch, dseq,
                                   cap_l, lat_blk0, alpha)
        else:
            xs = _combine(xs1, y, slot_l.transpose(0, 2, 1), mod3, l2g, l2b, dbatch, dseq, cap_l, lat_blk0, True,
                          alpha)

    stacked = [jnp.stack(o, axis=1) for o in outs]
    return (xp.reshape(batch, seq, d), xs.reshape(dbatch, dseq, d), *stacked)
```

```python
import functools
import math

import numpy as np
import jax
import jax.numpy as jnp
from jax import lax
from jax.experimental import pallas as pl
from jax.experimental.pallas import tpu as pltpu

F32 = jnp.float32
BF16 = jnp.bfloat16

HEAD_DIM = 128
GRID_W = 64
NA_WIN_R = 8
NA_WIN_C = 16
ROPE_THETA = 10000.0
EC_CAPACITY_FACTOR = 2
LN_EPS = 1e-5
RMS_EPS = 1e-6
NEG_INF = -1e30
ATTN_SCALE = HEAD_DIM ** -0.5
LOG2E = 1.4426950408889634
NA_BLOCK_ROWS = 4
ROUTE_GROUP = 16
MOE_CHUNK = 256
MOE_WINDOW = 64
MOD_ROWS = 16
VMEM_LIMIT = 56 * 1024 * 1024

_NT = (((1,), (1,)), ((), ()))
_NN = (((1,), (0,)), ((), ()))


def _params(n_grid, vmem=VMEM_LIMIT):
    return pltpu.CompilerParams(dimension_semantics=("arbitrary",) * n_grid, vmem_limit_bytes=vmem)


def _dot(a, b, dims=_NN):
    return lax.dot_general(a, b, dims, preferred_element_type=F32)


def _split_bf16(x):
    hi = x.astype(BF16)
    lo = (x - hi.astype(F32)).astype(BF16)
    return hi, lo


def _dot3(a, b, dims=_NN):
    ah, al = _split_bf16(a)
    bh, bl = _split_bf16(b)
    return _dot(ah, bh, dims) + _dot(al, bh, dims) + _dot(ah, bl, dims)


def _silu(x):
    return x / (1.0 + jnp.exp(-x))


def _ln(x):
    mu = jnp.mean(x, axis=-1, keepdims=True)
    xc = x - mu
    var = jnp.mean(xc * xc, axis=-1, keepdims=True)
    return xc * lax.rsqrt(var + LN_EPS)


def _rms(x, g):
    return x * lax.rsqrt(jnp.mean(x * x, axis=-1, keepdims=True) + RMS_EPS) * g


def _rope(x, cos, sin_signed):
    lane = lax.broadcasted_iota(jnp.int32, x.shape, 1) & (HEAD_DIM // 2 - 1)
    partner = jnp.where(lane < HEAD_DIM // 4,
                        pltpu.roll(x, HEAD_DIM - HEAD_DIM // 4, 1),
                        pltpu.roll(x, HEAD_DIM // 4, 1))
    return x * cos + partner * sin_signed


def _softmax_pv(scores, values):
    m = functools.reduce(jnp.maximum, [jnp.max(s, axis=-1, keepdims=True) for s in scores])
    ps = [jnp.exp(s - m) for s in scores]
    denom = functools.reduce(lambda a, b: a + b, [jnp.sum(p, axis=-1, keepdims=True) for p in ps])
    o = functools.reduce(lambda a, b: a + b, [_dot(p.astype(BF16), v) for p, v in zip(ps, values)])
    return o / denom


def _mod_kernel(c_ref, w_ref, b_ref, o_ref):
    a = _silu(c_ref[...])
    o_ref[...] = _dot3(a, w_ref[...]) + b_ref[...]


def _modulation(cond, w_mod, b_mod):
    depth, d, n = w_mod.shape
    tn = 768
    return pl.pallas_call(
        _mod_kernel,
        grid=(depth, n // tn),
        in_specs=[pl.BlockSpec((MOD_ROWS, d), lambda l, j: (0, 0)),
                  pl.BlockSpec((None, d, tn), lambda l, j: (l, 0, j)),
                  pl.BlockSpec((None, 1, tn), lambda l, j: (l, 0, j))],
        out_specs=pl.BlockSpec((None, MOD_ROWS, tn), lambda l, j: (l, 0, j)),
        out_shape=jax.ShapeDtypeStruct((depth, MOD_ROWS, n), F32),
        compiler_params=_params(2),
        name="adaln_mod",
    )(cond, w_mod, b_mod.reshape(depth, 1, n))


def _inproj_kernel(x_ref, sh_ref, sc_ref, w_ref, z_ref, *, sub):
    for r in range(x_ref.shape[0] // sub):
        rs = slice(r * sub, (r + 1) * sub)
        h = _ln(x_ref[rs, :]) * (1.0 + sc_ref[...]) + sh_ref[...]
        z_ref[rs, :] = _dot(h.astype(BF16), w_ref[...]).astype(z_ref.dtype)


def _mod_row(block, blocks_per_request, per_request):
    return 1 + block // blocks_per_request if per_request else 0


def _in_projection(x, mod3, w_bf16, layer, tokens_per_batch, per_request, out_dtype):
    rows, d = x.shape
    n = w_bf16.shape[2]
    tm = min(512, tokens_per_batch)
    tn = 2560
    per = tokens_per_batch // tm

    def mod_spec(chunk):
        return pl.BlockSpec((None, 1, d), lambda j, i: (_mod_row(i, per, per_request) * 6 + chunk, 0, 0))

    return pl.pallas_call(
        functools.partial(_inproj_kernel, sub=min(256, tm)),
        grid=(n // tn, rows // tm),
        in_specs=[pl.BlockSpec((tm, d), lambda j, i: (i, 0)),
                  mod_spec(0), mod_spec(1),
                  pl.BlockSpec((None, d, tn), lambda j, i: (layer, 0, j))],
        out_specs=pl.BlockSpec((tm, tn), lambda j, i: (i, j)),
        out_shape=jax.ShapeDtypeStruct((rows, n), out_dtype),
        compiler_params=_params(2),
        name="ln_inproj",
    )(x, mod3, mod3, w_bf16)


def _ctx_attn_kernel(*refs, group, normed):
    if normed:
        q_ref, k_ref, v_ref, qg_ref, kg_ref, o_ref, kn_ref = refs
    else:
        q_ref, k_ref, v_ref, o_ref = refs
    for hh in range(k_ref.shape[1] // HEAD_DIM):
        k = k_ref[:, hh * HEAD_DIM:(hh + 1) * HEAD_DIM]
        if normed:
            k = _rms(k, kg_ref[...])
            kn_ref[...] = k
        kb = k.astype(BF16)
        vb = v_ref[:, hh * HEAD_DIM:(hh + 1) * HEAD_DIM].astype(BF16)
        for g in range(group):
            cols = slice((hh * group + g) * HEAD_DIM, (hh * group + g + 1) * HEAD_DIM)
            q = q_ref[:, cols]
            if normed:
                q = _rms(q, qg_ref[...])
            s = _dot(q.astype(BF16), kb, _NT) * ATTN_SCALE
            o_ref[:, cols] = _softmax_pv([s], [vb]).astype(o_ref.dtype)


def _ctx_attention(z, batch, seq, heads, group, q_col, k_col, v_col, gains=None):
    normed = gains is not None
    hps = 1 if normed else heads
    assert q_col % (group * hps) == 0 and k_col % hps == 0 and v_col % hps == 0
    qw = group * hps * HEAD_DIM
    kw = hps * HEAD_DIM
    in_specs = [pl.BlockSpec((seq, qw), lambda b, h: (b, q_col // (group * hps) + h)),
                pl.BlockSpec((seq, kw), lambda b, h: (b, k_col // hps + h)),
                pl.BlockSpec((seq, kw), lambda b, h: (b, v_col // hps + h))]
    args = [z, z, z]
    out_specs = [pl.BlockSpec((seq, qw), lambda b, h: (b, h))]
    out_shape = [jax.ShapeDtypeStruct((batch * seq, heads * group * HEAD_DIM), BF16)]
    if normed:
        in_specs += [pl.BlockSpec((1, HEAD_DIM), lambda b, h: (0, 0))] * 2
        args += list(gains)
        out_specs.append(pl.BlockSpec((None, None, seq, HEAD_DIM), lambda b, h: (b, h, 0, 0)))
        out_shape.append(jax.ShapeDtypeStruct((batch, heads, seq, HEAD_DIM), F32))
    return pl.pallas_call(
        functools.partial(_ctx_attn_kernel, group=group, normed=normed),
        grid=(batch, heads // hps),
        in_specs=in_specs, out_specs=out_specs, out_shape=out_shape,
        compiler_params=_params(2),
        name="ctx_gqa" if normed else "ctx_na",
    )(*args)


def _log_sigmoid(x):
    return -jnp.log1p(jnp.exp(-x))


def _retention_kernel(*refs, seq, tq, with_state_in, with_state_out):
    q_ref, k_ref, v_ref, g_ref, df_ref, db_ref = refs[:6]
    refs = refs[6:]
    if with_state_in:
        sf_ref, sb_ref = refs[:2]
        refs = refs[2:]
    o_ref = refs[0]
    decay_ref = refs[-1]
    lgf = _log_sigmoid(df_ref[...])
    lgb = _log_sigmoid(db_ref[...])
    t0 = pl.program_id(1) * tq

    @pl.when(pl.program_id(2) == 0)
    def _():
        i = t0 + lax.broadcasted_iota(jnp.int32, (tq, seq), 0)
        j = lax.broadcasted_iota(jnp.int32, (tq, seq), 1)
        dist = (i - j).astype(F32)
        decay = jnp.where(dist == 0.0, 2.0, jnp.exp(jnp.where(dist > 0.0, lgf * dist, -lgb * dist)))
        decay_ref[...] = decay * ATTN_SCALE

    q = q_ref[...]
    kb = k_ref[...].astype(BF16)
    vb = v_ref[...].astype(BF16)
    att = _dot(q.astype(BF16), kb, _NT) * decay_ref[...]
    o = _dot(att.astype(BF16), vb)
    if with_state_in:
        t = (t0 + lax.broadcasted_iota(jnp.int32, (tq, 1), 0)).astype(F32)
        qf = q.astype(F32)
        o += _dot((qf * jnp.exp(lgf * (t + 1.0))).astype(BF16), sf_ref[...].astype(BF16))
        o += _dot((qf * jnp.exp(lgb * (seq - t))).astype(BF16), sb_ref[...].astype(BF16))
    o_ref[...] = (_ln(o) * _silu(g_ref[...].astype(F32))).astype(o_ref.dtype)
    if with_state_out:
        nsf_ref, nsb_ref = refs[1:3]
        tj = lax.broadcasted_iota(jnp.int32, (seq, 1), 0).astype(F32)
        kf = k_ref[...].astype(F32) * ATTN_SCALE
        kdf = (kf * jnp.exp(lgf * (seq - 1.0 - tj))).T.astype(BF16)
        kdb = (kf * jnp.exp(lgb * tj)).T.astype(BF16)
        nsf_ref[...] = _dot(kdf, vb)
        nsb_ref[...] = _dot(kdb, vb)


def _retention(z, batch, seq, heads, col0, dec_f, dec_b, states=None, state_out=False):
    tq = min(512, seq)
    nq = seq // tq
    assert not (state_out and nq != 1)
    in_specs = [pl.BlockSpec((tq, HEAD_DIM), lambda h, i, b: (b * nq + i, col0 + h)),
                pl.BlockSpec((seq, HEAD_DIM), lambda h, i, b: (b, col0 + heads + h)),
                pl.BlockSpec((seq, HEAD_DIM), lambda h, i, b: (b, col0 + 2 * heads + h)),
                pl.BlockSpec((tq, HEAD_DIM), lambda h, i, b: (b * nq + i, col0 + 3 * heads + h)),
                pl.BlockSpec((None, 1, 1), lambda h, i, b: (h, 0, 0)),
                pl.BlockSpec((None, 1, 1), lambda h, i, b: (h, 0, 0))]
    args = [z, z, z, z, dec_f.reshape(heads, 1, 1), dec_b.reshape(heads, 1, 1)]
    if states is not None:
        (sf, sb), layer = states
        spec = pl.BlockSpec((None, None, None, HEAD_DIM, HEAD_DIM), lambda h, i, b: (b, layer, h, 0, 0))
        in_specs += [spec, spec]
        args += [sf, sb]
    out_specs = [pl.BlockSpec((tq, HEAD_DIM), lambda h, i, b: (b * nq + i, h))]
    out_shape = [jax.ShapeDtypeStruct((batch * seq, heads * HEAD_DIM), BF16)]
    if state_out:
        spec = pl.BlockSpec((None, None, HEAD_DIM, HEAD_DIM), lambda h, i, b: (b, h, 0, 0))
        out_specs += [spec, spec]
        out_shape += [jax.ShapeDtypeStruct((batch, heads, HEAD_DIM, HEAD_DIM), F32)] * 2
    return pl.pallas_call(
        functools.partial(_retention_kernel, seq=seq, tq=tq, with_state_in=states is not None,
                          with_state_out=state_out),
        grid=(heads, nq, batch),
        in_specs=in_specs, out_specs=out_specs, out_shape=out_shape,
        scratch_shapes=[pltpu.VMEM((tq, seq), F32)],
        compiler_params=_params(3),
        name="retention",
    )(*args)


def _with_ones(v):
    return jnp.concatenate([v, jnp.ones(v.shape, v.dtype)], axis=1)


def _exp2_pv(scores, values_with_ones):
    m = functools.reduce(jnp.maximum, [jnp.max(s, axis=-1, keepdims=True) for s in scores])
    r = functools.reduce(lambda a, b: a + b,
                         [_dot(jnp.exp2(s - m).astype(BF16), v) for s, v in zip(scores, values_with_ones)])
    return r[:, :HEAD_DIM] / r[:, HEAD_DIM:HEAD_DIM + 1]


def _na_plan(rows):
    kr = min(NA_WIN_R, rows)
    nq = min(NA_BLOCK_ROWS, rows)
    nk = min(nq + kr - 1, rows)
    assert rows % nq == 0
    blocks, cases = [], []
    for r0 in range(0, rows, nq):
        start = int(np.clip(r0 - kr // 2, 0, rows - nk))
        win = [int(np.clip(r - kr // 2, 0, rows - kr)) for r in range(r0, r0 + nq)]
        assert all(start <= w and w + kr <= start + nk for w in win)
        key = (start - r0,) + tuple(w - start for w in win)
        if key not in cases:
            cases.append(key)
        blocks.append((r0, start, cases.index(key)))
    return kr, nq, nk, blocks, cases


def _na_kernel(q_ref, k_ref, v_ref, ck_ref, cv_ref, bias_ref, o_ref, vx_ref, cvx_ref, *, blocks, nq, nk):
    vx_ref[...] = _with_ones(v_ref[...])
    cvx_ref[...] = _with_ones(cv_ref[...].astype(BF16))
    ck = ck_ref[...].astype(BF16)
    c = ATTN_SCALE * LOG2E
    for r0, start, case in blocks:
        qs = slice(r0 * GRID_W, (r0 + nq) * GRID_W)
        ks = slice(start * GRID_W, (start + nk) * GRID_W)
        q = q_ref[qs, :]
        s_loc = _dot(q, k_ref[ks, :], _NT) * c + bias_ref[case]
        s_ctx = _dot(q, ck, _NT) * c
        o_ref[qs, :] = _exp2_pv([s_loc, s_ctx], [vx_ref[ks, :], cvx_ref[...]]).astype(o_ref.dtype)


def _na_bias(rpb, rows):
    kr, nq, nk, _, cases = _na_plan(rows)
    dr = np.zeros((len(cases), nq, nk), np.int32)
    row_ok = np.zeros((len(cases), nq, nk), bool)
    for c, key in enumerate(cases):
        rel, offs = key[0], key[1:]
        for ri in range(nq):
            for ju in range(nk):
                row_ok[c, ri, ju] = 0 <= ju - offs[ri] < kr
                dr[c, ri, ju] = np.clip(rel + ju - ri + NA_WIN_R - 1, 0, 2 * NA_WIN_R - 2)
    cq = np.arange(GRID_W)
    ck = np.arange(GRID_W)
    col_start = np.clip(cq - NA_WIN_C // 2, 0, GRID_W - NA_WIN_C)
    col_ok = (ck[None, :] >= col_start[:, None]) & (ck[None, :] < col_start[:, None] + NA_WIN_C)
    dc = np.clip(ck[None, :] - cq[:, None] + (NA_WIN_C - 1), 0, 2 * NA_WIN_C - 2)
    pick_col = (dc.reshape(-1)[None, :] == np.arange(2 * NA_WIN_C - 1)[:, None]).astype(np.float32)
    n_l, n_h = rpb.shape[:2]
    t = jnp.take(rpb.astype(F32), dr.reshape(-1), axis=2)
    t = jnp.einsum("lhxb,bn->lhxn", t, pick_col, precision=lax.Precision.HIGHEST)
    t = t.reshape(n_l, n_h, len(cases), nq, nk, GRID_W, GRID_W).transpose(0, 1, 2, 3, 5, 4, 6)
    ok = row_ok[:, :, None, :, None] & col_ok[None, None, :, None, :]
    t = jnp.where(ok, t * LOG2E, NEG_INF)
    return t.reshape(n_l, n_h, len(cases), nq * GRID_W, nk * GRID_W)


def _na_attention(z, batch, seq, heads, cache_k, cache_v, layer, bias):
    _, nq, nk, blocks, cases = _na_plan(seq // GRID_W)
    past = cache_k.shape[3]
    cspec = pl.BlockSpec((None, None, None, past, HEAD_DIM), lambda b, h: (b, layer, h, 0, 0))
    return pl.pallas_call(
        functools.partial(_na_kernel, blocks=blocks, nq=nq, nk=nk),
        grid=(batch, heads),
        in_specs=[pl.BlockSpec((seq, HEAD_DIM), lambda b, h: (b, h)),
                  pl.BlockSpec((seq, HEAD_DIM), lambda b, h: (b, heads + h)),
                  pl.BlockSpec((seq, HEAD_DIM), lambda b, h: (b, 2 * heads + h)),
                  cspec, cspec,
                  pl.BlockSpec((None, None, len(cases), nq * GRID_W, nk * GRID_W),
                               lambda b, h: (layer, h, 0, 0, 0))],
        out_specs=pl.BlockSpec((seq, HEAD_DIM), lambda b, h: (b, h)),
        out_shape=jax.ShapeDtypeStruct((batch * seq, heads * HEAD_DIM), BF16),
        scratch_shapes=[pltpu.VMEM((seq, 2 * HEAD_DIM), BF16), pltpu.VMEM((past, 2 * HEAD_DIM), BF16)],
        compiler_params=_params(2),
        name="latent_na",
    )(z, z, z, cache_k, cache_v, bias)


def _gqa_kernel(q_ref, k_ref, v_ref, ck_ref, cv_ref, qg_ref, kg_ref, cos_ref, sin_ref, o_ref, kf_ref, vf_ref,
                *, group, past, seq, tq):
    i = pl.program_id(2)

    @pl.when(i == 0)
    def _():
        kf_ref[0:past, :] = ck_ref[...].astype(BF16)
        vf_ref[0:past, :] = _with_ones(cv_ref[...].astype(BF16))
        k = _rope(_rms(k_ref[...].astype(F32), kg_ref[...]), cos_ref[...], sin_ref[...])
        kf_ref[past:past + seq, :] = k.astype(BF16)
        vf_ref[past:past + seq, :] = _with_ones(v_ref[...])

    t0 = pl.multiple_of(i * tq, tq)
    cos = cos_ref[pl.ds(t0, tq), :]
    sin = sin_ref[pl.ds(t0, tq), :]
    kf = kf_ref[...]
    vf = vf_ref[...]
    def scores(g):
        q = q_ref[:, g * HEAD_DIM:(g + 1) * HEAD_DIM].astype(F32)
        q = _rope(_rms(q, qg_ref[...]), cos, sin) * (ATTN_SCALE * LOG2E)
        return _dot(q.astype(BF16), kf, _NT)

    s = scores(0)
    for g in range(group):
        s_next = scores(g + 1) if g + 1 < group else None
        o_ref[:, g * HEAD_DIM:(g + 1) * HEAD_DIM] = _exp2_pv([s], [vf]).astype(o_ref.dtype)
        s = s_next


def _rope_tables(seq):
    t = np.arange(seq)
    half = HEAD_DIM // 2
    inv = 1.0 / (ROPE_THETA ** (np.arange(0, half, 2, dtype=np.float32) / half))
    row = (t // GRID_W).astype(np.float32)
    col = (t % GRID_W).astype(np.float32)
    ang = jnp.concatenate([jnp.asarray(row[:, None] * inv)] * 2 + [jnp.asarray(col[:, None] * inv)] * 2, axis=-1)
    sign = np.where((np.arange(HEAD_DIM) % half) < half // 2, -1.0, 1.0).astype(np.float32)
    return jnp.cos(ang), jnp.sin(ang) * sign


def _gqa_attention(z, batch, seq, kv_heads, group, q_col, k_col, v_col, cache_k, cache_v, layer, gains, tables):
    past = cache_k.shape[3]
    tq = min(256, seq)
    nq = seq // tq
    qw = group * HEAD_DIM
    cspec = pl.BlockSpec((None, None, None, past, HEAD_DIM), lambda b, h, i: (b, layer, h, 0, 0))
    gspec = pl.BlockSpec((1, HEAD_DIM), lambda b, h, i: (0, 0))
    tspec = pl.BlockSpec((seq, HEAD_DIM), lambda b, h, i: (0, 0))
    return pl.pallas_call(
        functools.partial(_gqa_kernel, group=group, past=past, seq=seq, tq=tq),
        grid=(batch, kv_heads, nq),
        in_specs=[pl.BlockSpec((tq, qw), lambda b, h, i: (b * nq + i, q_col // group + h)),
                  pl.BlockSpec((seq, HEAD_DIM), lambda b, h, i: (b, k_col + h)),
                  pl.BlockSpec((seq, HEAD_DIM), lambda b, h, i: (b, v_col + h)),
                  cspec, cspec, gspec, gspec, tspec, tspec],
        out_specs=pl.BlockSpec((tq, qw), lambda b, h, i: (b * nq + i, h)),
        out_shape=jax.ShapeDtypeStruct((batch * seq, kv_heads * qw), BF16),
        scratch_shapes=[pltpu.VMEM((past + seq, HEAD_DIM), BF16), pltpu.VMEM((past + seq, 2 * HEAD_DIM), BF16)],
        compiler_params=_params(3),
        name="latent_gqa",
    )(z, z, z, cache_k, cache_v, gains[0], gains[1], tables[0], tables[1])


def _outproj_kernel(x_ref, na_ref, gq_ref, rt_ref, w_ref, g1_ref, sh2_ref, sc2_ref, lg_ref, lb_ref, wr_ref,
                    x1_ref, h2_ref, lt_ref, *, alpha, sub):
    n_exp = lt_ref.shape[1]
    for r in range(x_ref.shape[0] // sub):
        rs = slice(r * sub, (r + 1) * sub)
        mix = jnp.concatenate([na_ref[rs, :], gq_ref[rs, :], rt_ref[rs, :]], axis=1)
        y = _dot(mix, w_ref[...])
        x1 = _ln(alpha * x_ref[rs, :] + g1_ref[...] * y) * lg_ref[...] + lb_ref[...]
        x1_ref[rs, :] = x1
        h2 = _ln(x1) * (1.0 + sc2_ref[...]) + sh2_ref[...]
        hi, lo = _split_bf16(h2)
        h2_ref[rs, :] = hi
        both = _dot(hi, wr_ref[...])
        lt_ref[rs, :] = both[:, :n_exp] + both[:, n_exp:] + _dot(lo, wr_ref[:, 0:n_exp])


def _out_projection(x, na_o, gq_o, rt_o, w_bf16, layer, mod3, ln_g, ln_b, wr_hl, tokens_per_batch, per_request,
                    alpha):
    rows, d = x.shape
    tm = 512
    per = tokens_per_batch // tm if per_request else 1
    n_exp = wr_hl.shape[1] // 2

    def mod_spec(chunk):
        return pl.BlockSpec((None, 1, d), lambda i: (_mod_row(i, per, per_request) * 6 + chunk, 0, 0))

    def row_spec(width):
        return pl.BlockSpec((tm, width), lambda i: (i, 0))

    vec = pl.BlockSpec((1, d), lambda i: (0, 0))
    return pl.pallas_call(
        functools.partial(_outproj_kernel, alpha=alpha, sub=256),
        grid=(rows // tm,),
        in_specs=[row_spec(d), row_spec(na_o.shape[1]), row_spec(gq_o.shape[1]), row_spec(rt_o.shape[1]),
                  pl.BlockSpec((None,) + w_bf16.shape[1:], lambda i: (layer, 0, 0)),
                  mod_spec(2), mod_spec(3), mod_spec(4), vec, vec,
                  pl.BlockSpec(wr_hl.shape, lambda i: (0, 0))],
        out_specs=[row_spec(d), row_spec(d), row_spec(n_exp)],
        out_shape=[jax.ShapeDtypeStruct((rows, d), F32), jax.ShapeDtypeStruct((rows, d), BF16),
                   jax.ShapeDtypeStruct((rows, n_exp), F32)],
        compiler_params=_params(1),
        name="outproj_postnorm_router",
    )(x, na_o, gq_o, rt_o, w_bf16, mod3, mod3, mod3, ln_g, ln_b, wr_hl)


def _topk_kernel(lt_ref, slot_ref, gate_ref, before_ref, *, cap, seq, group):
    affs = []
    for g in range(group):
        logits = lt_ref[:, g * seq:(g + 1) * seq]
        m = jnp.max(logits, axis=0, keepdims=True)
        ex = jnp.exp(logits - m)
        affs.append(ex / jnp.sum(ex, axis=0, keepdims=True))
    aff = jnp.concatenate(affs, axis=0)
    n_exp = aff.shape[0]
    bits = lax.bitcast_convert_type(aff, jnp.int32)

    def count(mask):
        return jnp.sum(jnp.where(mask, 1.0, 0.0), axis=1, keepdims=True)

    def value_step(it, thr):
        cand = thr | jnp.left_shift(jnp.int32(1), 30 - it)
        return jnp.where(count(bits >= cand) >= cap, cand, thr)

    thr = lax.fori_loop(0, 31, value_step, jnp.zeros((n_exp, 1), jnp.int32))
    above = bits > thr
    tied = bits == thr
    need = cap - count(above)
    tok = lax.broadcasted_iota(jnp.int32, (n_exp, seq), 1)
    n_bits = int(seq - 1).bit_length()

    def index_step(it, bound):
        cand = bound | jnp.left_shift(jnp.int32(1), n_bits - 1 - it)
        return jnp.where(count(tied & (tok < cand)) < need, cand, bound)

    bound = lax.fori_loop(0, n_bits, index_step, jnp.zeros((n_exp, 1), jnp.int32))
    sel = above | (tied & (tok <= bound))
    self = jnp.where(sel, 1.0, 0.0)
    lanes = 128
    upper = jnp.where(lax.broadcasted_iota(jnp.int32, (lanes, lanes), 0)
                      < lax.broadcasted_iota(jnp.int32, (lanes, lanes), 1), 1.0, 0.0).astype(BF16)
    running = jnp.zeros((n_exp, 1), F32)
    lane = lax.broadcasted_iota(jnp.int32, (n_exp, lanes), 1)
    before = jnp.zeros((n_exp, lanes), F32)
    for blk in range(seq // lanes):
        sl = slice(blk * lanes, (blk + 1) * lanes)
        chunk = self[:, sl]
        before = jnp.where(lane == blk, running, before)
        pos = _dot(chunk.astype(BF16), upper) + running
        slot_ref[:, sl] = jnp.where(chunk > 0.0, pos.astype(jnp.int32), -1)
        running = running + jnp.sum(chunk, axis=1, keepdims=True)
    gate_ref[...] = aff
    before_ref[...] = before.astype(jnp.int32)


def _route(logits_t, batch, seq, cap):
    n_exp = logits_t.shape[0]
    assert seq // 128 <= 128
    group = math.gcd(batch, ROUTE_GROUP)
    rows = group * n_exp
    spec = pl.BlockSpec((rows, seq), lambda i: (i, 0))
    slot, gate, before = pl.pallas_call(
        functools.partial(_topk_kernel, cap=cap, seq=seq, group=group),
        grid=(batch // group,),
        in_specs=[pl.BlockSpec((n_exp, group * seq), lambda i: (0, i))],
        out_specs=[spec, spec, pl.BlockSpec((rows, 128), lambda i: (i, 0))],
        out_shape=[jax.ShapeDtypeStruct((batch * n_exp, seq), jnp.int32),
                   jax.ShapeDtypeStruct((batch * n_exp, seq), F32),
                   jax.ShapeDtypeStruct((batch * n_exp, 128), jnp.int32)],
        compiler_params=_params(1),
        name="route_topk",
    )(logits_t)
    return (slot.reshape(batch, n_exp, seq), gate.reshape(batch, n_exp, seq), before.reshape(batch, n_exp, 128))


def _window_plan(before, cap, seq):
    per = MOE_CHUNK // 128
    lo = before[:, :, 0:seq // 128:per]
    hi = jnp.concatenate([lo[:, :, 1:], jnp.full(lo.shape[:2] + (1,), cap, jnp.int32)], axis=2)
    base = jnp.minimum(lo // 16 * 16, cap - MOE_WINDOW)
    overflow = jnp.any(hi - base > MOE_WINDOW, axis=1)
    return base.transpose(0, 2, 1).reshape(-1), overflow.astype(jnp.int32).reshape(-1)


def _gather_kernel(slot_ref, gate_ref, h_ref, xs_ref, gs_ref, *, n_inner, cap):
    h = h_ref[...]
    seq = h.shape[0]
    row = lax.broadcasted_iota(jnp.int32, (cap, seq), 0)
    for e in range(n_inner):
        onehot = slot_ref[e:e + 1, :] == row
        xs_ref[e] = _dot(jnp.where(onehot, 1.0, 0.0).astype(BF16), h).astype(BF16)
        gs_ref[e] = jnp.sum(jnp.where(onehot, gate_ref[e:e + 1, :], 0.0), axis=1, keepdims=True)


def _gather(slot, gate, h, batch, seq, cap, n_inner):
    n_exp = slot.shape[1]
    d = h.shape[1]
    n_outer = n_exp // n_inner
    slot4 = slot.reshape(batch, n_outer, n_inner, seq)
    gate4 = gate.reshape(batch, n_outer, n_inner, seq)
    sspec = pl.BlockSpec((None, None, n_inner, seq), lambda b, e: (b, e, 0, 0))
    return pl.pallas_call(
        functools.partial(_gather_kernel, n_inner=n_inner, cap=cap),
        grid=(batch, n_outer),
        in_specs=[sspec, sspec, pl.BlockSpec((seq, d), lambda b, e: (b, 0))],
        out_specs=[pl.BlockSpec((n_inner, cap, d), lambda b, e: (e, b, 0)),
                   pl.BlockSpec((n_inner, cap, 1), lambda b, e: (e, b, 0))],
        out_shape=[jax.ShapeDtypeStruct((n_exp, batch * cap, d), BF16),
                   jax.ShapeDtypeStruct((n_exp, batch * cap, 1), F32)],
        compiler_params=_params(2),
        name="moe_gather",
    )(slot4, gate4, h)


def _gather_win_kernel(base_ref, flag_ref, slot_ref, gate_ref, h_ref, xs_ref, gs_ref, *, cap):
    k = pl.program_id(1)
    step = pl.program_id(0) * pl.num_programs(1) + k
    n_exp, chunk = slot_ref.shape
    win = MOE_WINDOW

    @pl.when(k == 0)
    def _():
        xs_ref[...] = jnp.zeros(xs_ref.shape, xs_ref.dtype)
        gs_ref[...] = jnp.zeros(gs_ref.shape, gs_ref.dtype)

    h = h_ref[...]

    @pl.when(flag_ref[step] == 0)
    def _():
        row = lax.broadcasted_iota(jnp.int32, (win, chunk), 0)
        bases = [pl.multiple_of(base_ref[step * n_exp + e], 16) for e in range(n_exp)]
        hots = [(slot_ref[e:e + 1, :] - bases[e]) == row for e in range(n_exp)]
        stacked = jnp.concatenate([jnp.where(hot, 1.0, 0.0).astype(BF16) for hot in hots], axis=0)
        picked = _dot(stacked, h)
        for e in range(n_exp):
            rows = pl.ds(bases[e], win)
            xs_ref[e, rows, :] += picked[e * win:(e + 1) * win].astype(BF16)
            gs_ref[e, rows, :] += jnp.sum(jnp.where(hots[e], gate_ref[e:e + 1, :], 0.0), axis=1, keepdims=True)

    @pl.when(flag_ref[step] != 0)
    def _():
        row = lax.broadcasted_iota(jnp.int32, (cap, chunk), 0)
        for e in range(n_exp):
            hot = slot_ref[e:e + 1, :] == row
            xs_ref[e] += _dot(jnp.where(hot, 1.0, 0.0).astype(BF16), h).astype(BF16)
            gs_ref[e] += jnp.sum(jnp.where(hot, gate_ref[e:e + 1, :], 0.0), axis=1, keepdims=True)


def _gather_windowed(slot, gate, h, base, flag, batch, seq, cap):
    n_exp = slot.shape[1]
    d = h.shape[1]
    nk = seq // MOE_CHUNK
    sspec = pl.BlockSpec((None, n_exp, MOE_CHUNK), lambda b, k, *_: (b, 0, k))
    return pl.pallas_call(
        functools.partial(_gather_win_kernel, cap=cap),
        grid_spec=pltpu.PrefetchScalarGridSpec(
            num_scalar_prefetch=2,
            grid=(batch, nk),
            in_specs=[sspec, sspec, pl.BlockSpec((MOE_CHUNK, d), lambda b, k, *_: (b * nk + k, 0))],
            out_specs=[pl.BlockSpec((n_exp, cap, d), lambda b, k, *_: (0, b, 0)),
                       pl.BlockSpec((n_exp, cap, 1), lambda b, k, *_: (0, b, 0))]),
        out_shape=[jax.ShapeDtypeStruct((n_exp, batch * cap, d), BF16),
                   jax.ShapeDtypeStruct((n_exp, batch * cap, 1), F32)],
        compiler_params=_params(2),
        name="moe_gather_windowed",
    )(base, flag, slot, gate, h)


def _ffn_kernel(xc_ref, gc_ref, xl_ref, gl_ref, wg_hbm, wu_hbm, wd_hbm, y_ref, wg_s, wu_s, wd_s, stg_g, stg_u,
                stg_d, sems, *, layer, n_chunks, ctx_steps):
    e = pl.program_id(0)
    m = pl.program_id(1)
    slot = e % 2
    rows_gu = wg_s.shape[1] // n_chunks
    rows_d = wd_s.shape[1] // n_chunks

    def chunk_copies(expert, c):
        gu = pl.ds(pl.multiple_of(c * rows_gu, rows_gu), rows_gu)
        dn = pl.ds(pl.multiple_of(c * rows_d, rows_d), rows_d)
        return (pltpu.make_async_copy(wg_hbm.at[layer, expert, gu, :], stg_g, sems.at[0]),
                pltpu.make_async_copy(wu_hbm.at[layer, expert, gu, :], stg_u, sems.at[1]),
                pltpu.make_async_copy(wd_hbm.at[layer, expert, dn, :], stg_d, sems.at[2]))

    def land(dst_slot, c, copies):
        for cp in copies:
            cp.wait()
        gu = pl.ds(pl.multiple_of(c * rows_gu, rows_gu), rows_gu)
        dn = pl.ds(pl.multiple_of(c * rows_d, rows_d), rows_d)
        wg_s[dst_slot, gu, :] = stg_g[...].astype(BF16)
        wu_s[dst_slot, gu, :] = stg_u[...].astype(BF16)
        wd_s[dst_slot, dn, :] = stg_d[...].astype(BF16)

    @pl.when((e == 0) & (m == 0))
    def _():
        for c in range(n_chunks):
            copies = chunk_copies(0, c)
            for cp in copies:
                cp.start()
            land(0, c, copies)

    prefetch = (m < n_chunks) & (e + 1 < pl.num_programs(0))

    @pl.when(prefetch)
    def _():
        for cp in chunk_copies(e + 1, m):
            cp.start()

    def swiglu(x_ref, g_ref):
        x = x_ref[...]
        a = _dot(x, wg_s[slot])
        u = _dot(x, wu_s[slot])
        y = _dot((_silu(a) * u).astype(BF16), wd_s[slot])
        y_ref[...] = (y * g_ref[...]).astype(y_ref.dtype)

    @pl.when(m < ctx_steps)
    def _():
        swiglu(xc_ref, gc_ref)

    @pl.when(m >= ctx_steps)
    def _():
        swiglu(xl_ref, gl_ref)

    @pl.when(prefetch)
    def _():
        land(1 - slot, m, chunk_copies(e + 1, m))


def _expert_ffn(xs_c, gs_c, xs_l, gs_l, wg, wu, wd, layer):
    n_exp, rows_c, d = xs_c.shape
    rows_l = xs_l.shape[1]
    ff = wg.shape[3]
    tm = math.gcd(math.gcd(rows_c, rows_l), 512)
    ctx_steps = rows_c // tm
    steps = ctx_steps + rows_l // tm
    n_chunks = 4 if steps >= 4 else (2 if steps >= 2 else 1)
    any_spec = pl.BlockSpec(memory_space=pl.ANY)

    def ctx_spec(width):
        return pl.BlockSpec((None, tm, width), lambda e, i: (e, jnp.minimum(i, ctx_steps - 1), 0))

    def lat_spec(width):
        return pl.BlockSpec((None, tm, width), lambda e, i: (e, jnp.maximum(i - ctx_steps, 0), 0))

    return pl.pallas_call(
        functools.partial(_ffn_kernel, layer=layer, n_chunks=n_chunks, ctx_steps=ctx_steps),
        grid=(n_exp, steps),
        in_specs=[ctx_spec(d), ctx_spec(1), lat_spec(d), lat_spec(1), any_spec, any_spec, any_spec],
        out_specs=pl.BlockSpec((None, tm, d), lambda e, i: (e, i, 0)),
        out_shape=jax.ShapeDtypeStruct((n_exp, rows_c + rows_l, d), BF16),
        scratch_shapes=[pltpu.VMEM((2, d, ff), BF16), pltpu.VMEM((2, d, ff), BF16), pltpu.VMEM((2, ff, d), BF16),
                        pltpu.VMEM((d // n_chunks, ff), F32), pltpu.VMEM((d // n_chunks, ff), F32),
                        pltpu.VMEM((ff // n_chunks, d), F32), pltpu.SemaphoreType.DMA((3,))],
        compiler_params=_params(2),
        name="moe_ffn",
    )(xs_c, gs_c, xs_l, gs_l, wg, wu, wd)


def _combine_kernel(x_ref, y_ref, slot_ref, g2_ref, lg_ref, lb_ref, o_ref, *, alpha, cap):
    n_exp = y_ref.shape[0]
    tt = x_ref.shape[0]
    slot = slot_ref[...]
    lanes = 128
    if cap < lanes and lanes % cap == 0 and n_exp % (lanes // cap) == 0:
        per = lanes // cap
        lane = lax.broadcasted_iota(jnp.int32, (tt, lanes), 1)
        hots = []
        for g in range(n_exp // per):
            hit = None
            for j in range(per):
                s = slot[:, g * per + j:g * per + j + 1]
                match = jnp.where(s >= 0, s + j * cap, -1) == lane
                hit = match if hit is None else hit | match
            hots.append(jnp.where(hit, 1.0, 0.0).astype(BF16))
        acc = _dot(jnp.concatenate(hots, axis=1), y_ref[...].reshape(n_exp * cap, y_ref.shape[2]))
    else:
        col = lax.broadcasted_iota(jnp.int32, (tt, cap), 1)
        acc = jnp.zeros(x_ref.shape, F32)
        for e in range(n_exp):
            onehot = jnp.where(slot[:, e:e + 1] == col, 1.0, 0.0).astype(BF16)
            acc += _dot(onehot, y_ref[e])
    o_ref[...] = _ln(alpha * x_ref[...] + g2_ref[...] * acc) * lg_ref[...] + lb_ref[...]


def _combine(x1, y, slot_t, mod3, ln_g, ln_b, batch, seq, cap, row_blk0, per_request, alpha):
    rows, d = x1.shape
    n_exp = y.shape[0]
    tt = min(256, seq)
    nt = seq // tt
    vec = pl.BlockSpec((1, d), lambda b, i: (0, 0))
    return pl.pallas_call(
        functools.partial(_combine_kernel, alpha=alpha, cap=cap),
        grid=(batch, nt),
        in_specs=[pl.BlockSpec((tt, d), lambda b, i: (b * nt + i, 0)),
                  pl.BlockSpec((n_exp, cap, d), lambda b, i: (0, row_blk0 + b, 0)),
                  pl.BlockSpec((None, tt, n_exp), lambda b, i: (b, i, 0)),
                  pl.BlockSpec((None, 1, d), lambda b, i: (_mod_row(b, 1, per_request) * 6 + 5, 0, 0)),
                  vec, vec],
        out_specs=pl.BlockSpec((tt, d), lambda b, i: (b * nt + i, 0)),
        out_shape=jax.ShapeDtypeStruct((rows, d), F32),
        compiler_params=_params(2),
        name="moe_combine_postnorm",
    )(x1, y, slot_t, mod3, ln_g, ln_b)


def _combine_win_kernel(base_ref, flag_ref, x_ref, y_ref, slot_ref, g2_ref, lg_ref, lb_ref, o_ref, acc_ref,
                        *, alpha, cap):
    step = pl.program_id(0) * pl.num_programs(1) + pl.program_id(1)
    n_exp = y_ref.shape[0]
    tt = x_ref.shape[0]
    win = MOE_WINDOW
    slot = slot_ref[...]

    @pl.when(flag_ref[step] == 0)
    def _():
        lane = lax.broadcasted_iota(jnp.int32, (tt, 2 * win), 1)
        hots, ys = [], []
        for pair in range(n_exp // 2):
            picks = []
            for half in range(2):
                e = 2 * pair + half
                base = pl.multiple_of(base_ref[step * n_exp + e], 16)
                rel = slot[:, e:e + 1] - base
                picks.append(jnp.where((rel >= 0) & (rel < win), rel + half * win, -1))
                ys.append(y_ref[e, pl.ds(base, win), :])
            hots.append(jnp.where((picks[0] == lane) | (picks[1] == lane), 1.0, 0.0).astype(BF16))
        acc_ref[...] = _dot(jnp.concatenate(hots, axis=1), jnp.concatenate(ys, axis=0))

    @pl.when(flag_ref[step] != 0)
    def _():
        col = lax.broadcasted_iota(jnp.int32, (tt, cap), 1)
        acc = jnp.zeros(x_ref.shape, F32)
        for e in range(n_exp):
            acc += _dot(jnp.where(slot[:, e:e + 1] == col, 1.0, 0.0).astype(BF16), y_ref[e])
        acc_ref[...] = acc

    o_ref[...] = _ln(alpha * x_ref[...] + g2_ref[...] * acc_ref[...]) * lg_ref[...] + lb_ref[...]


def _combine_windowed(x1, y, slot_t, base, flag, mod3, ln_g, ln_b, batch, seq, cap, row_blk0, alpha):
    rows, d = x1.shape
    n_exp = y.shape[0]
    tt = MOE_CHUNK
    nt = seq // tt
    assert 2 * MOE_WINDOW == 128 and n_exp % 2 == 0
    vec = pl.BlockSpec((1, d), lambda b, i, *_: (0, 0))
    return pl.pallas_call(
        functools.partial(_combine_win_kernel, alpha=alpha, cap=cap),
        grid_spec=pltpu.PrefetchScalarGridSpec(
            num_scalar_prefetch=2,
            grid=(batch, nt),
            in_specs=[pl.BlockSpec((tt, d), lambda b, i, *_: (b * nt + i, 0)),
                      pl.BlockSpec((n_exp, cap, d), lambda b, i, *_: (0, row_blk0 + b, 0)),
                      pl.BlockSpec((None, tt, n_exp), lambda b, i, *_: (b, i, 0)),
                      pl.BlockSpec((None, 1, d), lambda b, i, *_: (_mod_row(b, 1, True) * 6 + 5, 0, 0)),
                      vec, vec],
            out_specs=pl.BlockSpec((tt, d), lambda b, i, *_: (b * nt + i, 0)),
            scratch_shapes=[pltpu.VMEM((tt, d), F32)]),
        out_shape=jax.ShapeDtypeStruct((rows, d), F32),
        compiler_params=_params(2),
        name="moe_combine_windowed_postnorm",
    )(base, flag, x1, y, slot_t, mod3, ln_g, ln_b)


def kernel(x_prompt, x_sample, cache_na_k, cache_na_v, cache_gqa_k, cache_gqa_v, state_ret_fwd, state_ret_bwd,
           c, c_ctx, w_in, w_out, w_mod, b_mod, ln1_g, ln1_b, ln2_g, ln2_b, q_norm_g, k_norm_g, na_rpb,
           ret_decay_fwd, ret_decay_bwd, w_router, w_gate, w_up, w_down):
    batch, seq, d = x_prompt.shape
    dbatch, dseq, _ = x_sample.shape
    depth = w_in.shape[0]
    na_heads = cache_na_k.shape[2]
    kv_heads = cache_gqa_k.shape[2]
    ret_heads = state_ret_fwd.shape[2]
    n_exp = w_router.shape[2]
    hd = HEAD_DIM
    gqa_heads = (w_in.shape[2] // hd - 3 * na_heads - 2 * kv_heads - 4 * ret_heads)
    group = gqa_heads // kv_heads
    alpha = float((2 * depth) ** 0.25)
    c_na = 0
    c_gq = 3 * na_heads
    c_gk = c_gq + gqa_heads
    c_gv = c_gk + kv_heads
    c_rt = c_gv + kv_heads
    cap_c = EC_CAPACITY_FACTOR * seq // n_exp
    cap_l = EC_CAPACITY_FACTOR * dseq // n_exp
    assert dbatch + 1 <= MOD_ROWS and (batch * cap_c) % cap_l == 0
    lat_blk0 = batch * cap_c // cap_l

    cond = jnp.zeros((MOD_ROWS, d), F32).at[0].set(c_ctx).at[1:1 + dbatch].set(c)
    mod_all = _modulation(cond, w_mod, b_mod)
    tables = _rope_tables(dseq)
    bias = _na_bias(na_rpb, dseq // GRID_W)

    w_in_b, w_out_b = w_in.astype(BF16), w_out.astype(BF16)

    xp = x_prompt.reshape(batch * seq, d)
    xs = x_sample.reshape(dbatch * dseq, d)
    outs = [[] for _ in range(6)]
    for l in range(depth):
        mod3 = mod_all[l].reshape(MOD_ROWS * 6, 1, d)
        wr_t = jnp.concatenate(_split_bf16(w_router[l]), axis=1)
        gains = (q_norm_g[l].reshape(1, hd), k_norm_g[l].reshape(1, hd))
        l1g, l1b = ln1_g[l].reshape(1, d), ln1_b[l].reshape(1, d)
        l2g, l2b = ln2_g[l].reshape(1, d), ln2_b[l].reshape(1, d)

        zc = _in_projection(xp, mod3, w_in_b, l, seq, False, F32)
        (na_c,) = _ctx_attention(zc, batch, seq, na_heads, 1, c_na, c_na + na_heads, c_na + 2 * na_heads)
        gq_c, gk_n = _ctx_attention(zc, batch, seq, kv_heads, group, c_gq, c_gk, c_gv, gains)
        rt_c, s_f, s_b = _retention(zc, batch, seq, ret_heads, c_rt, ret_decay_fwd[l], ret_decay_bwd[l],
                                    state_out=True)

        def heads_of(col, n):
            return zc[:, col * hd:(col + n) * hd].reshape(batch, seq, n, hd).transpose(0, 2, 1, 3)

        outs[0].append(heads_of(c_na + na_heads, na_heads))
        outs[1].append(heads_of(c_na + 2 * na_heads, na_heads))
        outs[2].append(gk_n)
        outs[3].append(heads_of(c_gv, kv_heads))
        outs[4].append(s_f)
        outs[5].append(s_b)

        zl = _in_projection(xs, mod3, w_in_b, l, dseq, True, BF16)
        (na_l,) = [_na_attention(zl, dbatch, dseq, na_heads, cache_na_k, cache_na_v, l, bias)]
        gq_l = _gqa_attention(zl, dbatch, dseq, kv_heads, group, c_gq, c_gk, c_gv, cache_gqa_k, cache_gqa_v, l,
                              gains, tables)
        (rt_l,) = _retention(zl, dbatch, dseq, ret_heads, c_rt, ret_decay_fwd[l], ret_decay_bwd[l],
                             states=((state_ret_fwd, state_ret_bwd), l))

        xp1, hp, lt_c = _out_projection(xp, na_c, gq_c, rt_c, w_out_b, l, mod3, l1g, l1b, wr_t, seq, False, alpha)
        xs1, hs, lt_l = _out_projection(xs, na_l, gq_l, rt_l, w_out_b, l, mod3, l1g, l1b, wr_t, dseq, True, alpha)

        slot_c, gate_c, _ = _route(lt_c.T, batch, seq, cap_c)
        slot_l, gate_l, before_l = _route(lt_l.T, dbatch, dseq, cap_l)
        windowed = cap_l > MOE_WINDOW and dseq % MOE_CHUNK == 0
        if windowed:
            base_l, flag_l = _window_plan(before_l, cap_l, dseq)
            rows_l, gates_l = _gather_windowed(slot_l, gate_l, hs, base_l, flag_l, dbatch, dseq, cap_l)
        else:
            rows_l, gates_l = _gather(slot_l, gate_l, hs, dbatch, dseq, cap_l, 2)
        rows_c, gates_c = _gather(slot_c, gate_c, hp, batch, seq, cap_c, n_exp)
        y = _expert_ffn(rows_c, gates_c, rows_l, gates_l, w_gate, w_up, w_down, l)
        xp = _combine(xp1, y, slot_c.transpose(0, 2, 1), mod3, l2g, l2b, batch, seq, cap_c, 0, False, alpha)
        if windowed:
            xs = _combine_windowed(xs1, y, slot_l.transpose(0, 2, 1), base_l, flag_l, mod3, l2g, l2b, dbatch, dseq,
                                   cap_l, lat_blk0, alpha)
        else:
            xs = _combine(xs1, y, slot_l.transpose(0, 2, 1), mod3, l2g, l2b, dbatch, dseq, cap_l, lat_blk0, True,
                          alpha)

    stacked = [jnp.stack(o, axis=1) for o in outs]
    return (xp.reshape(batch, seq, d), xs.reshape(dbatch, dseq, d), *stacked)
```

```python
import functools
import math

import numpy as np
import jax
import jax.numpy as jnp
from jax import lax
from jax.experimental import pallas as pl
from jax.experimental.pallas import tpu as pltpu

F32 = jnp.float32
BF16 = jnp.bfloat16

HEAD_DIM = 128
GRID_W = 64
NA_WIN_R = 8
NA_WIN_C = 16
ROPE_THETA = 10000.0
EC_CAPACITY_FACTOR = 2
LN_EPS = 1e-5
RMS_EPS = 1e-6
NEG_INF = -1e30
ATTN_SCALE = HEAD_DIM ** -0.5
LOG2E = 1.4426950408889634
NA_BLOCK_ROWS = 4
ROUTE_GROUP = 16
MOE_CHUNK = 256
MOE_WINDOW = 64
MOD_ROWS = 16
VMEM_LIMIT = 56 * 1024 * 1024

_NT = (((1,), (1,)), ((), ()))
_NN = (((1,), (0,)), ((), ()))


def _params(n_grid, vmem=VMEM_LIMIT):
    return pltpu.CompilerParams(dimension_semantics=("arbitrary",) * n_grid, vmem_limit_bytes=vmem)


def _dot(a, b, dims=_NN):
    return lax.dot_general(a, b, dims, preferred_element_type=F32)


def _split_bf16(x):
    hi = x.astype(BF16)
    lo = (x - hi.astype(F32)).astype(BF16)
    return hi, lo


def _dot3(a, b, dims=_NN):
    ah, al = _split_bf16(a)
    bh, bl = _split_bf16(b)
    return _dot(ah, bh, dims) + _dot(al, bh, dims) + _dot(ah, bl, dims)


def _silu(x):
    return x / (1.0 + jnp.exp(-x))


def _ln(x):
    mu = jnp.mean(x, axis=-1, keepdims=True)
    xc = x - mu
    var = jnp.mean(xc * xc, axis=-1, keepdims=True)
    return xc * lax.rsqrt(var + LN_EPS)


def _rms(x, g):
    return x * lax.rsqrt(jnp.mean(x * x, axis=-1, keepdims=True) + RMS_EPS) * g


def _rope(x, cos, sin_signed):
    lane = lax.broadcasted_iota(jnp.int32, x.shape, 1) & (HEAD_DIM // 2 - 1)
    partner = jnp.where(lane < HEAD_DIM // 4,
                        pltpu.roll(x, HEAD_DIM - HEAD_DIM // 4, 1),
                        pltpu.roll(x, HEAD_DIM // 4, 1))
    return x * cos + partner * sin_signed


def _softmax_pv(scores, values):
    m = functools.reduce(jnp.maximum, [jnp.max(s, axis=-1, keepdims=True) for s in scores])
    ps = [jnp.exp(s - m) for s in scores]
    denom = functools.reduce(lambda a, b: a + b, [jnp.sum(p, axis=-1, keepdims=True) for p in ps])
    o = functools.reduce(lambda a, b: a + b, [_dot(p.astype(BF16), v) for p, v in zip(ps, values)])
    return o / denom


def _mod_kernel(c_ref, w_ref, b_ref, o_ref):
    a = _silu(c_ref[...])
    o_ref[...] = _dot3(a, w_ref[...]) + b_ref[...]


def _modulation(cond, w_mod, b_mod):
    depth, d, n = w_mod.shape
    tn = 768
    return pl.pallas_call(
        _mod_kernel,
        grid=(depth, n // tn),
        in_specs=[pl.BlockSpec((MOD_ROWS, d), lambda l, j: (0, 0)),
                  pl.BlockSpec((None, d, tn), lambda l, j: (l, 0, j)),
                  pl.BlockSpec((None, 1, tn), lambda l, j: (l, 0, j))],
        out_specs=pl.BlockSpec((None, MOD_ROWS, tn), lambda l, j: (l, 0, j)),
        out_shape=jax.ShapeDtypeStruct((depth, MOD_ROWS, n), F32),
        compiler_params=_params(2),
        name="adaln_mod",
    )(cond, w_mod, b_mod.reshape(depth, 1, n))


def _inproj_kernel(x_ref, sh_ref, sc_ref, w_ref, z_ref, *, sub, tn):
    for r in range(x_ref.shape[0] // sub):
        rs = slice(r * sub, (r + 1) * sub)
        h = (_ln(x_ref[rs, :]) * (1.0 + sc_ref[...]) + sh_ref[...]).astype(BF16)
        for c in range(w_ref.shape[1] // tn):
            cs = slice(c * tn, (c + 1) * tn)
            z_ref[rs, cs] = _dot(h, w_ref[:, cs]).astype(z_ref.dtype)


def _mod_row(block, blocks_per_request, per_request):
    return 1 + block // blocks_per_request if per_request else 0


def _in_projection(x, mod3, w_bf16, layer, tokens_per_batch, per_request, out_dtype):
    rows, d = x.shape
    n = w_bf16.shape[2]
    tm = min(512 if out_dtype == BF16 else 256, tokens_per_batch)
    per = tokens_per_batch // tm

    def mod_spec(chunk):
        return pl.BlockSpec((None, 1, d), lambda i: (_mod_row(i, per, per_request) * 6 + chunk, 0, 0))

    return pl.pallas_call(
        functools.partial(_inproj_kernel, sub=min(256, tm), tn=n // 2),
        grid=(rows // tm,),
        in_specs=[pl.BlockSpec((tm, d), lambda i: (i, 0)),
                  mod_spec(0), mod_spec(1),
                  pl.BlockSpec((None, d, n), lambda i: (layer, 0, 0), pipeline_mode=pl.Buffered(1))],
        out_specs=pl.BlockSpec((tm, n), lambda i: (i, 0)),
        out_shape=jax.ShapeDtypeStruct((rows, n), out_dtype),
        compiler_params=_params(1),
        name="ln_inproj",
    )(x, mod3, mod3, w_bf16)


def _ctx_attn_kernel(*refs, group, normed):
    if normed:
        q_ref, k_ref, v_ref, qg_ref, kg_ref, o_ref, kn_ref = refs
    else:
        q_ref, k_ref, v_ref, o_ref = refs
    for hh in range(k_ref.shape[1] // HEAD_DIM):
        k = k_ref[:, hh * HEAD_DIM:(hh + 1) * HEAD_DIM]
        if normed:
            k = _rms(k, kg_ref[...])
            kn_ref[...] = k
        kb = k.astype(BF16)
        vb = v_ref[:, hh * HEAD_DIM:(hh + 1) * HEAD_DIM].astype(BF16)
        for g in range(group):
            cols = slice((hh * group + g) * HEAD_DIM, (hh * group + g + 1) * HEAD_DIM)
            q = q_ref[:, cols]
            if normed:
                q = _rms(q, qg_ref[...])
            s = _dot(q.astype(BF16), kb, _NT) * ATTN_SCALE
            o_ref[:, cols] = _softmax_pv([s], [vb]).astype(o_ref.dtype)


def _ctx_attention(z, batch, seq, heads, group, q_col, k_col, v_col, gains=None):
    normed = gains is not None
    hps = 1 if normed else heads
    assert q_col % (group * hps) == 0 and k_col % hps == 0 and v_col % hps == 0
    qw = group * hps * HEAD_DIM
    kw = hps * HEAD_DIM
    in_specs = [pl.BlockSpec((seq, qw), lambda b, h: (b, q_col // (group * hps) + h)),
                pl.BlockSpec((seq, kw), lambda b, h: (b, k_col // hps + h)),
                pl.BlockSpec((seq, kw), lambda b, h: (b, v_col // hps + h))]
    args = [z, z, z]
    out_specs = [pl.BlockSpec((seq, qw), lambda b, h: (b, h))]
    out_shape = [jax.ShapeDtypeStruct((batch * seq, heads * group * HEAD_DIM), BF16)]
    if normed:
        in_specs += [pl.BlockSpec((1, HEAD_DIM), lambda b, h: (0, 0))] * 2
        args += list(gains)
        out_specs.append(pl.BlockSpec((None, None, seq, HEAD_DIM), lambda b, h: (b, h, 0, 0)))
        out_shape.append(jax.ShapeDtypeStruct((batch, heads, seq, HEAD_DIM), F32))
    return pl.pallas_call(
        functools.partial(_ctx_attn_kernel, group=group, normed=normed),
        grid=(batch, heads // hps),
        in_specs=in_specs, out_specs=out_specs, out_shape=out_shape,
        compiler_params=_params(2),
        name="ctx_gqa" if normed else "ctx_na",
    )(*args)


def _log_sigmoid(x):
    return -jnp.log1p(jnp.exp(-x))


def _retention_kernel(*refs, seq, tq, with_state_in, with_state_out):
    q_ref, k_ref, v_ref, g_ref, df_ref, db_ref = refs[:6]
    refs = refs[6:]
    if with_state_in:
        sf_ref, sb_ref = refs[:2]
        refs = refs[2:]
    o_ref = refs[0]
    decay_ref = refs[-1]
    t0 = pl.program_id(1) * tq
    for hh in range(decay_ref.shape[0]):
        hs = slice(hh * HEAD_DIM, (hh + 1) * HEAD_DIM)
        lgf = _log_sigmoid(df_ref[hh])
        lgb = _log_sigmoid(db_ref[hh])

        @pl.when(pl.program_id(2) == 0)
        def _():
            i = t0 + lax.broadcasted_iota(jnp.int32, (tq, seq), 0)
            j = lax.broadcasted_iota(jnp.int32, (tq, seq), 1)
            dist = (i - j).astype(F32)
            decay = jnp.where(dist == 0.0, 2.0, jnp.exp(jnp.where(dist > 0.0, lgf * dist, -lgb * dist)))
            decay_ref[hh] = decay * ATTN_SCALE

        q = q_ref[:, hs]
        kb = k_ref[:, hs].astype(BF16)
        vb = v_ref[:, hs].astype(BF16)
        att = _dot(q.astype(BF16), kb, _NT) * decay_ref[hh]
        o = _dot(att.astype(BF16), vb)
        if with_state_in:
            t = (t0 + lax.broadcasted_iota(jnp.int32, (tq, 1), 0)).astype(F32)
            qf = q.astype(F32)
            o += _dot((qf * jnp.exp(lgf * (t + 1.0))).astype(BF16), sf_ref[hh].astype(BF16))
            o += _dot((qf * jnp.exp(lgb * (seq - t))).astype(BF16), sb_ref[hh].astype(BF16))
        o_ref[:, hs] = (_ln(o) * _silu(g_ref[:, hs].astype(F32))).astype(o_ref.dtype)
        if with_state_out:
            nsf_ref, nsb_ref = refs[1:3]
            tj = lax.broadcasted_iota(jnp.int32, (seq, 1), 0).astype(F32)
            kf = k_ref[:, hs].astype(F32) * ATTN_SCALE
            kdf = (kf * jnp.exp(lgf * (seq - 1.0 - tj))).T.astype(BF16)
            kdb = (kf * jnp.exp(lgb * tj)).T.astype(BF16)
            nsf_ref[hh] = _dot(kdf, vb)
            nsb_ref[hh] = _dot(kdb, vb)


def _retention(z, batch, seq, heads, col0, dec_f, dec_b, states=None, state_out=False):
    tq = min(512, seq)
    nq = seq // tq
    assert not (state_out and nq != 1)
    hps = heads if nq == 1 else 1
    assert col0 % hps == 0
    w = hps * HEAD_DIM
    c0 = col0 // hps
    nh = heads // hps
    in_specs = [pl.BlockSpec((tq, w), lambda h, i, b: (b * nq + i, c0 + h)),
                pl.BlockSpec((seq, w), lambda h, i, b: (b, c0 + nh + h)),
                pl.BlockSpec((seq, w), lambda h, i, b: (b, c0 + 2 * nh + h)),
                pl.BlockSpec((tq, w), lambda h, i, b: (b * nq + i, c0 + 3 * nh + h)),
                pl.BlockSpec((hps, 1, 1), lambda h, i, b: (h, 0, 0)),
                pl.BlockSpec((hps, 1, 1), lambda h, i, b: (h, 0, 0))]
    args = [z, z, z, z, dec_f.reshape(heads, 1, 1), dec_b.reshape(heads, 1, 1)]
    if states is not None:
        (sf, sb), layer = states
        spec = pl.BlockSpec((None, None, hps, HEAD_DIM, HEAD_DIM), lambda h, i, b: (b, layer, h, 0, 0))
        in_specs += [spec, spec]
        args += [sf, sb]
    out_specs = [pl.BlockSpec((tq, w), lambda h, i, b: (b * nq + i, h))]
    out_shape = [jax.ShapeDtypeStruct((batch * seq, heads * HEAD_DIM), BF16)]
    if state_out:
        spec = pl.BlockSpec((None, hps, HEAD_DIM, HEAD_DIM), lambda h, i, b: (b, h, 0, 0))
        out_specs += [spec, spec]
        out_shape += [jax.ShapeDtypeStruct((batch, heads, HEAD_DIM, HEAD_DIM), F32)] * 2
    return pl.pallas_call(
        functools.partial(_retention_kernel, seq=seq, tq=tq, with_state_in=states is not None,
                          with_state_out=state_out),
        grid=(nh, nq, batch),
        in_specs=in_specs, out_specs=out_specs, out_shape=out_shape,
        scratch_shapes=[pltpu.VMEM((hps, tq, seq), F32)],
        compiler_params=_params(3),
        name="retention",
    )(*args)


def _with_ones(v):
    return jnp.concatenate([v, jnp.ones(v.shape, v.dtype)], axis=1)


def _exp2_pv(scores, values_with_ones):
    m = functools.reduce(jnp.maximum, [jnp.max(s, axis=-1, keepdims=True) for s in scores])
    r = functools.reduce(lambda a, b: a + b,
                         [_dot(jnp.exp2(s - m).astype(BF16), v) for s, v in zip(scores, values_with_ones)])
    return r[:, :HEAD_DIM] / r[:, HEAD_DIM:HEAD_DIM + 1]


def _na_plan(rows):
    kr = min(NA_WIN_R, rows)
    nq = min(NA_BLOCK_ROWS, rows)
    nk = min(nq + kr - 1, rows)
    assert rows % nq == 0
    blocks, cases = [], []
    for r0 in range(0, rows, nq):
        start = int(np.clip(r0 - kr // 2, 0, rows - nk))
        win = [int(np.clip(r - kr // 2, 0, rows - kr)) for r in range(r0, r0 + nq)]
        assert all(start <= w and w + kr <= start + nk for w in win)
        key = (start - r0,) + tuple(w - start for w in win)
        if key not in cases:
            cases.append(key)
        blocks.append((r0, start, cases.index(key)))
    return kr, nq, nk, blocks, cases


def _na_kernel(q_ref, k_ref, v_ref, ck_ref, cv_ref, bias_ref, o_ref, vx_ref, cvx_ref, *, blocks, nq, nk):
    vx_ref[...] = _with_ones(v_ref[...])
    cvx_ref[...] = _with_ones(cv_ref[...].astype(BF16))
    ck = ck_ref[...].astype(BF16)
    c = ATTN_SCALE * LOG2E
    for r0, start, case in blocks:
        qs = slice(r0 * GRID_W, (r0 + nq) * GRID_W)
        ks = slice(start * GRID_W, (start + nk) * GRID_W)
        q = q_ref[qs, :]
        s_loc = _dot(q, k_ref[ks, :], _NT) * c + bias_ref[case]
        s_ctx = _dot(q, ck, _NT) * c
        o_ref[qs, :] = _exp2_pv([s_loc, s_ctx], [vx_ref[ks, :], cvx_ref[...]]).astype(o_ref.dtype)


def _na_bias(rpb, rows):
    kr, nq, nk, _, cases = _na_plan(rows)
    dr = np.zeros((len(cases), nq, nk), np.int32)
    row_ok = np.zeros((len(cases), nq, nk), bool)
    for c, key in enumerate(cases):
        rel, offs = key[0], key[1:]
        for ri in range(nq):
            for ju in range(nk):
                row_ok[c, ri, ju] = 0 <= ju - offs[ri] < kr
                dr[c, ri, ju] = np.clip(rel + ju - ri + NA_WIN_R - 1, 0, 2 * NA_WIN_R - 2)
    cq = np.arange(GRID_W)
    ck = np.arange(GRID_W)
    col_start = np.clip(cq - NA_WIN_C // 2, 0, GRID_W - NA_WIN_C)
    col_ok = (ck[None, :] >= col_start[:, None]) & (ck[None, :] < col_start[:, None] + NA_WIN_C)
    dc = np.clip(ck[None, :] - cq[:, None] + (NA_WIN_C - 1), 0, 2 * NA_WIN_C - 2)
    pick_col = (dc.reshape(-1)[None, :] == np.arange(2 * NA_WIN_C - 1)[:, None]).astype(np.float32)
    n_l, n_h = rpb.shape[:2]
    t = jnp.take(rpb.astype(F32), dr.reshape(-1), axis=2)
    t = jnp.einsum("lhxb,bn->lhxn", t, pick_col, precision=lax.Precision.HIGHEST)
    t = t.reshape(n_l, n_h, len(cases), nq, nk, GRID_W, GRID_W).transpose(0, 1, 2, 3, 5, 4, 6)
    ok = row_ok[:, :, None, :, None] & col_ok[None, None, :, None, :]
    t = jnp.where(ok, t * LOG2E, NEG_INF)
    return t.reshape(n_l, n_h, len(cases), nq * GRID_W, nk * GRID_W)


def _na_attention(z, batch, seq, heads, cache_k, cache_v, layer, bias):
    _, nq, nk, blocks, cases = _na_plan(seq // GRID_W)
    past = cache_k.shape[3]
    cspec = pl.BlockSpec((None, None, None, past, HEAD_DIM), lambda b, h: (b, layer, h, 0, 0))
    return pl.pallas_call(
        functools.partial(_na_kernel, blocks=blocks, nq=nq, nk=nk),
        grid=(batch, heads),
        in_specs=[pl.BlockSpec((seq, HEAD_DIM), lambda b, h: (b, h)),
                  pl.BlockSpec((seq, HEAD_DIM), lambda b, h: (b, heads + h)),
                  pl.BlockSpec((seq, HEAD_DIM), lambda b, h: (b, 2 * heads + h)),
                  cspec, cspec,
                  pl.BlockSpec((None, None, len(cases), nq * GRID_W, nk * GRID_W),
                               lambda b, h: (layer, h, 0, 0, 0))],
        out_specs=pl.BlockSpec((seq, HEAD_DIM), lambda b, h: (b, h)),
        out_shape=jax.ShapeDtypeStruct((batch * seq, heads * HEAD_DIM), BF16),
        scratch_shapes=[pltpu.VMEM((seq, 2 * HEAD_DIM), BF16), pltpu.VMEM((past, 2 * HEAD_DIM), BF16)],
        compiler_params=_params(2),
        name="latent_na",
    )(z, z, z, cache_k, cache_v, bias)


def _gqa_kernel(q_ref, k_ref, v_ref, ck_ref, cv_ref, qg_ref, kg_ref, cos_ref, sin_ref, o_ref, kf_ref, vf_ref,
                *, group, past, seq, tq):
    i = pl.program_id(2)

    @pl.when(i == 0)
    def _():
        kf_ref[0:past, :] = ck_ref[...].astype(BF16)
        vf_ref[0:past, :] = _with_ones(cv_ref[...].astype(BF16))
        k = _rope(_rms(k_ref[...].astype(F32), kg_ref[...]), cos_ref[...], sin_ref[...])
        kf_ref[past:past + seq, :] = k.astype(BF16)
        vf_ref[past:past + seq, :] = _with_ones(v_ref[...])

    t0 = pl.multiple_of(i * tq, tq)
    cos = cos_ref[pl.ds(t0, tq), :]
    sin = sin_ref[pl.ds(t0, tq), :]
    kf = kf_ref[...]
    vf = vf_ref[...]
    def scores(g):
        q = q_ref[:, g * HEAD_DIM:(g + 1) * HEAD_DIM].astype(F32)
        q = _rope(_rms(q, qg_ref[...]), cos, sin) * (ATTN_SCALE * LOG2E)
        return _dot(q.astype(BF16), kf, _NT)

    s = scores(0)
    for g in range(group):
        s_next = scores(g + 1) if g + 1 < group else None
        o_ref[:, g * HEAD_DIM:(g + 1) * HEAD_DIM] = _exp2_pv([s], [vf]).astype(o_ref.dtype)
        s = s_next


def _rope_tables(seq):
    t = np.arange(seq)
    half = HEAD_DIM // 2
    inv = 1.0 / (ROPE_THETA ** (np.arange(0, half, 2, dtype=np.float32) / half))
    row = (t // GRID_W).astype(np.float32)
    col = (t % GRID_W).astype(np.float32)
    ang = jnp.concatenate([jnp.asarray(row[:, None] * inv)] * 2 + [jnp.asarray(col[:, None] * inv)] * 2, axis=-1)
    sign = np.where((np.arange(HEAD_DIM) % half) < half // 2, -1.0, 1.0).astype(np.float32)
    return jnp.cos(ang), jnp.sin(ang) * sign


def _gqa_attention(z, batch, seq, kv_heads, group, q_col, k_col, v_col, cache_k, cache_v, layer, gains, tables):
    past = cache_k.shape[3]
    tq = min(256, seq)
    nq = seq // tq
    qw = group * HEAD_DIM
    cspec = pl.BlockSpec((None, None, None, past, HEAD_DIM), lambda b, h, i: (b, layer, h, 0, 0))
    gspec = pl.BlockSpec((1, HEAD_DIM), lambda b, h, i: (0, 0))
    tspec = pl.BlockSpec((seq, HEAD_DIM), lambda b, h, i: (0, 0))
    return pl.pallas_call(
        functools.partial(_gqa_kernel, group=group, past=past, seq=seq, tq=tq),
        grid=(batch, kv_heads, nq),
        in_specs=[pl.BlockSpec((tq, qw), lambda b, h, i: (b * nq + i, q_col // group + h)),
                  pl.BlockSpec((seq, HEAD_DIM), lambda b, h, i: (b, k_col + h)),
                  pl.BlockSpec((seq, HEAD_DIM), lambda b, h, i: (b, v_col + h)),
                  cspec, cspec, gspec, gspec, tspec, tspec],
        out_specs=pl.BlockSpec((tq, qw), lambda b, h, i: (b * nq + i, h)),
        out_shape=jax.ShapeDtypeStruct((batch * seq, kv_heads * qw), BF16),
        scratch_shapes=[pltpu.VMEM((past + seq, HEAD_DIM), BF16), pltpu.VMEM((past + seq, 2 * HEAD_DIM), BF16)],
        compiler_params=_params(3),
        name="latent_gqa",
    )(z, z, z, cache_k, cache_v, gains[0], gains[1], tables[0], tables[1])


def _outproj_kernel(x_ref, na_ref, gq_ref, rt_ref, w_ref, g1_ref, sh2_ref, sc2_ref, lg_ref, lb_ref, wr_ref,
                    x1_ref, h2_ref, lt_ref, *, alpha, sub):
    n_exp = lt_ref.shape[1]
    for r in range(x_ref.shape[0] // sub):
        rs = slice(r * sub, (r + 1) * sub)
        mix = jnp.concatenate([na_ref[rs, :], gq_ref[rs, :], rt_ref[rs, :]], axis=1)
        y = _dot(mix, w_ref[...])
        x1 = _ln(alpha * x_ref[rs, :] + g1_ref[...] * y) * lg_ref[...] + lb_ref[...]
        x1_ref[rs, :] = x1
        h2 = _ln(x1) * (1.0 + sc2_ref[...]) + sh2_ref[...]
        hi, lo = _split_bf16(h2)
        h2_ref[rs, :] = hi
        both = _dot(hi, wr_ref[...])
        lt_ref[rs, :] = both[:, :n_exp] + both[:, n_exp:] + _dot(lo, wr_ref[:, 0:n_exp])


def _out_projection(x, na_o, gq_o, rt_o, w_bf16, layer, mod3, ln_g, ln_b, wr_hl, tokens_per_batch, per_request,
                    alpha):
    rows, d = x.shape
    tm = 512
    per = tokens_per_batch // tm if per_request else 1
    n_exp = wr_hl.shape[1] // 2

    def mod_spec(chunk):
        return pl.BlockSpec((None, 1, d), lambda i: (_mod_row(i, per, per_request) * 6 + chunk, 0, 0))

    def row_spec(width):
        return pl.BlockSpec((tm, width), lambda i: (i, 0))

    vec = pl.BlockSpec((1, d), lambda i: (0, 0))
    return pl.pallas_call(
        functools.partial(_outproj_kernel, alpha=alpha, sub=256),
        grid=(rows // tm,),
        in_specs=[row_spec(d), row_spec(na_o.shape[1]), row_spec(gq_o.shape[1]), row_spec(rt_o.shape[1]),
                  pl.BlockSpec((None,) + w_bf16.shape[1:], lambda i: (layer, 0, 0)),
                  mod_spec(2), mod_spec(3), mod_spec(4), vec, vec,
                  pl.BlockSpec(wr_hl.shape, lambda i: (0, 0))],
        out_specs=[row_spec(d), row_spec(d), row_spec(n_exp)],
        out_shape=[jax.ShapeDtypeStruct((rows, d), F32), jax.ShapeDtypeStruct((rows, d), BF16),
                   jax.ShapeDtypeStruct((rows, n_exp), F32)],
        compiler_params=_params(1),
        name="outproj_postnorm_router",
    )(x, na_o, gq_o, rt_o, w_bf16, mod3, mod3, mod3, ln_g, ln_b, wr_hl)


def _topk_kernel(lt_ref, slot_ref, gate_ref, before_ref, *, cap, seq, group):
    affs = []
    for g in range(group):
        logits = lt_ref[:, g * seq:(g + 1) * seq]
        m = jnp.max(logits, axis=0, keepdims=True)
        ex = jnp.exp(logits - m)
        affs.append(ex / jnp.sum(ex, axis=0, keepdims=True))
    aff = jnp.concatenate(affs, axis=0)
    n_exp = aff.shape[0]
    bits = lax.bitcast_convert_type(aff, jnp.int32)

    def count(mask):
        return jnp.sum(jnp.where(mask, 1.0, 0.0), axis=1, keepdims=True)

    def value_step(it, thr):
        cand = thr | jnp.left_shift(jnp.int32(1), 30 - it)
        return jnp.where(count(bits >= cand) >= cap, cand, thr)

    thr = lax.fori_loop(0, 31, value_step, jnp.zeros((n_exp, 1), jnp.int32))
    above = bits > thr
    tied = bits == thr
    need = cap - count(above)
    tok = lax.broadcasted_iota(jnp.int32, (n_exp, seq), 1)
    n_bits = int(seq - 1).bit_length()

    def index_step(it, bound):
        cand = bound | jnp.left_shift(jnp.int32(1), n_bits - 1 - it)
        return jnp.where(count(tied & (tok < cand)) < need, cand, bound)

    bound = lax.fori_loop(0, n_bits, index_step, jnp.zeros((n_exp, 1), jnp.int32))
    sel = above | (tied & (tok <= bound))
    self = jnp.where(sel, 1.0, 0.0)
    lanes = 128
    upper = jnp.where(lax.broadcasted_iota(jnp.int32, (lanes, lanes), 0)
                      < lax.broadcasted_iota(jnp.int32, (lanes, lanes), 1), 1.0, 0.0).astype(BF16)
    running = jnp.zeros((n_exp, 1), F32)
    lane = lax.broadcasted_iota(jnp.int32, (n_exp, lanes), 1)
    before = jnp.zeros((n_exp, lanes), F32)
    for blk in range(seq // lanes):
        sl = slice(blk * lanes, (blk + 1) * lanes)
        chunk = self[:, sl]
        before = jnp.where(lane == blk, running, before)
        pos = _dot(chunk.astype(BF16), upper) + running
        slot_ref[:, sl] = jnp.where(chunk > 0.0, pos.astype(jnp.int32), -1)
        running = running + jnp.sum(chunk, axis=1, keepdims=True)
    gate_ref[...] = aff
    before_ref[...] = before.astype(jnp.int32)


def _route(logits_t, batch, seq, cap):
    n_exp = logits_t.shape[0]
    assert seq // 128 <= 128
    group = math.gcd(batch, ROUTE_GROUP)
    rows = group * n_exp
    spec = pl.BlockSpec((rows, seq), lambda i: (i, 0))
    slot, gate, before = pl.pallas_call(
        functools.partial(_topk_kernel, cap=cap, seq=seq, group=group),
        grid=(batch // group,),
        in_specs=[pl.BlockSpec((n_exp, group * seq), lambda i: (0, i))],
        out_specs=[spec, spec, pl.BlockSpec((rows, 128), lambda i: (i, 0))],
        out_shape=[jax.ShapeDtypeStruct((batch * n_exp, seq), jnp.int32),
                   jax.ShapeDtypeStruct((batch * n_exp, seq), F32),
                   jax.ShapeDtypeStruct((batch * n_exp, 128), jnp.int32)],
        compiler_params=_params(1),
        name="route_topk",
    )(logits_t)
    return (slot.reshape(batch, n_exp, seq), gate.reshape(batch, n_exp, seq), before.reshape(batch, n_exp, 128))


def _window_plan(before, cap, seq):
    per = MOE_CHUNK // 128
    lo = before[:, :, 0:seq // 128:per]
    hi = jnp.concatenate([lo[:, :, 1:], jnp.full(lo.shape[:2] + (1,), cap, jnp.int32)], axis=2)
    base = jnp.minimum(lo // 16 * 16, cap - MOE_WINDOW)
    overflow = jnp.any(hi - base > MOE_WINDOW, axis=1)
    return base.transpose(0, 2, 1).reshape(-1), overflow.astype(jnp.int32).reshape(-1)


def _gather_kernel(slot_ref, gate_ref, h_ref, xs_ref, gs_ref, *, n_inner, cap):
    h = h_ref[...]
    seq = h.shape[0]
    row = lax.broadcasted_iota(jnp.int32, (cap, seq), 0)
    for e in range(n_inner):
        onehot = slot_ref[e:e + 1, :] == row
        xs_ref[e] = _dot(jnp.where(onehot, 1.0, 0.0).astype(BF16), h).astype(BF16)
        gs_ref[e] = jnp.sum(jnp.where(onehot, gate_ref[e:e + 1, :], 0.0), axis=1, keepdims=True)


def _gather(slot, gate, h, batch, seq, cap, n_inner):
    n_exp = slot.shape[1]
    d = h.shape[1]
    n_outer = n_exp // n_inner
    slot4 = slot.reshape(batch, n_outer, n_inner, seq)
    gate4 = gate.reshape(batch, n_outer, n_inner, seq)
    sspec = pl.BlockSpec((None, None, n_inner, seq), lambda b, e: (b, e, 0, 0))
    return pl.pallas_call(
        functools.partial(_gather_kernel, n_inner=n_inner, cap=cap),
        grid=(batch, n_outer),
        in_specs=[sspec, sspec, pl.BlockSpec((seq, d), lambda b, e: (b, 0))],
        out_specs=[pl.BlockSpec((n_inner, cap, d), lambda b, e: (e, b, 0)),
                   pl.BlockSpec((n_inner, cap, 1), lambda b, e: (e, b, 0))],
        out_shape=[jax.ShapeDtypeStruct((n_exp, batch * cap, d), BF16),
                   jax.ShapeDtypeStruct((n_exp, batch * cap, 1), F32)],
        compiler_params=_params(2),
        name="moe_gather",
    )(slot4, gate4, h)


def _gather_win_kernel(base_ref, flag_ref, slot_ref, gate_ref, h_ref, xs_ref, gs_ref, *, cap):
    k = pl.program_id(1)
    step = pl.program_id(0) * pl.num_programs(1) + k
    n_exp, chunk = slot_ref.shape
    win = MOE_WINDOW

    @pl.when(k == 0)
    def _():
        xs_ref[...] = jnp.zeros(xs_ref.shape, xs_ref.dtype)
        gs_ref[...] = jnp.zeros(gs_ref.shape, gs_ref.dtype)

    h = h_ref[...]

    @pl.when(flag_ref[step] == 0)
    def _():
        row = lax.broadcasted_iota(jnp.int32, (win, chunk), 0)
        bases = [pl.multiple_of(base_ref[step * n_exp + e], 16) for e in range(n_exp)]
        hots = [(slot_ref[e:e + 1, :] - bases[e]) == row for e in range(n_exp)]
        stacked = jnp.concatenate([jnp.where(hot, 1.0, 0.0).astype(BF16) for hot in hots], axis=0)
        picked = _dot(stacked, h)
        for e in range(n_exp):
            rows = pl.ds(bases[e], win)
            xs_ref[e, rows, :] += picked[e * win:(e + 1) * win].astype(BF16)
            gs_ref[e, rows, :] += jnp.sum(jnp.where(hots[e], gate_ref[e:e + 1, :], 0.0), axis=1, keepdims=True)

    @pl.when(flag_ref[step] != 0)
    def _():
        row = lax.broadcasted_iota(jnp.int32, (cap, chunk), 0)
        for e in range(n_exp):
            hot = slot_ref[e:e + 1, :] == row
            xs_ref[e] += _dot(jnp.where(hot, 1.0, 0.0).astype(BF16), h).astype(BF16)
            gs_ref[e] += jnp.sum(jnp.where(hot, gate_ref[e:e + 1, :], 0.0), axis=1, keepdims=True)


def _gather_windowed(slot, gate, h, base, flag, batch, seq, cap):
    n_exp = slot.shape[1]
    d = h.shape[1]
    nk = seq // MOE_CHUNK
    sspec = pl.BlockSpec((None, n_exp, MOE_CHUNK), lambda b, k, *_: (b, 0, k))
    return pl.pallas_call(
        functools.partial(_gather_win_kernel, cap=cap),
        grid_spec=pltpu.PrefetchScalarGridSpec(
            num_scalar_prefetch=2,
            grid=(batch, nk),
            in_specs=[sspec, sspec, pl.BlockSpec((MOE_CHUNK, d), lambda b, k, *_: (b * nk + k, 0))],
            out_specs=[pl.BlockSpec((n_exp, cap, d), lambda b, k, *_: (0, b, 0)),
                       pl.BlockSpec((n_exp, cap, 1), lambda b, k, *_: (0, b, 0))]),
        out_shape=[jax.ShapeDtypeStruct((n_exp, batch * cap, d), BF16),
                   jax.ShapeDtypeStruct((n_exp, batch * cap, 1), F32)],
        compiler_params=_params(2),
        name="moe_gather_windowed",
    )(base, flag, slot, gate, h)


def _ffn_kernel(xc_ref, gc_ref, xl_ref, gl_ref, wg_hbm, wu_hbm, wd_hbm, y_ref, wg_s, wu_s, wd_s, stg_g, stg_u,
                stg_d, sems, *, layer, n_chunks, ctx_steps):
    e = pl.program_id(0)
    m = pl.program_id(1)
    slot = e % 2
    rows_gu = wg_s.shape[1] // n_chunks
    rows_d = wd_s.shape[1] // n_chunks

    def chunk_copies(expert, c):
        gu = pl.ds(pl.multiple_of(c * rows_gu, rows_gu), rows_gu)
        dn = pl.ds(pl.multiple_of(c * rows_d, rows_d), rows_d)
        return (pltpu.make_async_copy(wg_hbm.at[layer, expert, gu, :], stg_g, sems.at[0]),
                pltpu.make_async_copy(wu_hbm.at[layer, expert, gu, :], stg_u, sems.at[1]),
                pltpu.make_async_copy(wd_hbm.at[layer, expert, dn, :], stg_d, sems.at[2]))

    def land(dst_slot, c, copies):
        for cp in copies:
            cp.wait()
        gu = pl.ds(pl.multiple_of(c * rows_gu, rows_gu), rows_gu)
        dn = pl.ds(pl.multiple_of(c * rows_d, rows_d), rows_d)
        wg_s[dst_slot, gu, :] = stg_g[...].astype(BF16)
        wu_s[dst_slot, gu, :] = stg_u[...].astype(BF16)
        wd_s[dst_slot, dn, :] = stg_d[...].astype(BF16)

    @pl.when((e == 0) & (m == 0))
    def _():
        for c in range(n_chunks):
            copies = chunk_copies(0, c)
            for cp in copies:
                cp.start()
            land(0, c, copies)

    prefetch = (m < n_chunks) & (e + 1 < pl.num_programs(0))

    @pl.when(prefetch)
    def _():
        for cp in chunk_copies(e + 1, m):
            cp.start()

    def swiglu(x_ref, g_ref):
        x = x_ref[...]
        a = _dot(x, wg_s[slot])
        u = _dot(x, wu_s[slot])
        y = _dot((_silu(a) * u).astype(BF16), wd_s[slot])
        y_ref[...] = (y * g_ref[...]).astype(y_ref.dtype)

    @pl.when(m < ctx_steps)
    def _():
        swiglu(xc_ref, gc_ref)

    @pl.when(m >= ctx_steps)
    def _():
        swiglu(xl_ref, gl_ref)

    @pl.when(prefetch)
    def _():
        land(1 - slot, m, chunk_copies(e + 1, m))


def _expert_ffn(xs_c, gs_c, xs_l, gs_l, wg, wu, wd, layer):
    n_exp, rows_c, d = xs_c.shape
    rows_l = xs_l.shape[1]
    ff = wg.shape[3]
    tm = math.gcd(math.gcd(rows_c, rows_l), 512)
    ctx_steps = rows_c // tm
    steps = ctx_steps + rows_l // tm
    n_chunks = 4 if steps >= 4 else (2 if steps >= 2 else 1)
    any_spec = pl.BlockSpec(memory_space=pl.ANY)

    def ctx_spec(width):
        return pl.BlockSpec((None, tm, width), lambda e, i: (e, jnp.minimum(i, ctx_steps - 1), 0))

    def lat_spec(width):
        return pl.BlockSpec((None, tm, width), lambda e, i: (e, jnp.maximum(i - ctx_steps, 0), 0))

    return pl.pallas_call(
        functools.partial(_ffn_kernel, layer=layer, n_chunks=n_chunks, ctx_steps=ctx_steps),
        grid=(n_exp, steps),
        in_specs=[ctx_spec(d), ctx_spec(1), lat_spec(d), lat_spec(1), any_spec, any_spec, any_spec],
        out_specs=pl.BlockSpec((None, tm, d), lambda e, i: (e, i, 0)),
        out_shape=jax.ShapeDtypeStruct((n_exp, rows_c + rows_l, d), BF16),
        scratch_shapes=[pltpu.VMEM((2, d, ff), BF16), pltpu.VMEM((2, d, ff), BF16), pltpu.VMEM((2, ff, d), BF16),
                        pltpu.VMEM((d // n_chunks, ff), F32), pltpu.VMEM((d // n_chunks, ff), F32),
                        pltpu.VMEM((ff // n_chunks, d), F32), pltpu.SemaphoreType.DMA((3,))],
        compiler_params=_params(2),
        name="moe_ffn",
    )(xs_c, gs_c, xs_l, gs_l, wg, wu, wd)


def _combine_kernel(x_ref, y_ref, slot_ref, g2_ref, lg_ref, lb_ref, o_ref, *, alpha, cap):
    n_exp = y_ref.shape[0]
    tt = x_ref.shape[0]
    slot = slot_ref[...]
    lanes = 128
    if cap < lanes and lanes % cap == 0 and n_exp % (lanes // cap) == 0:
        per = lanes // cap
        lane = lax.broadcasted_iota(jnp.int32, (tt, lanes), 1)
        hots = []
        for g in range(n_exp // per):
            hit = None
            for j in range(per):
                s = slot[:, g * per + j:g * per + j + 1]
                match = jnp.where(s >= 0, s + j * cap, -1) == lane
                hit = match if hit is None else hit | match
            hots.append(jnp.where(hit, 1.0, 0.0).astype(BF16))
        acc = _dot(jnp.concatenate(hots, axis=1), y_ref[...].reshape(n_exp * cap, y_ref.shape[2]))
    else:
        col = lax.broadcasted_iota(jnp.int32, (tt, cap), 1)
        acc = jnp.zeros(x_ref.shape, F32)
        for e in range(n_exp):
            onehot = jnp.where(slot[:, e:e + 1] == col, 1.0, 0.0).astype(BF16)
            acc += _dot(onehot, y_ref[e])
    o_ref[...] = _ln(alpha * x_ref[...] + g2_ref[...] * acc) * lg_ref[...] + lb_ref[...]


def _combine(x1, y, slot_t, mod3, ln_g, ln_b, batch, seq, cap, row_blk0, per_request, alpha):
    rows, d = x1.shape
    n_exp = y.shape[0]
    tt = min(256, seq)
    nt = seq // tt
    vec = pl.BlockSpec((1, d), lambda b, i: (0, 0))
    return pl.pallas_call(
        functools.partial(_combine_kernel, alpha=alpha, cap=cap),
        grid=(batch, nt),
        in_specs=[pl.BlockSpec((tt, d), lambda b, i: (b * nt + i, 0)),
                  pl.BlockSpec((n_exp, cap, d), lambda b, i: (0, row_blk0 + b, 0)),
                  pl.BlockSpec((None, tt, n_exp), lambda b, i: (b, i, 0)),
                  pl.BlockSpec((None, 1, d), lambda b, i: (_mod_row(b, 1, per_request) * 6 + 5, 0, 0)),
                  vec, vec],
        out_specs=pl.BlockSpec((tt, d), lambda b, i: (b * nt + i, 0)),
        out_shape=jax.ShapeDtypeStruct((rows, d), F32),
        compiler_params=_params(2),
        name="moe_combine_postnorm",
    )(x1, y, slot_t, mod3, ln_g, ln_b)


def _combine_win_kernel(base_ref, flag_ref, x_ref, y_ref, slot_ref, g2_ref, lg_ref, lb_ref, o_ref, acc_ref,
                        *, alpha, cap):
    step = pl.program_id(0) * pl.num_programs(1) + pl.program_id(1)
    n_exp = y_ref.shape[0]
    tt = x_ref.shape[0]
    win = MOE_WINDOW
    slot = slot_ref[...]

    @pl.when(flag_ref[step] == 0)
    def _():
        lane = lax.broadcasted_iota(jnp.int32, (tt, 2 * win), 1)
        hots, ys = [], []
        for pair in range(n_exp // 2):
            picks = []
            for half in range(2):
                e = 2 * pair + half
                base = pl.multiple_of(base_ref[step * n_exp + e], 16)
                rel = slot[:, e:e + 1] - base
                picks.append(jnp.where((rel >= 0) & (rel < win), rel + half * win, -1))
                ys.append(y_ref[e, pl.ds(base, win), :])
            hots.append(jnp.where((picks[0] == lane) | (picks[1] == lane), 1.0, 0.0).astype(BF16))
        acc_ref[...] = _dot(jnp.concatenate(hots, axis=1), jnp.concatenate(ys, axis=0))

    @pl.when(flag_ref[step] != 0)
    def _():
        col = lax.broadcasted_iota(jnp.int32, (tt, cap), 1)
        acc = jnp.zeros(x_ref.shape, F32)
        for e in range(n_exp):
            acc += _dot(jnp.where(slot[:, e:e + 1] == col, 1.0, 0.0).astype(BF16), y_ref[e])
        acc_ref[...] = acc

    o_ref[...] = _ln(alpha * x_ref[...] + g2_ref[...] * acc_ref[...]) * lg_ref[...] + lb_ref[...]


def _combine_windowed(x1, y, slot_t, base, flag, mod3, ln_g, ln_b, batch, seq, cap, row_blk0, alpha):
    rows, d = x1.shape
    n_exp = y.shape[0]
    tt = MOE_CHUNK
    nt = seq // tt
    assert 2 * MOE_WINDOW == 128 and n_exp % 2 == 0
    vec = pl.BlockSpec((1, d), lambda b, i, *_: (0, 0))
    return pl.pallas_call(
        functools.partial(_combine_win_kernel, alpha=alpha, cap=cap),
        grid_spec=pltpu.PrefetchScalarGridSpec(
            num_scalar_prefetch=2,
            grid=(batch, nt),
            in_specs=[pl.BlockSpec((tt, d), lambda b, i, *_: (b * nt + i, 0)),
                      pl.BlockSpec((n_exp, cap, d), lambda b, i, *_: (0, row_blk0 + b, 0)),
                      pl.BlockSpec((None, tt, n_exp), lambda b, i, *_: (b, i, 0)),
                      pl.BlockSpec((None, 1, d), lambda b, i, *_: (_mod_row(b, 1, True) * 6 + 5, 0, 0)),
                      vec, vec],
            out_specs=pl.BlockSpec((tt, d), lambda b, i, *_: (b * nt + i, 0)),
            scratch_shapes=[pltpu.VMEM((tt, d), F32)]),
        out_shape=jax.ShapeDtypeStruct((rows, d), F32),
        compiler_params=_params(2),
        name="moe_combine_windowed_postnorm",
    )(base, flag, x1, y, slot_t, mod3, ln_g, ln_b)


def kernel(x_prompt, x_sample, cache_na_k, cache_na_v, cache_gqa_k, cache_gqa_v, state_ret_fwd, state_ret_bwd,
           c, c_ctx, w_in, w_out, w_mod, b_mod, ln1_g, ln1_b, ln2_g, ln2_b, q_norm_g, k_norm_g, na_rpb,
           ret_decay_fwd, ret_decay_bwd, w_router, w_gate, w_up, w_down):
    batch, seq, d = x_prompt.shape
    dbatch, dseq, _ = x_sample.shape
    depth = w_in.shape[0]
    na_heads = cache_na_k.shape[2]
    kv_heads = cache_gqa_k.shape[2]
    ret_heads = state_ret_fwd.shape[2]
    n_exp = w_router.shape[2]
    hd = HEAD_DIM
    gqa_heads = (w_in.shape[2] // hd - 3 * na_heads - 2 * kv_heads - 4 * ret_heads)
    group = gqa_heads // kv_heads
    alpha = float((2 * depth) ** 0.25)
    c_na = 0
    c_gq = 3 * na_heads
    c_gk = c_gq + gqa_heads
    c_gv = c_gk + kv_heads
    c_rt = c_gv + kv_heads
    cap_c = EC_CAPACITY_FACTOR * seq // n_exp
    cap_l = EC_CAPACITY_FACTOR * dseq // n_exp
    assert dbatch + 1 <= MOD_ROWS and (batch * cap_c) % cap_l == 0
    lat_blk0 = batch * cap_c // cap_l

    cond = jnp.zeros((MOD_ROWS, d), F32).at[0].set(c_ctx).at[1:1 + dbatch].set(c)
    mod_all = _modulation(cond, w_mod, b_mod)
    tables = _rope_tables(dseq)
    bias = _na_bias(na_rpb, dseq // GRID_W)

    w_in_b, w_out_b = w_in.astype(BF16), w_out.astype(BF16)

    xp = x_prompt.reshape(batch * seq, d)
    xs = x_sample.reshape(dbatch * dseq, d)
    outs = [[] for _ in range(6)]
    for l in range(depth):
        mod3 = mod_all[l].reshape(MOD_ROWS * 6, 1, d)
        wr_t = jnp.concatenate(_split_bf16(w_router[l]), axis=1)
        gains = (q_norm_g[l].reshape(1, hd), k_norm_g[l].reshape(1, hd))
        l1g, l1b = ln1_g[l].reshape(1, d), ln1_b[l].reshape(1, d)
        l2g, l2b = ln2_g[l].reshape(1, d), ln2_b[l].reshape(1, d)

        zc = _in_projection(xp, mod3, w_in_b, l, seq, False, F32)
        (na_c,) = _ctx_attention(zc, batch, seq, na_heads, 1, c_na, c_na + na_heads, c_na + 2 * na_heads)
        gq_c, gk_n = _ctx_attention(zc, batch, seq, kv_heads, group, c_gq, c_gk, c_gv, gains)
        rt_c, s_f, s_b = _retention(zc, batch, seq, ret_heads, c_rt, ret_decay_fwd[l], ret_decay_bwd[l],
                                    state_out=True)

        def heads_of(col, n):
            return zc[:, col * hd:(col + n) * hd].reshape(batch, seq, n, hd).transpose(0, 2, 1, 3)

        outs[0].append(heads_of(c_na + na_heads, na_heads))
        outs[1].append(heads_of(c_na + 2 * na_heads, na_heads))
        outs[2].append(gk_n)
        outs[3].append(heads_of(c_gv, kv_heads))
        outs[4].append(s_f)
        outs[5].append(s_b)

        zl = _in_projection(xs, mod3, w_in_b, l, dseq, True, BF16)
        (na_l,) = [_na_attention(zl, dbatch, dseq, na_heads, cache_na_k, cache_na_v, l, bias)]
        gq_l = _gqa_attention(zl, dbatch, dseq, kv_heads, group, c_gq, c_gk, c_gv, cache_gqa_k, cache_gqa_v, l,
                              gains, tables)
        (rt_l,) = _retention(zl, dbatch, dseq, ret_heads, c_rt, ret_decay_fwd[l], ret_decay_bwd[l],
                             states=((state_ret_fwd, state_ret_bwd), l))

        xp1, hp, lt_c = _out_projection(xp, na_c, gq_c, rt_c, w_out_b, l, mod3, l1g, l1b, wr_t, seq, False, alpha)
        xs1, hs, lt_l = _out_projection(xs, na_l, gq_l, rt_l, w_out_b, l, mod3, l1g, l1b, wr_t, dseq, True, alpha)

        slot_c, gate_c, _ = _route(lt_c.T, batch, seq, cap_c)
        slot_l, gate_l, before_l = _route(lt_l.T, dbatch, dseq, cap_l)
        windowed = cap_l > MOE_WINDOW and dseq % MOE_CHUNK == 0
        if windowed:
            base_l, flag_l = _window_plan(before_l, cap_l, dseq)
            rows_l, gates_l = _gather_windowed(slot_l, gate_l, hs, base_l, flag_l, dbatch, dseq, cap_l)
        else:
            rows_l, gates_l = _gather(slot_l, gate_l, hs, dbatch, dseq, cap_l, 2)
        rows_c, gates_c = _gather(slot_c, gate_c, hp, batch, seq, cap_c, n_exp)
        y = _expert_ffn(rows_c, gates_c, rows_l, gates_l, w_gate, w_up, w_down, l)
        xp = _combine(xp1, y, slot_c.transpose(0, 2, 1), mod3, l2g, l2b, batch, seq, cap_c, 0, False, alpha)
        if windowed:
            xs = _combine_windowed(xs1, y, slot_l.transpose(0, 2, 1), base_l, flag_l, mod3, l2g, l2b, dbatch, dseq,
                                   cap_l, lat_blk0, alpha)
        else:
            xs = _combine(xs1, y, slot_l.transpose(0, 2, 1), mod3, l2g, l2b, dbatch, dseq, cap_l, lat_blk0, True,
                          alpha)

    stacked = [jnp.stack(o, axis=1) for o in outs]
    return (xp.reshape(batch, seq, d), xs.reshape(dbatch, dseq, d), *stacked)
```

```python
import functools
import math

import numpy as np
import jax
import jax.numpy as jnp
from jax import lax
from jax.experimental import pallas as pl
from jax.experimental.pallas import tpu as pltpu

F32 = jnp.float32
BF16 = jnp.bfloat16

HEAD_DIM = 128
GRID_W = 64
NA_WIN_R = 8
NA_WIN_C = 16
ROPE_THETA = 10000.0
EC_CAPACITY_FACTOR = 2
LN_EPS = 1e-5
RMS_EPS = 1e-6
NEG_INF = -1e30
ATTN_SCALE = HEAD_DIM ** -0.5
LOG2E = 1.4426950408889634
NA_BLOCK_ROWS = 4
ROUTE_GROUP = 16
MOE_CHUNK = 256
MOE_WINDOW = 64
MOD_ROWS = 16
VMEM_LIMIT = 56 * 1024 * 1024

_NT = (((1,), (1,)), ((), ()))
_NN = (((1,), (0,)), ((), ()))


def _params(n_grid, vmem=VMEM_LIMIT):
    return pltpu.CompilerParams(dimension_semantics=("arbitrary",) * n_grid, vmem_limit_bytes=vmem)


def _dot(a, b, dims=_NN):
    return lax.dot_general(a, b, dims, preferred_element_type=F32)


def _split_bf16(x):
    hi = x.astype(BF16)
    lo = (x - hi.astype(F32)).astype(BF16)
    return hi, lo


def _dot3(a, b, dims=_NN):
    ah, al = _split_bf16(a)
    bh, bl = _split_bf16(b)
    return _dot(ah, bh, dims) + _dot(al, bh, dims) + _dot(ah, bl, dims)


def _silu(x):
    return x / (1.0 + jnp.exp(-x))


def _ln(x):
    mu = jnp.mean(x, axis=-1, keepdims=True)
    xc = x - mu
    var = jnp.mean(xc * xc, axis=-1, keepdims=True)
    return xc * lax.rsqrt(var + LN_EPS)


def _rms(x, g):
    return x * lax.rsqrt(jnp.mean(x * x, axis=-1, keepdims=True) + RMS_EPS) * g


def _rope(x, cos, sin_signed):
    lane = lax.broadcasted_iota(jnp.int32, x.shape, 1) & (HEAD_DIM // 2 - 1)
    partner = jnp.where(lane < HEAD_DIM // 4,
                        pltpu.roll(x, HEAD_DIM - HEAD_DIM // 4, 1),
                        pltpu.roll(x, HEAD_DIM // 4, 1))
    return x * cos + partner * sin_signed


def _softmax_pv(scores, values):
    m = functools.reduce(jnp.maximum, [jnp.max(s, axis=-1, keepdims=True) for s in scores])
    ps = [jnp.exp(s - m) for s in scores]
    denom = functools.reduce(lambda a, b: a + b, [jnp.sum(p, axis=-1, keepdims=True) for p in ps])
    o = functools.reduce(lambda a, b: a + b, [_dot(p.astype(BF16), v) for p, v in zip(ps, values)])
    return o / denom


def _mod_kernel(c_ref, w_ref, b_ref, o_ref):
    a = _silu(c_ref[...])
    o_ref[...] = _dot3(a, w_ref[...]) + b_ref[...]


def _modulation(cond, w_mod, b_mod):
    depth, d, n = w_mod.shape
    tn = 768
    return pl.pallas_call(
        _mod_kernel,
        grid=(depth, n // tn),
        in_specs=[pl.BlockSpec((MOD_ROWS, d), lambda l, j: (0, 0)),
                  pl.BlockSpec((None, d, tn), lambda l, j: (l, 0, j)),
                  pl.BlockSpec((None, 1, tn), lambda l, j: (l, 0, j))],
        out_specs=pl.BlockSpec((None, MOD_ROWS, tn), lambda l, j: (l, 0, j)),
        out_shape=jax.ShapeDtypeStruct((depth, MOD_ROWS, n), F32),
        compiler_params=_params(2),
        name="adaln_mod",
    )(cond, w_mod, b_mod.reshape(depth, 1, n))


def _inproj_kernel(x_ref, sh_ref, sc_ref, w_ref, z_ref, *, sub, tn):
    for r in range(x_ref.shape[0] // sub):
        rs = slice(r * sub, (r + 1) * sub)
        h = (_ln(x_ref[rs, :]) * (1.0 + sc_ref[...]) + sh_ref[...]).astype(BF16)
        for c in range(w_ref.shape[1] // tn):
            cs = slice(c * tn, (c + 1) * tn)
            z_ref[rs, cs] = _dot(h, w_ref[:, cs]).astype(z_ref.dtype)


def _mod_row(block, blocks_per_request, per_request):
    return 1 + block // blocks_per_request if per_request else 0


def _in_projection(x, mod3, w_bf16, layer, tokens_per_batch, per_request, out_dtype):
    rows, d = x.shape
    n = w_bf16.shape[2]
    tm = min(512 if out_dtype == BF16 else 256, tokens_per_batch)
    per = tokens_per_batch // tm

    def mod_spec(chunk):
        return pl.BlockSpec((None, 1, d), lambda i: (_mod_row(i, per, per_request) * 6 + chunk, 0, 0))

    return pl.pallas_call(
        functools.partial(_inproj_kernel, sub=min(256, tm), tn=n // 2),
        grid=(rows // tm,),
        in_specs=[pl.BlockSpec((tm, d), lambda i: (i, 0)),
                  mod_spec(0), mod_spec(1),
                  pl.BlockSpec((None, d, n), lambda i: (layer, 0, 0), pipeline_mode=pl.Buffered(1))],
        out_specs=pl.BlockSpec((tm, n), lambda i: (i, 0)),
        out_shape=jax.ShapeDtypeStruct((rows, n), out_dtype),
        compiler_params=_params(1),
        name="ln_inproj",
    )(x, mod3, mod3, w_bf16)


def _ctx_attn_kernel(*refs, group, normed):
    if normed:
        q_ref, k_ref, v_ref, qg_ref, kg_ref, o_ref, kn_ref = refs
    else:
        q_ref, k_ref, v_ref, o_ref = refs
    for hh in range(k_ref.shape[1] // HEAD_DIM):
        k = k_ref[:, hh * HEAD_DIM:(hh + 1) * HEAD_DIM]
        if normed:
            k = _rms(k, kg_ref[...])
            kn_ref[...] = k
        kb = k.astype(BF16)
        vb = v_ref[:, hh * HEAD_DIM:(hh + 1) * HEAD_DIM].astype(BF16)
        for g in range(group):
            cols = slice((hh * group + g) * HEAD_DIM, (hh * group + g + 1) * HEAD_DIM)
            q = q_ref[:, cols]
            if normed:
                q = _rms(q, qg_ref[...])
            s = _dot(q.astype(BF16), kb, _NT) * ATTN_SCALE
            o_ref[:, cols] = _softmax_pv([s], [vb]).astype(o_ref.dtype)


def _ctx_attention(z, batch, seq, heads, group, q_col, k_col, v_col, gains=None):
    normed = gains is not None
    hps = 1 if normed else heads
    assert q_col % (group * hps) == 0 and k_col % hps == 0 and v_col % hps == 0
    qw = group * hps * HEAD_DIM
    kw = hps * HEAD_DIM
    in_specs = [pl.BlockSpec((seq, qw), lambda b, h: (b, q_col // (group * hps) + h)),
                pl.BlockSpec((seq, kw), lambda b, h: (b, k_col // hps + h)),
                pl.BlockSpec((seq, kw), lambda b, h: (b, v_col // hps + h))]
    args = [z, z, z]
    out_specs = [pl.BlockSpec((seq, qw), lambda b, h: (b, h))]
    out_shape = [jax.ShapeDtypeStruct((batch * seq, heads * group * HEAD_DIM), BF16)]
    if normed:
        in_specs += [pl.BlockSpec((1, HEAD_DIM), lambda b, h: (0, 0))] * 2
        args += list(gains)
        out_specs.append(pl.BlockSpec((None, None, seq, HEAD_DIM), lambda b, h: (b, h, 0, 0)))
        out_shape.append(jax.ShapeDtypeStruct((batch, heads, seq, HEAD_DIM), F32))
    return pl.pallas_call(
        functools.partial(_ctx_attn_kernel, group=group, normed=normed),
        grid=(batch, heads // hps),
        in_specs=in_specs, out_specs=out_specs, out_shape=out_shape,
        compiler_params=_params(2),
        name="ctx_gqa" if normed else "ctx_na",
    )(*args)


def _log_sigmoid(x):
    return -jnp.log1p(jnp.exp(-x))


def _retention_kernel(*refs, seq, tq, with_state_in, with_state_out):
    q_ref, k_ref, v_ref, g_ref, df_ref, db_ref = refs[:6]
    refs = refs[6:]
    if with_state_in:
        sf_ref, sb_ref = refs[:2]
        refs = refs[2:]
    o_ref = refs[0]
    decay_ref = refs[-1]
    chunk = tq
    n_chunks = seq // chunk
    il = lax.broadcasted_iota(jnp.int32, (chunk, 1), 0).astype(F32)
    for hh in range(decay_ref.shape[0]):
        hs = slice(hh * HEAD_DIM, (hh + 1) * HEAD_DIM)
        lgf = _log_sigmoid(df_ref[hh])
        lgb = _log_sigmoid(db_ref[hh])

        @pl.when(pl.program_id(1) == 0)
        def _():
            i = lax.broadcasted_iota(jnp.int32, (chunk, chunk), 0)
            j = lax.broadcasted_iota(jnp.int32, (chunk, chunk), 1)
            dist = (i - j).astype(F32)
            decay = jnp.where(dist == 0.0, 2.0, jnp.exp(jnp.where(dist > 0.0, lgf * dist, -lgb * dist)))
            decay_ref[hh] = decay * ATTN_SCALE

        k_dec_f = jnp.exp(lgf * (chunk - 1.0 - il)) * ATTN_SCALE
        k_dec_b = jnp.exp(lgb * il) * ATTN_SCALE
        add_f, add_b = [], []
        for c in range(n_chunks):
            cs = slice(c * chunk, (c + 1) * chunk)
            kf = k_ref[cs, hs].astype(F32)
            vb = v_ref[cs, hs].astype(BF16)
            add_f.append(_dot((kf * k_dec_f).T.astype(BF16), vb))
            add_b.append(_dot((kf * k_dec_b).T.astype(BF16), vb))
        state_f, state_b = [None] * n_chunks, [None] * n_chunks
        carry_f = jnp.exp(lgf * chunk)
        carry_b = jnp.exp(lgb * chunk)
        s = sf_ref[hh] if with_state_in else None
        for c in range(n_chunks):
            state_f[c] = s
            s = add_f[c] if s is None else carry_f * s + add_f[c]
        final_f = s
        s = sb_ref[hh] if with_state_in else None
        for c in reversed(range(n_chunks)):
            state_b[c] = s
            s = add_b[c] if s is None else carry_b * s + add_b[c]
        final_b = s

        q_dec_f = jnp.exp(lgf * (il + 1.0))
        q_dec_b = jnp.exp(lgb * (chunk - il))
        for c in range(n_chunks):
            cs = slice(c * chunk, (c + 1) * chunk)
            q = q_ref[cs, hs]
            att = _dot(q.astype(BF16), k_ref[cs, hs].astype(BF16), _NT) * decay_ref[hh]
            o = _dot(att.astype(BF16), v_ref[cs, hs].astype(BF16))
            qf = q.astype(F32)
            if state_f[c] is not None:
                o += _dot((qf * q_dec_f).astype(BF16), state_f[c].astype(BF16))
            if state_b[c] is not None:
                o += _dot((qf * q_dec_b).astype(BF16), state_b[c].astype(BF16))
            o_ref[cs, hs] = (_ln(o) * _silu(g_ref[cs, hs].astype(F32))).astype(o_ref.dtype)
        if with_state_out:
            nsf_ref, nsb_ref = refs[1:3]
            nsf_ref[hh] = final_f
            nsb_ref[hh] = final_b


def _retention(z, batch, seq, heads, col0, dec_f, dec_b, states=None, state_out=False):
    chunk = min(512, seq)
    hps = heads if seq == chunk else 1
    assert col0 % hps == 0 and seq % chunk == 0
    w = hps * HEAD_DIM
    c0 = col0 // hps
    nh = heads // hps
    in_specs = [pl.BlockSpec((seq, w), lambda h, b: (b, c0 + h)),
                pl.BlockSpec((seq, w), lambda h, b: (b, c0 + nh + h)),
                pl.BlockSpec((seq, w), lambda h, b: (b, c0 + 2 * nh + h)),
                pl.BlockSpec((seq, w), lambda h, b: (b, c0 + 3 * nh + h)),
                pl.BlockSpec((hps, 1, 1), lambda h, b: (h, 0, 0)),
                pl.BlockSpec((hps, 1, 1), lambda h, b: (h, 0, 0))]
    args = [z, z, z, z, dec_f.reshape(heads, 1, 1), dec_b.reshape(heads, 1, 1)]
    if states is not None:
        (sf, sb), layer = states
        spec = pl.BlockSpec((None, None, hps, HEAD_DIM, HEAD_DIM), lambda h, b: (b, layer, h, 0, 0))
        in_specs += [spec, spec]
        args += [sf, sb]
    out_specs = [pl.BlockSpec((seq, w), lambda h, b: (b, h))]
    out_shape = [jax.ShapeDtypeStruct((batch * seq, heads * HEAD_DIM), BF16)]
    if state_out:
        spec = pl.BlockSpec((None, hps, HEAD_DIM, HEAD_DIM), lambda h, b: (b, h, 0, 0))
        out_specs += [spec, spec]
        out_shape += [jax.ShapeDtypeStruct((batch, heads, HEAD_DIM, HEAD_DIM), F32)] * 2
    return pl.pallas_call(
        functools.partial(_retention_kernel, seq=seq, tq=chunk, with_state_in=states is not None,
                          with_state_out=state_out),
        grid=(nh, batch),
        in_specs=in_specs, out_specs=out_specs, out_shape=out_shape,
        scratch_shapes=[pltpu.VMEM((hps, chunk, chunk), F32)],
        compiler_params=_params(2),
        name="retention",
    )(*args)


def _with_ones(v):
    return jnp.concatenate([v, jnp.ones(v.shape, v.dtype)], axis=1)


def _exp2_pv(scores, values_with_ones):
    m = functools.reduce(jnp.maximum, [jnp.max(s, axis=-1, keepdims=True) for s in scores])
    r = functools.reduce(lambda a, b: a + b,
                         [_dot(jnp.exp2(s - m).astype(BF16), v) for s, v in zip(scores, values_with_ones)])
    return r[:, :HEAD_DIM] / r[:, HEAD_DIM:HEAD_DIM + 1]


def _na_plan(rows):
    kr = min(NA_WIN_R, rows)
    nq = min(NA_BLOCK_ROWS, rows)
    nk = min(nq + kr - 1, rows)
    assert rows % nq == 0
    blocks, cases = [], []
    for r0 in range(0, rows, nq):
        start = int(np.clip(r0 - kr // 2, 0, rows - nk))
        win = [int(np.clip(r - kr // 2, 0, rows - kr)) for r in range(r0, r0 + nq)]
        assert all(start <= w and w + kr <= start + nk for w in win)
        key = (start - r0,) + tuple(w - start for w in win)
        if key not in cases:
            cases.append(key)
        blocks.append((r0, start, cases.index(key)))
    return kr, nq, nk, blocks, cases


def _na_kernel(q_ref, k_ref, v_ref, ck_ref, cv_ref, bias_ref, o_ref, vx_ref, cvx_ref, *, blocks, nq, nk):
    vx_ref[...] = _with_ones(v_ref[...])
    cvx_ref[...] = _with_ones(cv_ref[...].astype(BF16))
    ck = ck_ref[...].astype(BF16)
    c = ATTN_SCALE * LOG2E
    for r0, start, case in blocks:
        qs = slice(r0 * GRID_W, (r0 + nq) * GRID_W)
        ks = slice(start * GRID_W, (start + nk) * GRID_W)
        q = q_ref[qs, :]
        s_loc = _dot(q, k_ref[ks, :], _NT) * c + bias_ref[case]
        s_ctx = _dot(q, ck, _NT) * c
        o_ref[qs, :] = _exp2_pv([s_loc, s_ctx], [vx_ref[ks, :], cvx_ref[...]]).astype(o_ref.dtype)


def _na_bias(rpb, rows):
    kr, nq, nk, _, cases = _na_plan(rows)
    dr = np.zeros((len(cases), nq, nk), np.int32)
    row_ok = np.zeros((len(cases), nq, nk), bool)
    for c, key in enumerate(cases):
        rel, offs = key[0], key[1:]
        for ri in range(nq):
            for ju in range(nk):
                row_ok[c, ri, ju] = 0 <= ju - offs[ri] < kr
                dr[c, ri, ju] = np.clip(rel + ju - ri + NA_WIN_R - 1, 0, 2 * NA_WIN_R - 2)
    cq = np.arange(GRID_W)
    ck = np.arange(GRID_W)
    col_start = np.clip(cq - NA_WIN_C // 2, 0, GRID_W - NA_WIN_C)
    col_ok = (ck[None, :] >= col_start[:, None]) & (ck[None, :] < col_start[:, None] + NA_WIN_C)
    n_l, n_h = rpb.shape[:2]
    t = jnp.take(rpb.astype(F32), dr.reshape(-1), axis=2)
    pad_idx = np.clip(np.arange(2 * GRID_W - 1) - (GRID_W - 1) + (NA_WIN_C - 1), 0, 2 * NA_WIN_C - 2)
    t = jnp.take(t, pad_idx, axis=3).reshape(n_l, n_h, len(cases), nq, nk, 2 * GRID_W - 1)
    t = jnp.stack([t[..., GRID_W - 1 - q:2 * GRID_W - 1 - q] for q in range(GRID_W)], axis=4)
    ok = row_ok[:, :, None, :, None] & col_ok[None, None, :, None, :]
    t = jnp.where(ok, t * LOG2E, NEG_INF)
    return t.reshape(n_l, n_h, len(cases), nq * GRID_W, nk * GRID_W)


def _na_attention(z, batch, seq, heads, cache_k, cache_v, layer, bias):
    _, nq, nk, blocks, cases = _na_plan(seq // GRID_W)
    past = cache_k.shape[3]
    cspec = pl.BlockSpec((None, None, None, past, HEAD_DIM), lambda b, h: (b, layer, h, 0, 0))
    return pl.pallas_call(
        functools.partial(_na_kernel, blocks=blocks, nq=nq, nk=nk),
        grid=(batch, heads),
        in_specs=[pl.BlockSpec((seq, HEAD_DIM), lambda b, h: (b, h)),
                  pl.BlockSpec((seq, HEAD_DIM), lambda b, h: (b, heads + h)),
                  pl.BlockSpec((seq, HEAD_DIM), lambda b, h: (b, 2 * heads + h)),
                  cspec, cspec,
                  pl.BlockSpec((None, None, len(cases), nq * GRID_W, nk * GRID_W),
                               lambda b, h: (layer, h, 0, 0, 0))],
        out_specs=pl.BlockSpec((seq, HEAD_DIM), lambda b, h: (b, h)),
        out_shape=jax.ShapeDtypeStruct((batch * seq, heads * HEAD_DIM), BF16),
        scratch_shapes=[pltpu.VMEM((seq, 2 * HEAD_DIM), BF16), pltpu.VMEM((past, 2 * HEAD_DIM), BF16)],
        compiler_params=_params(2),
        name="latent_na",
    )(z, z, z, cache_k, cache_v, bias)


def _gqa_kernel(q_ref, k_ref, v_ref, ck_ref, cv_ref, qg_ref, kg_ref, cos_ref, sin_ref, o_ref, kf_ref, vf_ref,
                *, group, past, seq, tq):
    i = pl.program_id(2)

    @pl.when(i == 0)
    def _():
        kf_ref[0:past, :] = ck_ref[...].astype(BF16)
        vf_ref[0:past, :] = _with_ones(cv_ref[...].astype(BF16))
        k = _rope(_rms(k_ref[...].astype(F32), kg_ref[...]), cos_ref[...], sin_ref[...])
        kf_ref[past:past + seq, :] = k.astype(BF16)
        vf_ref[past:past + seq, :] = _with_ones(v_ref[...])

    t0 = pl.multiple_of(i * tq, tq)
    cos = cos_ref[pl.ds(t0, tq), :]
    sin = sin_ref[pl.ds(t0, tq), :]
    kf = kf_ref[...]
    vf = vf_ref[...]
    def scores(g):
        q = q_ref[:, g * HEAD_DIM:(g + 1) * HEAD_DIM].astype(F32)
        q = _rope(_rms(q, qg_ref[...]), cos, sin) * (ATTN_SCALE * LOG2E)
        return _dot(q.astype(BF16), kf, _NT)

    s = scores(0)
    for g in range(group):
        s_next = scores(g + 1) if g + 1 < group else None
        o_ref[:, g * HEAD_DIM:(g + 1) * HEAD_DIM] = _exp2_pv([s], [vf]).astype(o_ref.dtype)
        s = s_next


def _rope_tables(seq):
    t = np.arange(seq)
    half = HEAD_DIM // 2
    inv = 1.0 / (ROPE_THETA ** (np.arange(0, half, 2, dtype=np.float32) / half))
    row = (t // GRID_W).astype(np.float32)
    col = (t % GRID_W).astype(np.float32)
    ang = jnp.concatenate([jnp.asarray(row[:, None] * inv)] * 2 + [jnp.asarray(col[:, None] * inv)] * 2, axis=-1)
    sign = np.where((np.arange(HEAD_DIM) % half) < half // 2, -1.0, 1.0).astype(np.float32)
    return jnp.cos(ang), jnp.sin(ang) * sign


def _gqa_attention(z, batch, seq, kv_heads, group, q_col, k_col, v_col, cache_k, cache_v, layer, gains, tables):
    past = cache_k.shape[3]
    tq = min(512, seq)
    nq = seq // tq
    qw = group * HEAD_DIM
    cspec = pl.BlockSpec((None, None, None, past, HEAD_DIM), lambda b, h, i: (b, layer, h, 0, 0))
    gspec = pl.BlockSpec((1, HEAD_DIM), lambda b, h, i: (0, 0))
    tspec = pl.BlockSpec((seq, HEAD_DIM), lambda b, h, i: (0, 0))
    return pl.pallas_call(
        functools.partial(_gqa_kernel, group=group, past=past, seq=seq, tq=tq),
        grid=(batch, kv_heads, nq),
        in_specs=[pl.BlockSpec((tq, qw), lambda b, h, i: (b * nq + i, q_col // group + h)),
                  pl.BlockSpec((seq, HEAD_DIM), lambda b, h, i: (b, k_col + h)),
                  pl.BlockSpec((seq, HEAD_DIM), lambda b, h, i: (b, v_col + h)),
                  cspec, cspec, gspec, gspec, tspec, tspec],
        out_specs=pl.BlockSpec((tq, qw), lambda b, h, i: (b * nq + i, h)),
        out_shape=jax.ShapeDtypeStruct((batch * seq, kv_heads * qw), BF16),
        scratch_shapes=[pltpu.VMEM((past + seq, HEAD_DIM), BF16), pltpu.VMEM((past + seq, 2 * HEAD_DIM), BF16)],
        compiler_params=_params(3),
        name="latent_gqa",
    )(z, z, z, cache_k, cache_v, gains[0], gains[1], tables[0], tables[1])


def _outproj_kernel(x_ref, na_ref, gq_ref, rt_ref, w_ref, g1_ref, sh2_ref, sc2_ref, lg_ref, lb_ref, wr_ref,
                    x1_ref, h2_ref, lt_ref, *, alpha, sub):
    n_exp = lt_ref.shape[1]
    for r in range(x_ref.shape[0] // sub):
        rs = slice(r * sub, (r + 1) * sub)
        mix = jnp.concatenate([na_ref[rs, :], gq_ref[rs, :], rt_ref[rs, :]], axis=1)
        y = _dot(mix, w_ref[...])
        x1 = _ln(alpha * x_ref[rs, :] + g1_ref[...] * y) * lg_ref[...] + lb_ref[...]
        x1_ref[rs, :] = x1
        h2 = _ln(x1) * (1.0 + sc2_ref[...]) + sh2_ref[...]
        hi, lo = _split_bf16(h2)
        h2_ref[rs, :] = hi
        both = _dot(hi, wr_ref[...])
        lt_ref[rs, :] = both[:, :n_exp] + both[:, n_exp:] + _dot(lo, wr_ref[:, 0:n_exp])


def _out_projection(x, na_o, gq_o, rt_o, w_bf16, layer, mod3, ln_g, ln_b, wr_hl, tokens_per_batch, per_request,
                    alpha):
    rows, d = x.shape
    tm = 512
    per = tokens_per_batch // tm if per_request else 1
    n_exp = wr_hl.shape[1] // 2

    def mod_spec(chunk):
        return pl.BlockSpec((None, 1, d), lambda i: (_mod_row(i, per, per_request) * 6 + chunk, 0, 0))

    def row_spec(width):
        return pl.BlockSpec((tm, width), lambda i: (i, 0))

    vec = pl.BlockSpec((1, d), lambda i: (0, 0))
    return pl.pallas_call(
        functools.partial(_outproj_kernel, alpha=alpha, sub=256),
        grid=(rows // tm,),
        in_specs=[row_spec(d), row_spec(na_o.shape[1]), row_spec(gq_o.shape[1]), row_spec(rt_o.shape[1]),
                  pl.BlockSpec((None,) + w_bf16.shape[1:], lambda i: (layer, 0, 0)),
                  mod_spec(2), mod_spec(3), mod_spec(4), vec, vec,
                  pl.BlockSpec(wr_hl.shape, lambda i: (0, 0))],
        out_specs=[row_spec(d), row_spec(d), row_spec(n_exp)],
        out_shape=[jax.ShapeDtypeStruct((rows, d), F32), jax.ShapeDtypeStruct((rows, d), BF16),
                   jax.ShapeDtypeStruct((rows, n_exp), F32)],
        compiler_params=_params(1),
        name="outproj_postnorm_router",
    )(x, na_o, gq_o, rt_o, w_bf16, mod3, mod3, mod3, ln_g, ln_b, wr_hl)


def _topk_kernel(lt_ref, slot_ref, gate_ref, before_ref, *, cap, seq, group):
    affs = []
    for g in range(group):
        logits = lt_ref[:, g * seq:(g + 1) * seq]
        m = jnp.max(logits, axis=0, keepdims=True)
        ex = jnp.exp(logits - m)
        affs.append(ex / jnp.sum(ex, axis=0, keepdims=True))
    aff = jnp.concatenate(affs, axis=0)
    n_exp = aff.shape[0]
    bits = lax.bitcast_convert_type(aff, jnp.int32)

    def count(mask):
        return jnp.sum(jnp.where(mask, 1.0, 0.0), axis=1, keepdims=True)

    def value_step(it, thr):
        cand = thr | jnp.left_shift(jnp.int32(1), 30 - it)
        return jnp.where(count(bits >= cand) >= cap, cand, thr)

    thr = lax.fori_loop(0, 31, value_step, jnp.zeros((n_exp, 1), jnp.int32))
    above = bits > thr
    tied = bits == thr
    need = cap - count(above)
    tok = lax.broadcasted_iota(jnp.int32, (n_exp, seq), 1)
    n_bits = int(seq - 1).bit_length()

    def index_step(it, bound):
        cand = bound | jnp.left_shift(jnp.int32(1), n_bits - 1 - it)
        return jnp.where(count(tied & (tok < cand)) < need, cand, bound)

    bound = lax.fori_loop(0, n_bits, index_step, jnp.zeros((n_exp, 1), jnp.int32))
    sel = above | (tied & (tok <= bound))
    self = jnp.where(sel, 1.0, 0.0)
    lanes = 128
    upper = jnp.where(lax.broadcasted_iota(jnp.int32, (lanes, lanes), 0)
                      < lax.broadcasted_iota(jnp.int32, (lanes, lanes), 1), 1.0, 0.0).astype(BF16)
    running = jnp.zeros((n_exp, 1), F32)
    lane = lax.broadcasted_iota(jnp.int32, (n_exp, lanes), 1)
    before = jnp.zeros((n_exp, lanes), F32)
    for blk in range(seq // lanes):
        sl = slice(blk * lanes, (blk + 1) * lanes)
        chunk = self[:, sl]
        before = jnp.where(lane == blk, running, before)
        pos = _dot(chunk.astype(BF16), upper) + running
        slot_ref[:, sl] = jnp.where(chunk > 0.0, pos.astype(jnp.int32), -1)
        running = running + jnp.sum(chunk, axis=1, keepdims=True)
    gate_ref[...] = aff
    before_ref[...] = before.astype(jnp.int32)


def _route(logits_t, batch, seq, cap):
    n_exp = logits_t.shape[0]
    assert seq // 128 <= 128
    group = math.gcd(batch, ROUTE_GROUP)
    rows = group * n_exp
    spec = pl.BlockSpec((rows, seq), lambda i: (i, 0))
    slot, gate, before = pl.pallas_call(
        functools.partial(_topk_kernel, cap=cap, seq=seq, group=group),
        grid=(batch // group,),
        in_specs=[pl.BlockSpec((n_exp, group * seq), lambda i: (0, i))],
        out_specs=[spec, spec, pl.BlockSpec((rows, 128), lambda i: (i, 0))],
        out_shape=[jax.ShapeDtypeStruct((batch * n_exp, seq), jnp.int32),
                   jax.ShapeDtypeStruct((batch * n_exp, seq), F32),
                   jax.ShapeDtypeStruct((batch * n_exp, 128), jnp.int32)],
        compiler_params=_params(1),
        name="route_topk",
    )(logits_t)
    return (slot.reshape(batch, n_exp, seq), gate.reshape(batch, n_exp, seq), before.reshape(batch, n_exp, 128))


def _window_plan(before, cap, seq):
    per = MOE_CHUNK // 128
    lo = before[:, :, 0:seq // 128:per]
    hi = jnp.concatenate([lo[:, :, 1:], jnp.full(lo.shape[:2] + (1,), cap, jnp.int32)], axis=2)
    base = jnp.minimum(lo // 16 * 16, cap - MOE_WINDOW)
    overflow = jnp.any(hi - base > MOE_WINDOW, axis=1)
    return base.transpose(0, 2, 1).reshape(-1), overflow.astype(jnp.int32).reshape(-1)


def _gather_kernel(slot_ref, gate_ref, h_ref, xs_ref, gs_ref, *, n_inner, cap):
    h = h_ref[...]
    seq = h.shape[0]
    row = lax.broadcasted_iota(jnp.int32, (cap, seq), 0)
    for e in range(n_inner):
        onehot = slot_ref[e:e + 1, :] == row
        xs_ref[e] = _dot(jnp.where(onehot, 1.0, 0.0).astype(BF16), h).astype(BF16)
        gs_ref[e] = jnp.sum(jnp.where(onehot, gate_ref[e:e + 1, :], 0.0), axis=1, keepdims=True)


def _gather(slot, gate, h, batch, seq, cap, n_inner):
    n_exp = slot.shape[1]
    d = h.shape[1]
    n_outer = n_exp // n_inner
    slot4 = slot.reshape(batch, n_outer, n_inner, seq)
    gate4 = gate.reshape(batch, n_outer, n_inner, seq)
    sspec = pl.BlockSpec((None, None, n_inner, seq), lambda b, e: (b, e, 0, 0))
    return pl.pallas_call(
        functools.partial(_gather_kernel, n_inner=n_inner, cap=cap),
        grid=(batch, n_outer),
        in_specs=[sspec, sspec, pl.BlockSpec((seq, d), lambda b, e: (b, 0))],
        out_specs=[pl.BlockSpec((n_inner, cap, d), lambda b, e: (e, b, 0)),
                   pl.BlockSpec((n_inner, cap, 1), lambda b, e: (e, b, 0))],
        out_shape=[jax.ShapeDtypeStruct((n_exp, batch * cap, d), BF16),
                   jax.ShapeDtypeStruct((n_exp, batch * cap, 1), F32)],
        compiler_params=_params(2),
        name="moe_gather",
    )(slot4, gate4, h)


def _gather_win_kernel(base_ref, flag_ref, slot_ref, gate_ref, h_ref, xs_ref, gs_ref, *, cap):
    k = pl.program_id(1)
    step = pl.program_id(0) * pl.num_programs(1) + k
    n_exp, chunk = slot_ref.shape
    win = MOE_WINDOW

    @pl.when(k == 0)
    def _():
        xs_ref[...] = jnp.zeros(xs_ref.shape, xs_ref.dtype)
        gs_ref[...] = jnp.zeros(gs_ref.shape, gs_ref.dtype)

    h = h_ref[...]

    @pl.when(flag_ref[step] == 0)
    def _():
        row = lax.broadcasted_iota(jnp.int32, (win, chunk), 0)
        bases = [pl.multiple_of(base_ref[step * n_exp + e], 16) for e in range(n_exp)]
        hots = [(slot_ref[e:e + 1, :] - bases[e]) == row for e in range(n_exp)]
        stacked = jnp.concatenate([jnp.where(hot, 1.0, 0.0).astype(BF16) for hot in hots], axis=0)
        picked = _dot(stacked, h)
        for e in range(n_exp):
            rows = pl.ds(bases[e], win)
            xs_ref[e, rows, :] += picked[e * win:(e + 1) * win].astype(BF16)
            gs_ref[e, rows, :] += jnp.sum(jnp.where(hots[e], gate_ref[e:e + 1, :], 0.0), axis=1, keepdims=True)

    @pl.when(flag_ref[step] != 0)
    def _():
        row = lax.broadcasted_iota(jnp.int32, (cap, chunk), 0)
        for e in range(n_exp):
            hot = slot_ref[e:e + 1, :] == row
            xs_ref[e] += _dot(jnp.where(hot, 1.0, 0.0).astype(BF16), h).astype(BF16)
            gs_ref[e] += jnp.sum(jnp.where(hot, gate_ref[e:e + 1, :], 0.0), axis=1, keepdims=True)


def _gather_windowed(slot, gate, h, base, flag, batch, seq, cap):
    n_exp = slot.shape[1]
    d = h.shape[1]
    nk = seq // MOE_CHUNK
    sspec = pl.BlockSpec((None, n_exp, MOE_CHUNK), lambda b, k, *_: (b, 0, k))
    return pl.pallas_call(
        functools.partial(_gather_win_kernel, cap=cap),
        grid_spec=pltpu.PrefetchScalarGridSpec(
            num_scalar_prefetch=2,
            grid=(batch, nk),
            in_specs=[sspec, sspec, pl.BlockSpec((MOE_CHUNK, d), lambda b, k, *_: (b * nk + k, 0))],
            out_specs=[pl.BlockSpec((n_exp, cap, d), lambda b, k, *_: (0, b, 0)),
                       pl.BlockSpec((n_exp, cap, 1), lambda b, k, *_: (0, b, 0))]),
        out_shape=[jax.ShapeDtypeStruct((n_exp, batch * cap, d), BF16),
                   jax.ShapeDtypeStruct((n_exp, batch * cap, 1), F32)],
        compiler_params=_params(2),
        name="moe_gather_windowed",
    )(base, flag, slot, gate, h)


def _ffn_kernel(xc_ref, gc_ref, xl_ref, gl_ref, wg_hbm, wu_hbm, wd_hbm, y_ref, wg_s, wu_s, wd_s, stg_g, stg_u,
                stg_d, sems, *, layer, n_chunks, ctx_steps):
    e = pl.program_id(0)
    m = pl.program_id(1)
    slot = e % 2
    rows_gu = wg_s.shape[1] // n_chunks
    rows_d = wd_s.shape[1] // n_chunks

    def chunk_copies(expert, c):
        gu = pl.ds(pl.multiple_of(c * rows_gu, rows_gu), rows_gu)
        dn = pl.ds(pl.multiple_of(c * rows_d, rows_d), rows_d)
        return (pltpu.make_async_copy(wg_hbm.at[layer, expert, gu, :], stg_g, sems.at[0]),
                pltpu.make_async_copy(wu_hbm.at[layer, expert, gu, :], stg_u, sems.at[1]),
                pltpu.make_async_copy(wd_hbm.at[layer, expert, dn, :], stg_d, sems.at[2]))

    def land(dst_slot, c, copies):
        for cp in copies:
            cp.wait()
        gu = pl.ds(pl.multiple_of(c * rows_gu, rows_gu), rows_gu)
        dn = pl.ds(pl.multiple_of(c * rows_d, rows_d), rows_d)
        wg_s[dst_slot, gu, :] = stg_g[...].astype(BF16)
        wu_s[dst_slot, gu, :] = stg_u[...].astype(BF16)
        wd_s[dst_slot, dn, :] = stg_d[...].astype(BF16)

    @pl.when((e == 0) & (m == 0))
    def _():
        for c in range(n_chunks):
            copies = chunk_copies(0, c)
            for cp in copies:
                cp.start()
            land(0, c, copies)

    prefetch = (m < n_chunks) & (e + 1 < pl.num_programs(0))

    @pl.when(prefetch)
    def _():
        for cp in chunk_copies(e + 1, m):
            cp.start()

    def swiglu(x_ref, g_ref):
        x = x_ref[...]
        a = _dot(x, wg_s[slot])
        u = _dot(x, wu_s[slot])
        y = _dot((_silu(a) * u).astype(BF16), wd_s[slot])
        y_ref[...] = (y * g_ref[...]).astype(y_ref.dtype)

    @pl.when(m < ctx_steps)
    def _():
        swiglu(xc_ref, gc_ref)

    @pl.when(m >= ctx_steps)
    def _():
        swiglu(xl_ref, gl_ref)

    @pl.when(prefetch)
    def _():
        land(1 - slot, m, chunk_copies(e + 1, m))


def _expert_ffn(xs_c, gs_c, xs_l, gs_l, wg, wu, wd, layer):
    n_exp, rows_c, d = xs_c.shape
    rows_l = xs_l.shape[1]
    ff = wg.shape[3]
    tm = math.gcd(math.gcd(rows_c, rows_l), 512)
    ctx_steps = rows_c // tm
    steps = ctx_steps + rows_l // tm
    n_chunks = 4 if steps >= 4 else (2 if steps >= 2 else 1)
    any_spec = pl.BlockSpec(memory_space=pl.ANY)

    def ctx_spec(width):
        return pl.BlockSpec((None, tm, width), lambda e, i: (e, jnp.minimum(i, ctx_steps - 1), 0))

    def lat_spec(width):
        return pl.BlockSpec((None, tm, width), lambda e, i: (e, jnp.maximum(i - ctx_steps, 0), 0))

    return pl.pallas_call(
        functools.partial(_ffn_kernel, layer=layer, n_chunks=n_chunks, ctx_steps=ctx_steps),
        grid=(n_exp, steps),
        in_specs=[ctx_spec(d), ctx_spec(1), lat_spec(d), lat_spec(1), any_spec, any_spec, any_spec],
        out_specs=pl.BlockSpec((None, tm, d), lambda e, i: (e, i, 0)),
        out_shape=jax.ShapeDtypeStruct((n_exp, rows_c + rows_l, d), BF16),
        scratch_shapes=[pltpu.VMEM((2, d, ff), BF16), pltpu.VMEM((2, d, ff), BF16), pltpu.VMEM((2, ff, d), BF16),
                        pltpu.VMEM((d // n_chunks, ff), F32), pltpu.VMEM((d // n_chunks, ff), F32),
                        pltpu.VMEM((ff // n_chunks, d), F32), pltpu.SemaphoreType.DMA((3,))],
        compiler_params=_params(2),
        name="moe_ffn",
    )(xs_c, gs_c, xs_l, gs_l, wg, wu, wd)


def _combine_kernel(x_ref, y_ref, slot_ref, g2_ref, lg_ref, lb_ref, o_ref, *, alpha, cap):
    n_exp = y_ref.shape[0]
    tt = x_ref.shape[0]
    slot = slot_ref[...]
    lanes = 128
    if cap < lanes and lanes % cap == 0 and n_exp % (lanes // cap) == 0:
        per = lanes // cap
        lane = lax.broadcasted_iota(jnp.int32, (tt, lanes), 1)
        hots = []
        for g in range(n_exp // per):
            hit = None
            for j in range(per):
                s = slot[:, g * per + j:g * per + j + 1]
                match = jnp.where(s >= 0, s + j * cap, -1) == lane
                hit = match if hit is None else hit | match
            hots.append(jnp.where(hit, 1.0, 0.0).astype(BF16))
        acc = _dot(jnp.concatenate(hots, axis=1), y_ref[...].reshape(n_exp * cap, y_ref.shape[2]))
    else:
        col = lax.broadcasted_iota(jnp.int32, (tt, cap), 1)
        acc = jnp.zeros(x_ref.shape, F32)
        for e in range(n_exp):
            onehot = jnp.where(slot[:, e:e + 1] == col, 1.0, 0.0).astype(BF16)
            acc += _dot(onehot, y_ref[e])
    o_ref[...] = _ln(alpha * x_ref[...] + g2_ref[...] * acc) * lg_ref[...] + lb_ref[...]


def _combine(x1, y, slot_t, mod3, ln_g, ln_b, batch, seq, cap, row_blk0, per_request, alpha):
    rows, d = x1.shape
    n_exp = y.shape[0]
    tt = min(256, seq)
    nt = seq // tt
    vec = pl.BlockSpec((1, d), lambda b, i: (0, 0))
    return pl.pallas_call(
        functools.partial(_combine_kernel, alpha=alpha, cap=cap),
        grid=(batch, nt),
        in_specs=[pl.BlockSpec((tt, d), lambda b, i: (b * nt + i, 0)),
                  pl.BlockSpec((n_exp, cap, d), lambda b, i: (0, row_blk0 + b, 0)),
                  pl.BlockSpec((None, tt, n_exp), lambda b, i: (b, i, 0)),
                  pl.BlockSpec((None, 1, d), lambda b, i: (_mod_row(b, 1, per_request) * 6 + 5, 0, 0)),
                  vec, vec],
        out_specs=pl.BlockSpec((tt, d), lambda b, i: (b * nt + i, 0)),
        out_shape=jax.ShapeDtypeStruct((rows, d), F32),
        compiler_params=_params(2),
        name="moe_combine_postnorm",
    )(x1, y, slot_t, mod3, ln_g, ln_b)


def _combine_win_kernel(base_ref, flag_ref, x_ref, y_ref, slot_ref, g2_ref, lg_ref, lb_ref, o_ref, acc_ref,
                        *, alpha, cap):
    step = pl.program_id(0) * pl.num_programs(1) + pl.program_id(1)
    n_exp = y_ref.shape[0]
    tt = x_ref.shape[0]
    win = MOE_WINDOW
    slot = slot_ref[...]

    @pl.when(flag_ref[step] == 0)
    def _():
        lane = lax.broadcasted_iota(jnp.int32, (tt, 2 * win), 1)
        hots, ys = [], []
        for pair in range(n_exp // 2):
            picks = []
            for half in range(2):
                e = 2 * pair + half
                base = pl.multiple_of(base_ref[step * n_exp + e], 16)
                rel = slot[:, e:e + 1] - base
                picks.append(jnp.where((rel >= 0) & (rel < win), rel + half * win, -1))
                ys.append(y_ref[e, pl.ds(base, win), :])
            hots.append(jnp.where((picks[0] == lane) | (picks[1] == lane), 1.0, 0.0).astype(BF16))
        acc_ref[...] = _dot(jnp.concatenate(hots, axis=1), jnp.concatenate(ys, axis=0))

    @pl.when(flag_ref[step] != 0)
    def _():
        col = lax.broadcasted_iota(jnp.int32, (tt, cap), 1)
        acc = jnp.zeros(x_ref.shape, F32)
        for e in range(n_exp):
            acc += _dot(jnp.where(slot[:, e:e + 1] == col, 1.0, 0.0).astype(BF16), y_ref[e])
        acc_ref[...] = acc

    o_ref[...] = _ln(alpha * x_ref[...] + g2_ref[...] * acc_ref[...]) * lg_ref[...] + lb_ref[...]


def _combine_windowed(x1, y, slot_t, base, flag, mod3, ln_g, ln_b, batch, seq, cap, row_blk0, alpha):
    rows, d = x1.shape
    n_exp = y.shape[0]
    tt = MOE_CHUNK
    nt = seq // tt
    assert 2 * MOE_WINDOW == 128 and n_exp % 2 == 0
    vec = pl.BlockSpec((1, d), lambda b, i, *_: (0, 0))
    return pl.pallas_call(
        functools.partial(_combine_win_kernel, alpha=alpha, cap=cap),
        grid_spec=pltpu.PrefetchScalarGridSpec(
            num_scalar_prefetch=2,
            grid=(batch, nt),
            in_specs=[pl.BlockSpec((tt, d), lambda b, i, *_: (b * nt + i, 0)),
                      pl.BlockSpec((n_exp, cap, d), lambda b, i, *_: (0, row_blk0 + b, 0)),
                      pl.BlockSpec((None, tt, n_exp), lambda b, i, *_: (b, i, 0)),
                      pl.BlockSpec((None, 1, d), lambda b, i, *_: (_mod_row(b, 1, True) * 6 + 5, 0, 0)),
                      vec, vec],
            out_specs=pl.BlockSpec((tt, d), lambda b, i, *_: (b * nt + i, 0)),
            scratch_shapes=[pltpu.VMEM((tt, d), F32)]),
        out_shape=jax.ShapeDtypeStruct((rows, d), F32),
        compiler_params=_params(2),
        name="moe_combine_windowed_postnorm",
    )(base, flag, x1, y, slot_t, mod3, ln_g, ln_b)


def kernel(x_prompt, x_sample, cache_na_k, cache_na_v, cache_gqa_k, cache_gqa_v, state_ret_fwd, state_ret_bwd,
           c, c_ctx, w_in, w_out, w_mod, b_mod, ln1_g, ln1_b, ln2_g, ln2_b, q_norm_g, k_norm_g, na_rpb,
           ret_decay_fwd, ret_decay_bwd, w_router, w_gate, w_up, w_down):
    batch, seq, d = x_prompt.shape
    dbatch, dseq, _ = x_sample.shape
    depth = w_in.shape[0]
    na_heads = cache_na_k.shape[2]
    kv_heads = cache_gqa_k.shape[2]
    ret_heads = state_ret_fwd.shape[2]
    n_exp = w_router.shape[2]
    hd = HEAD_DIM
    gqa_heads = (w_in.shape[2] // hd - 3 * na_heads - 2 * kv_heads - 4 * ret_heads)
    group = gqa_heads // kv_heads
    alpha = float((2 * depth) ** 0.25)
    c_na = 0
    c_gq = 3 * na_heads
    c_gk = c_gq + gqa_heads
    c_gv = c_gk + kv_heads
    c_rt = c_gv + kv_heads
    cap_c = EC_CAPACITY_FACTOR * seq // n_exp
    cap_l = EC_CAPACITY_FACTOR * dseq // n_exp
    assert dbatch + 1 <= MOD_ROWS and (batch * cap_c) % cap_l == 0
    lat_blk0 = batch * cap_c // cap_l

    cond = jnp.zeros((MOD_ROWS, d), F32).at[0].set(c_ctx).at[1:1 + dbatch].set(c)
    mod_all = _modulation(cond, w_mod, b_mod)
    tables = _rope_tables(dseq)
    bias = _na_bias(na_rpb, dseq // GRID_W)

    w_in_b, w_out_b = w_in.astype(BF16), w_out.astype(BF16)

    xp = x_prompt.reshape(batch * seq, d)
    xs = x_sample.reshape(dbatch * dseq, d)
    outs = [[] for _ in range(6)]
    for l in range(depth):
        mod3 = mod_all[l].reshape(MOD_ROWS * 6, 1, d)
        wr_t = jnp.concatenate(_split_bf16(w_router[l]), axis=1)
        gains = (q_norm_g[l].reshape(1, hd), k_norm_g[l].reshape(1, hd))
        l1g, l1b = ln1_g[l].reshape(1, d), ln1_b[l].reshape(1, d)
        l2g, l2b = ln2_g[l].reshape(1, d), ln2_b[l].reshape(1, d)

        zc = _in_projection(xp, mod3, w_in_b, l, seq, False, F32)
        (na_c,) = _ctx_attention(zc, batch, seq, na_heads, 1, c_na, c_na + na_heads, c_na + 2 * na_heads)
        gq_c, gk_n = _ctx_attention(zc, batch, seq, kv_heads, group, c_gq, c_gk, c_gv, gains)
        rt_c, s_f, s_b = _retention(zc, batch, seq, ret_heads, c_rt, ret_decay_fwd[l], ret_decay_bwd[l],
                                    state_out=True)

        def heads_of(col, n):
            return zc[:, col * hd:(col + n) * hd].reshape(batch, seq, n, hd).transpose(0, 2, 1, 3)

        outs[0].append(heads_of(c_na + na_heads, na_heads))
        outs[1].append(heads_of(c_na + 2 * na_heads, na_heads))
        outs[2].append(gk_n)
        outs[3].append(heads_of(c_gv, kv_heads))
        outs[4].append(s_f)
        outs[5].append(s_b)

        zl = _in_projection(xs, mod3, w_in_b, l, dseq, True, BF16)
        (na_l,) = [_na_attention(zl, dbatch, dseq, na_heads, cache_na_k, cache_na_v, l, bias)]
        gq_l = _gqa_attention(zl, dbatch, dseq, kv_heads, group, c_gq, c_gk, c_gv, cache_gqa_k, cache_gqa_v, l,
                              gains, tables)
        (rt_l,) = _retention(zl, dbatch, dseq, ret_heads, c_rt, ret_decay_fwd[l], ret_decay_bwd[l],
                             states=((state_ret_fwd, state_ret_bwd), l))

        xp1, hp, lt_c = _out_projection(xp, na_c, gq_c, rt_c, w_out_b, l, mod3, l1g, l1b, wr_t, seq, False, alpha)
        xs1, hs, lt_l = _out_projection(xs, na_l, gq_l, rt_l, w_out_b, l, mod3, l1g, l1b, wr_t, dseq, True, alpha)

        slot_c, gate_c, _ = _route(lt_c.T, batch, seq, cap_c)
        slot_l, gate_l, before_l = _route(lt_l.T, dbatch, dseq, cap_l)
        windowed = cap_l > MOE_WINDOW and dseq % MOE_CHUNK == 0
        if windowed:
            base_l, flag_l = _window_plan(before_l, cap_l, dseq)
            rows_l, gates_l = _gather_windowed(slot_l, gate_l, hs, base_l, flag_l, dbatch, dseq, cap_l)
        else:
            rows_l, gates_l = _gather(slot_l, gate_l, hs, dbatch, dseq, cap_l, 2)
        rows_c, gates_c = _gather(slot_c, gate_c, hp, batch, seq, cap_c, n_exp)
        y = _expert_ffn(rows_c, gates_c, rows_l, gates_l, w_gate, w_up, w_down, l)
        xp = _combine(xp1, y, slot_c.transpose(0, 2, 1), mod3, l2g, l2b, batch, seq, cap_c, 0, False, alpha)
        if windowed:
            xs = _combine_windowed(xs1, y, slot_l.transpose(0, 2, 1), base_l, flag_l, mod3, l2g, l2b, dbatch, dseq,
                                   cap_l, lat_blk0, alpha)
        else:
            xs = _combine(xs1, y, slot_l.transpose(0, 2, 1), mod3, l2g, l2b, dbatch, dseq, cap_l, lat_blk0, True,
                          alpha)

    stacked = [jnp.stack(o, axis=1) for o in outs]
    return (xp.reshape(batch, seq, d), xs.reshape(dbatch, dseq, d), *stacked)
```

```python
import functools
import math

import numpy as np
import jax
import jax.numpy as jnp
from jax import lax
from jax.experimental import pallas as pl
from jax.experimental.pallas import tpu as pltpu

F32 = jnp.float32
BF16 = jnp.bfloat16

HEAD_DIM = 128
GRID_W = 64
NA_WIN_R = 8
NA_WIN_C = 16
ROPE_THETA = 10000.0
EC_CAPACITY_FACTOR = 2
LN_EPS = 1e-5
RMS_EPS = 1e-6
NEG_INF = -1e30
ATTN_SCALE = HEAD_DIM ** -0.5
LOG2E = 1.4426950408889634
NA_BLOCK_ROWS = 4
ROUTE_GROUP = 16
MOE_CHUNK = 256
MOE_WINDOW = 64
MOD_ROWS = 16
VMEM_LIMIT = 56 * 1024 * 1024

_NT = (((1,), (1,)), ((), ()))
_NN = (((1,), (0,)), ((), ()))


def _params(n_grid, vmem=VMEM_LIMIT):
    return pltpu.CompilerParams(dimension_semantics=("arbitrary",) * n_grid, vmem_limit_bytes=vmem)


def _dot(a, b, dims=_NN):
    return lax.dot_general(a, b, dims, preferred_element_type=F32)


def _split_bf16(x):
    hi = x.astype(BF16)
    lo = (x - hi.astype(F32)).astype(BF16)
    return hi, lo


def _dot3(a, b, dims=_NN):
    ah, al = _split_bf16(a)
    bh, bl = _split_bf16(b)
    return _dot(ah, bh, dims) + _dot(al, bh, dims) + _dot(ah, bl, dims)


def _silu(x):
    return x / (1.0 + jnp.exp(-x))


def _ln(x):
    mu = jnp.mean(x, axis=-1, keepdims=True)
    xc = x - mu
    var = jnp.mean(xc * xc, axis=-1, keepdims=True)
    return xc * lax.rsqrt(var + LN_EPS)


def _rms(x, g):
    return x * lax.rsqrt(jnp.mean(x * x, axis=-1, keepdims=True) + RMS_EPS) * g


def _rope(x, cos, sin_signed):
    lane = lax.broadcasted_iota(jnp.int32, x.shape, 1) & (HEAD_DIM // 2 - 1)
    partner = jnp.where(lane < HEAD_DIM // 4,
                        pltpu.roll(x, HEAD_DIM - HEAD_DIM // 4, 1),
                        pltpu.roll(x, HEAD_DIM // 4, 1))
    return x * cos + partner * sin_signed


def _softmax_pv(scores, values):
    m = functools.reduce(jnp.maximum, [jnp.max(s, axis=-1, keepdims=True) for s in scores])
    ps = [jnp.exp(s - m) for s in scores]
    denom = functools.reduce(lambda a, b: a + b, [jnp.sum(p, axis=-1, keepdims=True) for p in ps])
    o = functools.reduce(lambda a, b: a + b, [_dot(p.astype(BF16), v) for p, v in zip(ps, values)])
    return o / denom


def _mod_kernel(c_ref, w_ref, b_ref, o_ref):
    a = _silu(c_ref[...])
    o_ref[...] = _dot3(a, w_ref[...]) + b_ref[...]


def _modulation(cond, w_mod, b_mod):
    depth, d, n = w_mod.shape
    tn = 768
    return pl.pallas_call(
        _mod_kernel,
        grid=(depth, n // tn),
        in_specs=[pl.BlockSpec((MOD_ROWS, d), lambda l, j: (0, 0)),
                  pl.BlockSpec((None, d, tn), lambda l, j: (l, 0, j)),
                  pl.BlockSpec((None, 1, tn), lambda l, j: (l, 0, j))],
        out_specs=pl.BlockSpec((None, MOD_ROWS, tn), lambda l, j: (l, 0, j)),
        out_shape=jax.ShapeDtypeStruct((depth, MOD_ROWS, n), F32),
        compiler_params=_params(2),
        name="adaln_mod",
    )(cond, w_mod, b_mod.reshape(depth, 1, n))


def _inproj_kernel(x_ref, sh_ref, sc_ref, w_ref, z_ref, *, sub, tn):
    for r in range(x_ref.shape[0] // sub):
        rs = slice(r * sub, (r + 1) * sub)
        h = (_ln(x_ref[rs, :]) * (1.0 + sc_ref[...]) + sh_ref[...]).astype(BF16)
        for c in range(w_ref.shape[1] // tn):
            cs = slice(c * tn, (c + 1) * tn)
            z_ref[rs, cs] = _dot(h, w_ref[:, cs]).astype(z_ref.dtype)


def _mod_row(block, blocks_per_request, per_request):
    return 1 + block // blocks_per_request if per_request else 0


def _in_projection(x, mod3, w_bf16, layer, tokens_per_batch, per_request, out_dtype):
    rows, d = x.shape
    n = w_bf16.shape[2]
    tm = min(512 if out_dtype == BF16 else 256, tokens_per_batch)
    per = tokens_per_batch // tm

    def mod_spec(chunk):
        return pl.BlockSpec((None, 1, d), lambda i: (_mod_row(i, per, per_request) * 6 + chunk, 0, 0))

    return pl.pallas_call(
        functools.partial(_inproj_kernel, sub=min(256, tm), tn=n // 2),
        grid=(rows // tm,),
        in_specs=[pl.BlockSpec((tm, d), lambda i: (i, 0)),
                  mod_spec(0), mod_spec(1),
                  pl.BlockSpec((None, d, n), lambda i: (layer, 0, 0), pipeline_mode=pl.Buffered(1))],
        out_specs=pl.BlockSpec((tm, n), lambda i: (i, 0)),
        out_shape=jax.ShapeDtypeStruct((rows, n), out_dtype),
        compiler_params=_params(1),
        name="ln_inproj",
    )(x, mod3, mod3, w_bf16)


def _ctx_attn_kernel(*refs, group, normed):
    if normed:
        q_ref, k_ref, v_ref, qg_ref, kg_ref, o_ref, kn_ref = refs
    else:
        q_ref, k_ref, v_ref, o_ref = refs
    for hh in range(k_ref.shape[1] // HEAD_DIM):
        k = k_ref[:, hh * HEAD_DIM:(hh + 1) * HEAD_DIM]
        if normed:
            k = _rms(k, kg_ref[...])
            kn_ref[...] = k
        kb = k.astype(BF16)
        vb = v_ref[:, hh * HEAD_DIM:(hh + 1) * HEAD_DIM].astype(BF16)
        for g in range(group):
            cols = slice((hh * group + g) * HEAD_DIM, (hh * group + g + 1) * HEAD_DIM)
            q = q_ref[:, cols]
            if normed:
                q = _rms(q, qg_ref[...])
            s = _dot(q.astype(BF16), kb, _NT) * ATTN_SCALE
            o_ref[:, cols] = _softmax_pv([s], [vb]).astype(o_ref.dtype)


def _ctx_attention(z, batch, seq, heads, group, q_col, k_col, v_col, gains=None):
    normed = gains is not None
    hps = 1 if normed else heads
    assert q_col % (group * hps) == 0 and k_col % hps == 0 and v_col % hps == 0
    qw = group * hps * HEAD_DIM
    kw = hps * HEAD_DIM
    in_specs = [pl.BlockSpec((seq, qw), lambda b, h: (b, q_col // (group * hps) + h)),
                pl.BlockSpec((seq, kw), lambda b, h: (b, k_col // hps + h)),
                pl.BlockSpec((seq, kw), lambda b, h: (b, v_col // hps + h))]
    args = [z, z, z]
    out_specs = [pl.BlockSpec((seq, qw), lambda b, h: (b, h))]
    out_shape = [jax.ShapeDtypeStruct((batch * seq, heads * group * HEAD_DIM), BF16)]
    if normed:
        in_specs += [pl.BlockSpec((1, HEAD_DIM), lambda b, h: (0, 0))] * 2
        args += list(gains)
        out_specs.append(pl.BlockSpec((None, None, seq, HEAD_DIM), lambda b, h: (b, h, 0, 0)))
        out_shape.append(jax.ShapeDtypeStruct((batch, heads, seq, HEAD_DIM), F32))
    return pl.pallas_call(
        functools.partial(_ctx_attn_kernel, group=group, normed=normed),
        grid=(batch, heads // hps),
        in_specs=in_specs, out_specs=out_specs, out_shape=out_shape,
        compiler_params=_params(2),
        name="ctx_gqa" if normed else "ctx_na",
    )(*args)


def _log_sigmoid(x):
    return -jnp.log1p(jnp.exp(-x))


def _retention_kernel(*refs, seq, tq, with_state_in, with_state_out):
    q_ref, k_ref, v_ref, g_ref, df_ref, db_ref = refs[:6]
    refs = refs[6:]
    if with_state_in:
        sf_ref, sb_ref = refs[:2]
        refs = refs[2:]
    o_ref = refs[0]
    decay_ref = refs[-1]
    chunk = tq
    n_chunks = seq // chunk
    il = lax.broadcasted_iota(jnp.int32, (chunk, 1), 0).astype(F32)
    for hh in range(decay_ref.shape[0]):
        hs = slice(hh * HEAD_DIM, (hh + 1) * HEAD_DIM)
        lgf = _log_sigmoid(df_ref[hh])
        lgb = _log_sigmoid(db_ref[hh])

        @pl.when(pl.program_id(1) == 0)
        def _():
            i = lax.broadcasted_iota(jnp.int32, (chunk, chunk), 0)
            j = lax.broadcasted_iota(jnp.int32, (chunk, chunk), 1)
            dist = (i - j).astype(F32)
            decay = jnp.where(dist == 0.0, 2.0, jnp.exp(jnp.where(dist > 0.0, lgf * dist, -lgb * dist)))
            decay_ref[hh] = decay * ATTN_SCALE

        k_dec_f = jnp.exp(lgf * (chunk - 1.0 - il)) * ATTN_SCALE
        k_dec_b = jnp.exp(lgb * il) * ATTN_SCALE
        add_f, add_b = [], []
        for c in range(n_chunks):
            cs = slice(c * chunk, (c + 1) * chunk)
            kf = k_ref[cs, hs].astype(F32)
            vb = v_ref[cs, hs].astype(BF16)
            add_f.append(_dot((kf * k_dec_f).T.astype(BF16), vb))
            add_b.append(_dot((kf * k_dec_b).T.astype(BF16), vb))
        state_f, state_b = [None] * n_chunks, [None] * n_chunks
        carry_f = jnp.exp(lgf * chunk)
        carry_b = jnp.exp(lgb * chunk)
        s = sf_ref[hh] if with_state_in else None
        for c in range(n_chunks):
            state_f[c] = s
            s = add_f[c] if s is None else carry_f * s + add_f[c]
        final_f = s
        s = sb_ref[hh] if with_state_in else None
        for c in reversed(range(n_chunks)):
            state_b[c] = s
            s = add_b[c] if s is None else carry_b * s + add_b[c]
        final_b = s

        q_dec_f = jnp.exp(lgf * (il + 1.0))
        q_dec_b = jnp.exp(lgb * (chunk - il))
        for c in range(n_chunks):
            cs = slice(c * chunk, (c + 1) * chunk)
            q = q_ref[cs, hs]
            att = _dot(q.astype(BF16), k_ref[cs, hs].astype(BF16), _NT) * decay_ref[hh]
            o = _dot(att.astype(BF16), v_ref[cs, hs].astype(BF16))
            qf = q.astype(F32)
            if state_f[c] is not None:
                o += _dot((qf * q_dec_f).astype(BF16), state_f[c].astype(BF16))
            if state_b[c] is not None:
                o += _dot((qf * q_dec_b).astype(BF16), state_b[c].astype(BF16))
            o_ref[cs, hs] = (_ln(o) * _silu(g_ref[cs, hs].astype(F32))).astype(o_ref.dtype)
        if with_state_out:
            nsf_ref, nsb_ref = refs[1:3]
            nsf_ref[hh] = final_f
            nsb_ref[hh] = final_b


def _retention(z, batch, seq, heads, col0, dec_f, dec_b, states=None, state_out=False):
    chunk = min(512, seq)
    hps = heads if seq == chunk else 1
    assert col0 % hps == 0 and seq % chunk == 0
    w = hps * HEAD_DIM
    c0 = col0 // hps
    nh = heads // hps
    in_specs = [pl.BlockSpec((seq, w), lambda h, b: (b, c0 + h)),
                pl.BlockSpec((seq, w), lambda h, b: (b, c0 + nh + h)),
                pl.BlockSpec((seq, w), lambda h, b: (b, c0 + 2 * nh + h)),
                pl.BlockSpec((seq, w), lambda h, b: (b, c0 + 3 * nh + h)),
                pl.BlockSpec((hps, 1, 1), lambda h, b: (h, 0, 0)),
                pl.BlockSpec((hps, 1, 1), lambda h, b: (h, 0, 0))]
    args = [z, z, z, z, dec_f.reshape(heads, 1, 1), dec_b.reshape(heads, 1, 1)]
    if states is not None:
        (sf, sb), layer = states
        spec = pl.BlockSpec((None, None, hps, HEAD_DIM, HEAD_DIM), lambda h, b: (b, layer, h, 0, 0))
        in_specs += [spec, spec]
        args += [sf, sb]
    out_specs = [pl.BlockSpec((seq, w), lambda h, b: (b, h))]
    out_shape = [jax.ShapeDtypeStruct((batch * seq, heads * HEAD_DIM), BF16)]
    if state_out:
        spec = pl.BlockSpec((None, hps, HEAD_DIM, HEAD_DIM), lambda h, b: (b, h, 0, 0))
        out_specs += [spec, spec]
        out_shape += [jax.ShapeDtypeStruct((batch, heads, HEAD_DIM, HEAD_DIM), F32)] * 2
    return pl.pallas_call(
        functools.partial(_retention_kernel, seq=seq, tq=chunk, with_state_in=states is not None,
                          with_state_out=state_out),
        grid=(nh, batch),
        in_specs=in_specs, out_specs=out_specs, out_shape=out_shape,
        scratch_shapes=[pltpu.VMEM((hps, chunk, chunk), F32)],
        compiler_params=_params(2),
        name="retention",
    )(*args)


def _with_ones(v):
    return jnp.concatenate([v, jnp.ones(v.shape, v.dtype)], axis=1)


def _exp2_pv(scores, values_with_ones):
    m = functools.reduce(jnp.maximum, [jnp.max(s, axis=-1, keepdims=True) for s in scores])
    r = functools.reduce(lambda a, b: a + b,
                         [_dot(jnp.exp2(s - m).astype(BF16), v) for s, v in zip(scores, values_with_ones)])
    return r[:, :HEAD_DIM] / r[:, HEAD_DIM:HEAD_DIM + 1]


def _na_plan(rows):
    kr = min(NA_WIN_R, rows)
    nq = min(NA_BLOCK_ROWS, rows)
    nk = min(nq + kr - 1, rows)
    assert rows % nq == 0
    blocks, cases = [], []
    for r0 in range(0, rows, nq):
        start = int(np.clip(r0 - kr // 2, 0, rows - nk))
        win = [int(np.clip(r - kr // 2, 0, rows - kr)) for r in range(r0, r0 + nq)]
        assert all(start <= w and w + kr <= start + nk for w in win)
        key = (start - r0,) + tuple(w - start for w in win)
        if key not in cases:
            cases.append(key)
        blocks.append((r0, start, cases.index(key)))
    return kr, nq, nk, blocks, cases


def _na_kernel(q_ref, k_ref, v_ref, ck_ref, cv_ref, bias_ref, o_ref, vx_ref, cvx_ref, *, blocks, nq, nk):
    vx_ref[...] = _with_ones(v_ref[...])
    cvx_ref[...] = _with_ones(cv_ref[...].astype(BF16))
    ck = ck_ref[...].astype(BF16)
    c = ATTN_SCALE * LOG2E
    for r0, start, case in blocks:
        qs = slice(r0 * GRID_W, (r0 + nq) * GRID_W)
        ks = slice(start * GRID_W, (start + nk) * GRID_W)
        q = q_ref[qs, :]
        s_loc = _dot(q, k_ref[ks, :], _NT) * c + bias_ref[case]
        s_ctx = _dot(q, ck, _NT) * c
        o_ref[qs, :] = _exp2_pv([s_loc, s_ctx], [vx_ref[ks, :], cvx_ref[...]]).astype(o_ref.dtype)


def _na_bias(rpb, rows):
    kr, nq, nk, _, cases = _na_plan(rows)
    dr = np.zeros((len(cases), nq, nk), np.int32)
    row_ok = np.zeros((len(cases), nq, nk), bool)
    for c, key in enumerate(cases):
        rel, offs = key[0], key[1:]
        for ri in range(nq):
            for ju in range(nk):
                row_ok[c, ri, ju] = 0 <= ju - offs[ri] < kr
                dr[c, ri, ju] = np.clip(rel + ju - ri + NA_WIN_R - 1, 0, 2 * NA_WIN_R - 2)
    cq = np.arange(GRID_W)
    ck = np.arange(GRID_W)
    col_start = np.clip(cq - NA_WIN_C // 2, 0, GRID_W - NA_WIN_C)
    col_ok = (ck[None, :] >= col_start[:, None]) & (ck[None, :] < col_start[:, None] + NA_WIN_C)
    dc = np.clip(ck[None, :] - cq[:, None] + (NA_WIN_C - 1), 0, 2 * NA_WIN_C - 2)
    pick_col = (dc.reshape(-1)[None, :] == np.arange(2 * NA_WIN_C - 1)[:, None]).astype(np.float32)
    n_l, n_h = rpb.shape[:2]
    t = jnp.take(rpb.astype(F32), dr.reshape(-1), axis=2)
    t = jnp.einsum("lhxb,bn->lhxn", t, pick_col, precision=lax.Precision.HIGHEST)
    t = t.reshape(n_l, n_h, len(cases), nq, nk, GRID_W, GRID_W).transpose(0, 1, 2, 3, 5, 4, 6)
    ok = row_ok[:, :, None, :, None] & col_ok[None, None, :, None, :]
    t = jnp.where(ok, t * LOG2E, NEG_INF)
    return t.reshape(n_l, n_h, len(cases), nq * GRID_W, nk * GRID_W)


def _na_attention(z, batch, seq, heads, cache_k, cache_v, layer, bias):
    _, nq, nk, blocks, cases = _na_plan(seq // GRID_W)
    past = cache_k.shape[3]
    cspec = pl.BlockSpec((None, None, None, past, HEAD_DIM), lambda b, h: (b, layer, h, 0, 0))
    return pl.pallas_call(
        functools.partial(_na_kernel, blocks=blocks, nq=nq, nk=nk),
        grid=(batch, heads),
        in_specs=[pl.BlockSpec((seq, HEAD_DIM), lambda b, h: (b, h)),
                  pl.BlockSpec((seq, HEAD_DIM), lambda b, h: (b, heads + h)),
                  pl.BlockSpec((seq, HEAD_DIM), lambda b, h: (b, 2 * heads + h)),
                  cspec, cspec,
                  pl.BlockSpec((None, None, len(cases), nq * GRID_W, nk * GRID_W),
                               lambda b, h: (layer, h, 0, 0, 0))],
        out_specs=pl.BlockSpec((seq, HEAD_DIM), lambda b, h: (b, h)),
        out_shape=jax.ShapeDtypeStruct((batch * seq, heads * HEAD_DIM), BF16),
        scratch_shapes=[pltpu.VMEM((seq, 2 * HEAD_DIM), BF16), pltpu.VMEM((past, 2 * HEAD_DIM), BF16)],
        compiler_params=_params(2),
        name="latent_na",
    )(z, z, z, cache_k, cache_v, bias)


def _gqa_kernel(q_ref, k_ref, v_ref, ck_ref, cv_ref, qg_ref, kg_ref, cos_ref, sin_ref, o_ref, kf_ref, vf_ref,
                *, group, past, seq, tq):
    i = pl.program_id(2)

    @pl.when(i == 0)
    def _():
        kf_ref[0:past, :] = ck_ref[...].astype(BF16)
        vf_ref[0:past, :] = _with_ones(cv_ref[...].astype(BF16))
        k = _rope(_rms(k_ref[...].astype(F32), kg_ref[...]), cos_ref[...], sin_ref[...])
        kf_ref[past:past + seq, :] = k.astype(BF16)
        vf_ref[past:past + seq, :] = _with_ones(v_ref[...])

    t0 = pl.multiple_of(i * tq, tq)
    cos = cos_ref[pl.ds(t0, tq), :]
    sin = sin_ref[pl.ds(t0, tq), :]
    kf = kf_ref[...]
    vf = vf_ref[...]
    def scores(g):
        q = q_ref[:, g * HEAD_DIM:(g + 1) * HEAD_DIM].astype(F32)
        q = _rope(_rms(q, qg_ref[...]), cos, sin) * (ATTN_SCALE * LOG2E)
        return _dot(q.astype(BF16), kf, _NT)

    s = scores(0)
    for g in range(group):
        s_next = scores(g + 1) if g + 1 < group else None
        o_ref[:, g * HEAD_DIM:(g + 1) * HEAD_DIM] = _exp2_pv([s], [vf]).astype(o_ref.dtype)
        s = s_next


def _rope_tables(seq):
    t = np.arange(seq)
    half = HEAD_DIM // 2
    inv = 1.0 / (ROPE_THETA ** (np.arange(0, half, 2, dtype=np.float32) / half))
    row = (t // GRID_W).astype(np.float32)
    col = (t % GRID_W).astype(np.float32)
    ang = jnp.concatenate([jnp.asarray(row[:, None] * inv)] * 2 + [jnp.asarray(col[:, None] * inv)] * 2, axis=-1)
    sign = np.where((np.arange(HEAD_DIM) % half) < half // 2, -1.0, 1.0).astype(np.float32)
    return jnp.cos(ang), jnp.sin(ang) * sign


def _gqa_attention(z, batch, seq, kv_heads, group, q_col, k_col, v_col, cache_k, cache_v, layer, gains, tables):
    past = cache_k.shape[3]
    tq = min(512, seq)
    nq = seq // tq
    qw = group * HEAD_DIM
    cspec = pl.BlockSpec((None, None, None, past, HEAD_DIM), lambda b, h, i: (b, layer, h, 0, 0))
    gspec = pl.BlockSpec((1, HEAD_DIM), lambda b, h, i: (0, 0))
    tspec = pl.BlockSpec((seq, HEAD_DIM), lambda b, h, i: (0, 0))
    return pl.pallas_call(
        functools.partial(_gqa_kernel, group=group, past=past, seq=seq, tq=tq),
        grid=(batch, kv_heads, nq),
        in_specs=[pl.BlockSpec((tq, qw), lambda b, h, i: (b * nq + i, q_col // group + h)),
                  pl.BlockSpec((seq, HEAD_DIM), lambda b, h, i: (b, k_col + h)),
                  pl.BlockSpec((seq, HEAD_DIM), lambda b, h, i: (b, v_col + h)),
                  cspec, cspec, gspec, gspec, tspec, tspec],
        out_specs=pl.BlockSpec((tq, qw), lambda b, h, i: (b * nq + i, h)),
        out_shape=jax.ShapeDtypeStruct((batch * seq, kv_heads * qw), BF16),
        scratch_shapes=[pltpu.VMEM((past + seq, HEAD_DIM), BF16), pltpu.VMEM((past + seq, 2 * HEAD_DIM), BF16)],
        compiler_params=_params(3),
        name="latent_gqa",
    )(z, z, z, cache_k, cache_v, gains[0], gains[1], tables[0], tables[1])


def _outproj_kernel(x_ref, na_ref, gq_ref, rt_ref, w_ref, g1_ref, sh2_ref, sc2_ref, lg_ref, lb_ref, wr_ref,
                    x1_ref, h2_ref, lt_ref, *, alpha, sub):
    n_exp = lt_ref.shape[1]
    for r in range(x_ref.shape[0] // sub):
        rs = slice(r * sub, (r + 1) * sub)
        mix = jnp.concatenate([na_ref[rs, :], gq_ref[rs, :], rt_ref[rs, :]], axis=1)
        y = _dot(mix, w_ref[...])
        x1 = _ln(alpha * x_ref[rs, :] + g1_ref[...] * y) * lg_ref[...] + lb_ref[...]
        x1_ref[rs, :] = x1
        h2 = _ln(x1) * (1.0 + sc2_ref[...]) + sh2_ref[...]
        hi, lo = _split_bf16(h2)
        h2_ref[rs, :] = hi
        both = _dot(hi, wr_ref[...])
        lt_ref[rs, :] = both[:, :n_exp] + both[:, n_exp:] + _dot(lo, wr_ref[:, 0:n_exp])


def _out_projection(x, na_o, gq_o, rt_o, w_bf16, layer, mod3, ln_g, ln_b, wr_hl, tokens_per_batch, per_request,
                    alpha):
    rows, d = x.shape
    tm = 512
    per = tokens_per_batch // tm if per_request else 1
    n_exp = wr_hl.shape[1] // 2

    def mod_spec(chunk):
        return pl.BlockSpec((None, 1, d), lambda i: (_mod_row(i, per, per_request) * 6 + chunk, 0, 0))

    def row_spec(width):
        return pl.BlockSpec((tm, width), lambda i: (i, 0))

    vec = pl.BlockSpec((1, d), lambda i: (0, 0))
    return pl.pallas_call(
        functools.partial(_outproj_kernel, alpha=alpha, sub=256),
        grid=(rows // tm,),
        in_specs=[row_spec(d), row_spec(na_o.shape[1]), row_spec(gq_o.shape[1]), row_spec(rt_o.shape[1]),
                  pl.BlockSpec((None,) + w_bf16.shape[1:], lambda i: (layer, 0, 0)),
                  mod_spec(2), mod_spec(3), mod_spec(4), vec, vec,
                  pl.BlockSpec(wr_hl.shape, lambda i: (0, 0))],
        out_specs=[row_spec(d), row_spec(d), row_spec(n_exp)],
        out_shape=[jax.ShapeDtypeStruct((rows, d), F32), jax.ShapeDtypeStruct((rows, d), BF16),
                   jax.ShapeDtypeStruct((rows, n_exp), F32)],
        compiler_params=_params(1),
        name="outproj_postnorm_router",
    )(x, na_o, gq_o, rt_o, w_bf16, mod3, mod3, mod3, ln_g, ln_b, wr_hl)


def _topk_kernel(lt_ref, slot_ref, gate_ref, before_ref, *, cap, seq, group):
    affs = []
    for g in range(group):
        logits = lt_ref[:, g * seq:(g + 1) * seq]
        m = jnp.max(logits, axis=0, keepdims=True)
        ex = jnp.exp(logits - m)
        affs.append(ex / jnp.sum(ex, axis=0, keepdims=True))
    aff = jnp.concatenate(affs, axis=0)
    n_exp = aff.shape[0]
    bits = lax.bitcast_convert_type(aff, jnp.int32)

    def count(mask):
        return jnp.sum(jnp.where(mask, 1.0, 0.0), axis=1, keepdims=True)

    def value_step(it, thr):
        cand = thr | jnp.left_shift(jnp.int32(1), 30 - it)
        return jnp.where(count(bits >= cand) >= cap, cand, thr)

    thr = lax.fori_loop(0, 31, value_step, jnp.zeros((n_exp, 1), jnp.int32))
    above = bits > thr
    tied = bits == thr
    need = cap - count(above)
    tok = lax.broadcasted_iota(jnp.int32, (n_exp, seq), 1)
    n_bits = int(seq - 1).bit_length()

    def index_step(it, bound):
        cand = bound | jnp.left_shift(jnp.int32(1), n_bits - 1 - it)
        return jnp.where(count(tied & (tok < cand)) < need, cand, bound)

    bound = lax.fori_loop(0, n_bits, index_step, jnp.zeros((n_exp, 1), jnp.int32))
    sel = above | (tied & (tok <= bound))
    self = jnp.where(sel, 1.0, 0.0)
    lanes = 128
    upper = jnp.where(lax.broadcasted_iota(jnp.int32, (lanes, lanes), 0)
                      < lax.broadcasted_iota(jnp.int32, (lanes, lanes), 1), 1.0, 0.0).astype(BF16)
    running = jnp.zeros((n_exp, 1), F32)
    lane = lax.broadcasted_iota(jnp.int32, (n_exp, lanes), 1)
    before = jnp.zeros((n_exp, lanes), F32)
    for blk in range(seq // lanes):
        sl = slice(blk * lanes, (blk + 1) * lanes)
        chunk = self[:, sl]
        before = jnp.where(lane == blk, running, before)
        pos = _dot(chunk.astype(BF16), upper) + running
        slot_ref[:, sl] = jnp.where(chunk > 0.0, pos.astype(jnp.int32), -1)
        running = running + jnp.sum(chunk, axis=1, keepdims=True)
    gate_ref[...] = aff
    before_ref[...] = before.astype(jnp.int32)


def _route(logits_t, batch, seq, cap):
    n_exp = logits_t.shape[0]
    assert seq // 128 <= 128
    group = math.gcd(batch, ROUTE_GROUP)
    rows = group * n_exp
    spec = pl.BlockSpec((rows, seq), lambda i: (i, 0))
    slot, gate, before = pl.pallas_call(
        functools.partial(_topk_kernel, cap=cap, seq=seq, group=group),
        grid=(batch // group,),
        in_specs=[pl.BlockSpec((n_exp, group * seq), lambda i: (0, i))],
        out_specs=[spec, spec, pl.BlockSpec((rows, 128), lambda i: (i, 0))],
        out_shape=[jax.ShapeDtypeStruct((batch * n_exp, seq), jnp.int32),
                   jax.ShapeDtypeStruct((batch * n_exp, seq), F32),
                   jax.ShapeDtypeStruct((batch * n_exp, 128), jnp.int32)],
        compiler_params=_params(1),
        name="route_topk",
    )(logits_t)
    return (slot.reshape(batch, n_exp, seq), gate.reshape(batch, n_exp, seq), before.reshape(batch, n_exp, 128))


def _window_plan(before, cap, seq):
    per = MOE_CHUNK // 128
    lo = before[:, :, 0:seq // 128:per]
    hi = jnp.concatenate([lo[:, :, 1:], jnp.full(lo.shape[:2] + (1,), cap, jnp.int32)], axis=2)
    base = jnp.minimum(lo // 16 * 16, cap - MOE_WINDOW)
    overflow = jnp.any(hi - base > MOE_WINDOW, axis=1)
    return base.transpose(0, 2, 1).reshape(-1), overflow.astype(jnp.int32).reshape(-1)


def _gather_kernel(slot_ref, gate_ref, h_ref, xs_ref, gs_ref, *, n_inner, cap):
    h = h_ref[...]
    seq = h.shape[0]
    row = lax.broadcasted_iota(jnp.int32, (cap, seq), 0)
    for e in range(n_inner):
        onehot = slot_ref[e:e + 1, :] == row
        xs_ref[e] = _dot(jnp.where(onehot, 1.0, 0.0).astype(BF16), h).astype(BF16)
        gs_ref[e] = jnp.sum(jnp.where(onehot, gate_ref[e:e + 1, :], 0.0), axis=1, keepdims=True)


def _gather(slot, gate, h, batch, seq, cap, n_inner):
    n_exp = slot.shape[1]
    d = h.shape[1]
    n_outer = n_exp // n_inner
    slot4 = slot.reshape(batch, n_outer, n_inner, seq)
    gate4 = gate.reshape(batch, n_outer, n_inner, seq)
    sspec = pl.BlockSpec((None, None, n_inner, seq), lambda b, e: (b, e, 0, 0))
    return pl.pallas_call(
        functools.partial(_gather_kernel, n_inner=n_inner, cap=cap),
        grid=(batch, n_outer),
        in_specs=[sspec, sspec, pl.BlockSpec((seq, d), lambda b, e: (b, 0))],
        out_specs=[pl.BlockSpec((n_inner, cap, d), lambda b, e: (e, b, 0)),
                   pl.BlockSpec((n_inner, cap, 1), lambda b, e: (e, b, 0))],
        out_shape=[jax.ShapeDtypeStruct((n_exp, batch * cap, d), BF16),
                   jax.ShapeDtypeStruct((n_exp, batch * cap, 1), F32)],
        compiler_params=_params(2),
        name="moe_gather",
    )(slot4, gate4, h)


def _gather_win_kernel(base_ref, flag_ref, slot_ref, gate_ref, h_ref, xs_ref, gs_ref, *, cap):
    k = pl.program_id(1)
    step = pl.program_id(0) * pl.num_programs(1) + k
    n_exp, chunk = slot_ref.shape
    win = MOE_WINDOW

    @pl.when(k == 0)
    def _():
        xs_ref[...] = jnp.zeros(xs_ref.shape, xs_ref.dtype)
        gs_ref[...] = jnp.zeros(gs_ref.shape, gs_ref.dtype)

    h = h_ref[...]

    @pl.when(flag_ref[step] == 0)
    def _():
        row = lax.broadcasted_iota(jnp.int32, (win, chunk), 0)
        bases = [pl.multiple_of(base_ref[step * n_exp + e], 16) for e in range(n_exp)]
        hots = [(slot_ref[e:e + 1, :] - bases[e]) == row for e in range(n_exp)]
        stacked = jnp.concatenate([jnp.where(hot, 1.0, 0.0).astype(BF16) for hot in hots], axis=0)
        picked = _dot(stacked, h)
        for e in range(n_exp):
            rows = pl.ds(bases[e], win)
            xs_ref[e, rows, :] += picked[e * win:(e + 1) * win].astype(BF16)
            gs_ref[e, rows, :] += jnp.sum(jnp.where(hots[e], gate_ref[e:e + 1, :], 0.0), axis=1, keepdims=True)

    @pl.when(flag_ref[step] != 0)
    def _():
        row = lax.broadcasted_iota(jnp.int32, (cap, chunk), 0)
        for e in range(n_exp):
            hot = slot_ref[e:e + 1, :] == row
            xs_ref[e] += _dot(jnp.where(hot, 1.0, 0.0).astype(BF16), h).astype(BF16)
            gs_ref[e] += jnp.sum(jnp.where(hot, gate_ref[e:e + 1, :], 0.0), axis=1, keepdims=True)


def _gather_windowed(slot, gate, h, base, flag, batch, seq, cap):
    n_exp = slot.shape[1]
    d = h.shape[1]
    nk = seq // MOE_CHUNK
    sspec = pl.BlockSpec((None, n_exp, MOE_CHUNK), lambda b, k, *_: (b, 0, k))
    return pl.pallas_call(
        functools.partial(_gather_win_kernel, cap=cap),
        grid_spec=pltpu.PrefetchScalarGridSpec(
            num_scalar_prefetch=2,
            grid=(batch, nk),
            in_specs=[sspec, sspec, pl.BlockSpec((MOE_CHUNK, d), lambda b, k, *_: (b * nk + k, 0))],
            out_specs=[pl.BlockSpec((n_exp, cap, d), lambda b, k, *_: (0, b, 0)),
                       pl.BlockSpec((n_exp, cap, 1), lambda b, k, *_: (0, b, 0))]),
        out_shape=[jax.ShapeDtypeStruct((n_exp, batch * cap, d), BF16),
                   jax.ShapeDtypeStruct((n_exp, batch * cap, 1), F32)],
        compiler_params=_params(2),
        name="moe_gather_windowed",
    )(base, flag, slot, gate, h)


def _ffn_kernel(xc_ref, gc_ref, xl_ref, gl_ref, wg_hbm, wu_hbm, wd_hbm, y_ref, wg_s, wu_s, wd_s, stg_g, stg_u,
                stg_d, sems, *, layer, n_chunks, ctx_steps):
    e = pl.program_id(0)
    m = pl.program_id(1)
    slot = e % 2
    rows_gu = wg_s.shape[1] // n_chunks
    rows_d = wd_s.shape[1] // n_chunks

    def chunk_copies(expert, c):
        gu = pl.ds(pl.multiple_of(c * rows_gu, rows_gu), rows_gu)
        dn = pl.ds(pl.multiple_of(c * rows_d, rows_d), rows_d)
        return (pltpu.make_async_copy(wg_hbm.at[layer, expert, gu, :], stg_g, sems.at[0]),
                pltpu.make_async_copy(wu_hbm.at[layer, expert, gu, :], stg_u, sems.at[1]),
                pltpu.make_async_copy(wd_hbm.at[layer, expert, dn, :], stg_d, sems.at[2]))

    def land(dst_slot, c, copies):
        for cp in copies:
            cp.wait()
        gu = pl.ds(pl.multiple_of(c * rows_gu, rows_gu), rows_gu)
        dn = pl.ds(pl.multiple_of(c * rows_d, rows_d), rows_d)
        wg_s[dst_slot, gu, :] = stg_g[...].astype(BF16)
        wu_s[dst_slot, gu, :] = stg_u[...].astype(BF16)
        wd_s[dst_slot, dn, :] = stg_d[...].astype(BF16)

    @pl.when((e == 0) & (m == 0))
    def _():
        for c in range(n_chunks):
            copies = chunk_copies(0, c)
            for cp in copies:
                cp.start()
            land(0, c, copies)

    prefetch = (m < n_chunks) & (e + 1 < pl.num_programs(0))

    @pl.when(prefetch)
    def _():
        for cp in chunk_copies(e + 1, m):
            cp.start()

    def swiglu(x_ref, g_ref):
        x = x_ref[...]
        a = _dot(x, wg_s[slot])
        u = _dot(x, wu_s[slot])
        y = _dot((_silu(a) * u).astype(BF16), wd_s[slot])
        y_ref[...] = (y * g_ref[...]).astype(y_ref.dtype)

    @pl.when(m < ctx_steps)
    def _():
        swiglu(xc_ref, gc_ref)

    @pl.when(m >= ctx_steps)
    def _():
        swiglu(xl_ref, gl_ref)

    @pl.when(prefetch)
    def _():
        land(1 - slot, m, chunk_copies(e + 1, m))


def _expert_ffn(xs_c, gs_c, xs_l, gs_l, wg, wu, wd, layer):
    n_exp, rows_c, d = xs_c.shape
    rows_l = xs_l.shape[1]
    ff = wg.shape[3]
    tm = math.gcd(math.gcd(rows_c, rows_l), 512)
    ctx_steps = rows_c // tm
    steps = ctx_steps + rows_l // tm
    n_chunks = 4 if steps >= 4 else (2 if steps >= 2 else 1)
    any_spec = pl.BlockSpec(memory_space=pl.ANY)

    def ctx_spec(width):
        return pl.BlockSpec((None, tm, width), lambda e, i: (e, jnp.minimum(i, ctx_steps - 1), 0))

    def lat_spec(width):
        return pl.BlockSpec((None, tm, width), lambda e, i: (e, jnp.maximum(i - ctx_steps, 0), 0))

    return pl.pallas_call(
        functools.partial(_ffn_kernel, layer=layer, n_chunks=n_chunks, ctx_steps=ctx_steps),
        grid=(n_exp, steps),
        in_specs=[ctx_spec(d), ctx_spec(1), lat_spec(d), lat_spec(1), any_spec, any_spec, any_spec],
        out_specs=pl.BlockSpec((None, tm, d), lambda e, i: (e, i, 0)),
        out_shape=jax.ShapeDtypeStruct((n_exp, rows_c + rows_l, d), BF16),
        scratch_shapes=[pltpu.VMEM((2, d, ff), BF16), pltpu.VMEM((2, d, ff), BF16), pltpu.VMEM((2, ff, d), BF16),
                        pltpu.VMEM((d // n_chunks, ff), F32), pltpu.VMEM((d // n_chunks, ff), F32),
                        pltpu.VMEM((ff // n_chunks, d), F32), pltpu.SemaphoreType.DMA((3,))],
        compiler_params=_params(2),
        name="moe_ffn",
    )(xs_c, gs_c, xs_l, gs_l, wg, wu, wd)


def _combine_kernel(x_ref, y_ref, slot_ref, g2_ref, lg_ref, lb_ref, o_ref, *, alpha, cap):
    n_exp = y_ref.shape[0]
    tt = x_ref.shape[0]
    slot = slot_ref[...]
    lanes = 128
    if cap < lanes and lanes % cap == 0 and n_exp % (lanes // cap) == 0:
        per = lanes // cap
        lane = lax.broadcasted_iota(jnp.int32, (tt, lanes), 1)
        hots = []
        for g in range(n_exp // per):
            hit = None
            for j in range(per):
                s = slot[:, g * per + j:g * per + j + 1]
                match = jnp.where(s >= 0, s + j * cap, -1) == lane
                hit = match if hit is None else hit | match
            hots.append(jnp.where(hit, 1.0, 0.0).astype(BF16))
        acc = _dot(jnp.concatenate(hots, axis=1), y_ref[...].reshape(n_exp * cap, y_ref.shape[2]))
    else:
        col = lax.broadcasted_iota(jnp.int32, (tt, cap), 1)
        acc = jnp.zeros(x_ref.shape, F32)
        for e in range(n_exp):
            onehot = jnp.where(slot[:, e:e + 1] == col, 1.0, 0.0).astype(BF16)
            acc += _dot(onehot, y_ref[e])
    o_ref[...] = _ln(alpha * x_ref[...] + g2_ref[...] * acc) * lg_ref[...] + lb_ref[...]


def _combine(x1, y, slot_t, mod3, ln_g, ln_b, batch, seq, cap, row_blk0, per_request, alpha):
    rows, d = x1.shape
    n_exp = y.shape[0]
    tt = min(256, seq)
    nt = seq // tt
    vec = pl.BlockSpec((1, d), lambda b, i: (0, 0))
    return pl.pallas_call(
        functools.partial(_combine_kernel, alpha=alpha, cap=cap),
        grid=(batch, nt),
        in_specs=[pl.BlockSpec((tt, d), lambda b, i: (b * nt + i, 0)),
                  pl.BlockSpec((n_exp, cap, d), lambda b, i: (0, row_blk0 + b, 0)),
                  pl.BlockSpec((None, tt, n_exp), lambda b, i: (b, i, 0)),
                  pl.BlockSpec((None, 1, d), lambda b, i: (_mod_row(b, 1, per_request) * 6 + 5, 0, 0)),
                  vec, vec],
        out_specs=pl.BlockSpec((tt, d), lambda b, i: (b * nt + i, 0)),
        out_shape=jax.ShapeDtypeStruct((rows, d), F32),
        compiler_params=_params(2),
        name="moe_combine_postnorm",
    )(x1, y, slot_t, mod3, ln_g, ln_b)


def _combine_win_kernel(base_ref, flag_ref, x_ref, y_ref, slot_ref, g2_ref, lg_ref, lb_ref, o_ref, acc_ref,
                        *, alpha, cap):
    step = pl.program_id(0) * pl.num_programs(1) + pl.program_id(1)
    n_exp = y_ref.shape[0]
    tt = x_ref.shape[0]
    win = MOE_WINDOW
    slot = slot_ref[...]

    @pl.when(flag_ref[step] == 0)
    def _():
        lane = lax.broadcasted_iota(jnp.int32, (tt, 2 * win), 1)
        hots, ys = [], []
        for pair in range(n_exp // 2):
            picks = []
            for half in range(2):
                e = 2 * pair + half
                base = pl.multiple_of(base_ref[step * n_exp + e], 16)
                rel = slot[:, e:e + 1] - base
                picks.append(jnp.where((rel >= 0) & (rel < win), rel + half * win, -1))
                ys.append(y_ref[e, pl.ds(base, win), :])
            hots.append(jnp.where((picks[0] == lane) | (picks[1] == lane), 1.0, 0.0).astype(BF16))
        acc_ref[...] = _dot(jnp.concatenate(hots, axis=1), jnp.concatenate(ys, axis=0))

    @pl.when(flag_ref[step] != 0)
    def _():
        col = lax.broadcasted_iota(jnp.int32, (tt, cap), 1)
        acc = jnp.zeros(x_ref.shape, F32)
        for e in range(n_exp):
            acc += _dot(jnp.where(slot[:, e:e + 1] == col, 1.0, 0.0).astype(BF16), y_ref[e])
        acc_ref[...] = acc

    o_ref[...] = _ln(alpha * x_ref[...] + g2_ref[...] * acc_ref[...]) * lg_ref[...] + lb_ref[...]


def _combine_windowed(x1, y, slot_t, base, flag, mod3, ln_g, ln_b, batch, seq, cap, row_blk0, alpha):
    rows, d = x1.shape
    n_exp = y.shape[0]
    tt = MOE_CHUNK
    nt = seq // tt
    assert 2 * MOE_WINDOW == 128 and n_exp % 2 == 0
    vec = pl.BlockSpec((1, d), lambda b, i, *_: (0, 0))
    return pl.pallas_call(
        functools.partial(_combine_win_kernel, alpha=alpha, cap=cap),
        grid_spec=pltpu.PrefetchScalarGridSpec(
            num_scalar_prefetch=2,
            grid=(batch, nt),
            in_specs=[pl.BlockSpec((tt, d), lambda b, i, *_: (b * nt + i, 0)),
                      pl.BlockSpec((n_exp, cap, d), lambda b, i, *_: (0, row_blk0 + b, 0)),
                      pl.BlockSpec((None, tt, n_exp), lambda b, i, *_: (b, i, 0)),
                      pl.BlockSpec((None, 1, d), lambda b, i, *_: (_mod_row(b, 1, True) * 6 + 5, 0, 0)),
                      vec, vec],
            out_specs=pl.BlockSpec((tt, d), lambda b, i, *_: (b * nt + i, 0)),
            scratch_shapes=[pltpu.VMEM((tt, d), F32)]),
        out_shape=jax.ShapeDtypeStruct((rows, d), F32),
        compiler_params=_params(2),
        name="moe_combine_windowed_postnorm",
    )(base, flag, x1, y, slot_t, mod3, ln_g, ln_b)


def kernel(x_prompt, x_sample, cache_na_k, cache_na_v, cache_gqa_k, cache_gqa_v, state_ret_fwd, state_ret_bwd,
           c, c_ctx, w_in, w_out, w_mod, b_mod, ln1_g, ln1_b, ln2_g, ln2_b, q_norm_g, k_norm_g, na_rpb,
           ret_decay_fwd, ret_decay_bwd, w_router, w_gate, w_up, w_down):
    batch, seq, d = x_prompt.shape
    dbatch, dseq, _ = x_sample.shape
    depth = w_in.shape[0]
    na_heads = cache_na_k.shape[2]
    kv_heads = cache_gqa_k.shape[2]
    ret_heads = state_ret_fwd.shape[2]
    n_exp = w_router.shape[2]
    hd = HEAD_DIM
    gqa_heads = (w_in.shape[2] // hd - 3 * na_heads - 2 * kv_heads - 4 * ret_heads)
    group = gqa_heads // kv_heads
    alpha = float((2 * depth) ** 0.25)
    c_na = 0
    c_gq = 3 * na_heads
    c_gk = c_gq + gqa_heads
    c_gv = c_gk + kv_heads
    c_rt = c_gv + kv_heads
    cap_c = EC_CAPACITY_FACTOR * seq // n_exp
    cap_l = EC_CAPACITY_FACTOR * dseq // n_exp
    assert dbatch + 1 <= MOD_ROWS and (batch * cap_c) % cap_l == 0
    lat_blk0 = batch * cap_c // cap_l

    cond = jnp.zeros((MOD_ROWS, d), F32).at[0].set(c_ctx).at[1:1 + dbatch].set(c)
    mod_all = _modulation(cond, w_mod, b_mod)
    tables = _rope_tables(dseq)
    bias = _na_bias(na_rpb, dseq // GRID_W)

    w_in_b, w_out_b = w_in.astype(BF16), w_out.astype(BF16)

    xp = x_prompt.reshape(batch * seq, d)
    xs = x_sample.reshape(dbatch * dseq, d)
    outs = [[] for _ in range(6)]
    for l in range(depth):
        mod3 = mod_all[l].reshape(MOD_ROWS * 6, 1, d)
        wr_t = jnp.concatenate(_split_bf16(w_router[l]), axis=1)
        gains = (q_norm_g[l].reshape(1, hd), k_norm_g[l].reshape(1, hd))
        l1g, l1b = ln1_g[l].reshape(1, d), ln1_b[l].reshape(1, d)
        l2g, l2b = ln2_g[l].reshape(1, d), ln2_b[l].reshape(1, d)

        zc = _in_projection(xp, mod3, w_in_b, l, seq, False, F32)
        (na_c,) = _ctx_attention(zc, batch, seq, na_heads, 1, c_na, c_na + na_heads, c_na + 2 * na_heads)
        gq_c, gk_n = _ctx_attention(zc, batch, seq, kv_heads, group, c_gq, c_gk, c_gv, gains)
        rt_c, s_f, s_b = _retention(zc, batch, seq, ret_heads, c_rt, ret_decay_fwd[l], ret_decay_bwd[l],
                                    state_out=True)

        def heads_of(col, n):
            return zc[:, col * hd:(col + n) * hd].reshape(batch, seq, n, hd).transpose(0, 2, 1, 3)

        outs[0].append(heads_of(c_na + na_heads, na_heads))
        outs[1].append(heads_of(c_na + 2 * na_heads, na_heads))
        outs[2].append(gk_n)
        outs[3].append(heads_of(c_gv, kv_heads))
        outs[4].append(s_f)
        outs[5].append(s_b)

        zl = _in_projection(xs, mod3, w_in_b, l, dseq, True, BF16)
        (na_l,) = [_na_attention(zl, dbatch, dseq, na_heads, cache_na_k, cache_na_v, l, bias)]
        gq_l = _gqa_attention(zl, dbatch, dseq, kv_heads, group, c_gq, c_gk, c_gv, cache_gqa_k, cache_gqa_v, l,
                              gains, tables)
        (rt_l,) = _retention(zl, dbatch, dseq, ret_heads, c_rt, ret_decay_fwd[l], ret_decay_bwd[l],
                             states=((state_ret_fwd, state_ret_bwd), l))

        xp1, hp, lt_c = _out_projection(xp, na_c, gq_c, rt_c, w_out_b, l, mod3, l1g, l1b, wr_t, seq, False, alpha)
        xs1, hs, lt_l = _out_projection(xs, na_l, gq_l, rt_l, w_out_b, l, mod3, l1g, l1b, wr_t, dseq, True, alpha)

        slot_c, gate_c, _ = _route(lt_c.T, batch, seq, cap_c)
        slot_l, gate_l, before_l = _route(lt_l.T, dbatch, dseq, cap_l)
        windowed = cap_l > MOE_WINDOW and dseq % MOE_CHUNK == 0
        if windowed:
            base_l, flag_l = _window_plan(before_l, cap_l, dseq)
            rows_l, gates_l = _gather_windowed(slot_l, gate_l, hs, base_l, flag_l, dbatch, dseq, cap_l)
        else:
            rows_l, gates_l = _gather(slot_l, gate_l, hs, dbatch, dseq, cap_l, 2)
        rows_c, gates_c = _gather(slot_c, gate_c, hp, batch, seq, cap_c, n_exp)
        y = _expert_ffn(rows_c, gates_c, rows_l, gates_l, w_gate, w_up, w_down, l)
        xp = _combine(xp1, y, slot_c.transpose(0, 2, 1), mod3, l2g, l2b, batch, seq, cap_c, 0, False, alpha)
        if windowed:
            xs = _combine_windowed(xs1, y, slot_l.transpose(0, 2, 1), base_l, flag_l, mod3, l2g, l2b, dbatch, dseq,
                                   cap_l, lat_blk0, alpha)
        else:
            xs = _combine(xs1, y, slot_l.transpose(0, 2, 1), mod3, l2g, l2b, dbatch, dseq, cap_l, lat_blk0, True,
                          alpha)

    stacked = [jnp.stack(o, axis=1) for o in outs]
    return (xp.reshape(batch, seq, d), xs.reshape(dbatch, dseq, d), *stacked)
```

```python
import functools
import math

import numpy as np
import jax
import jax.numpy as jnp
from jax import lax
from jax.experimental import pallas as pl
from jax.experimental.pallas import tpu as pltpu

F32 = jnp.float32
BF16 = jnp.bfloat16

HEAD_DIM = 128
GRID_W = 64
NA_WIN_R = 8
NA_WIN_C = 16
ROPE_THETA = 10000.0
EC_CAPACITY_FACTOR = 2
LN_EPS = 1e-5
RMS_EPS = 1e-6
NEG_INF = -1e30
ATTN_SCALE = HEAD_DIM ** -0.5
LOG2E = 1.4426950408889634
NA_BLOCK_ROWS = 4
ROUTE_GROUP = 16
MOE_CHUNK = 256
MOE_WINDOW = 64
MOD_ROWS = 16
VMEM_LIMIT = 56 * 1024 * 1024

_NT = (((1,), (1,)), ((), ()))
_NN = (((1,), (0,)), ((), ()))


def _params(n_grid, vmem=VMEM_LIMIT):
    return pltpu.CompilerParams(dimension_semantics=("arbitrary",) * n_grid, vmem_limit_bytes=vmem)


def _dot(a, b, dims=_NN):
    return lax.dot_general(a, b, dims, preferred_element_type=F32)


def _split_bf16(x):
    hi = x.astype(BF16)
    lo = (x - hi.astype(F32)).astype(BF16)
    return hi, lo


def _silu(x):
    return x / (1.0 + jnp.exp(-x))


def _ln(x):
    mu = jnp.mean(x, axis=-1, keepdims=True)
    xc = x - mu
    var = jnp.mean(xc * xc, axis=-1, keepdims=True)
    return xc * lax.rsqrt(var + LN_EPS)


def _rms(x, g):
    return x * lax.rsqrt(jnp.mean(x * x, axis=-1, keepdims=True) + RMS_EPS) * g


def _rope(x, cos, sin_signed):
    lane = lax.broadcasted_iota(jnp.int32, x.shape, 1) & (HEAD_DIM // 2 - 1)
    partner = jnp.where(lane < HEAD_DIM // 4,
                        pltpu.roll(x, HEAD_DIM - HEAD_DIM // 4, 1),
                        pltpu.roll(x, HEAD_DIM // 4, 1))
    return x * cos + partner * sin_signed


def _softmax_pv(scores, values):
    m = functools.reduce(jnp.maximum, [jnp.max(s, axis=-1, keepdims=True) for s in scores])
    ps = [jnp.exp(s - m) for s in scores]
    denom = functools.reduce(lambda a, b: a + b, [jnp.sum(p, axis=-1, keepdims=True) for p in ps])
    o = functools.reduce(lambda a, b: a + b, [_dot(p.astype(BF16), v) for p, v in zip(ps, values)])
    return o / denom


def _mod_kernel(c_ref, w_ref, b_ref, o_ref):
    ah, al = _split_bf16(_silu(c_ref[...]))
    wh, wl = _split_bf16(w_ref[...])
    both = _dot(jnp.concatenate([ah, al], axis=0), wh)
    o_ref[...] = both[:MOD_ROWS] + both[MOD_ROWS:] + _dot(ah, wl) + b_ref[...]


def _modulation(cond, w_mod, b_mod):
    depth, d, n = w_mod.shape
    tn = 768
    return pl.pallas_call(
        _mod_kernel,
        grid=(depth, n // tn),
        in_specs=[pl.BlockSpec((MOD_ROWS, d), lambda l, j: (0, 0)),
                  pl.BlockSpec((None, d, tn), lambda l, j: (l, 0, j)),
                  pl.BlockSpec((None, 1, tn), lambda l, j: (l, 0, j))],
        out_specs=pl.BlockSpec((None, MOD_ROWS, tn), lambda l, j: (l, 0, j)),
        out_shape=jax.ShapeDtypeStruct((depth, MOD_ROWS, n), F32),
        compiler_params=_params(2),
        name="adaln_mod",
    )(cond, w_mod, b_mod.reshape(depth, 1, n))


def _inproj_kernel(x_ref, sh_ref, sc_ref, w_ref, z_ref, *, sub, tn):
    for r in range(x_ref.shape[0] // sub):
        rs = slice(r * sub, (r + 1) * sub)
        h = (_ln(x_ref[rs, :]) * (1.0 + sc_ref[...]) + sh_ref[...]).astype(BF16)
        for c in range(w_ref.shape[1] // tn):
            cs = slice(c * tn, (c + 1) * tn)
            z_ref[rs, cs] = _dot(h, w_ref[:, cs]).astype(z_ref.dtype)


def _mod_row(block, blocks_per_request, per_request):
    return 1 + block // blocks_per_request if per_request else 0


def _in_projection(x, mod3, w_bf16, layer, tokens_per_batch, per_request, out_dtype):
    rows, d = x.shape
    n = w_bf16.shape[2]
    tm = min(512 if out_dtype == BF16 else 256, tokens_per_batch)
    per = tokens_per_batch // tm

    def mod_spec(chunk):
        return pl.BlockSpec((None, 1, d), lambda i: (_mod_row(i, per, per_request) * 6 + chunk, 0, 0))

    return pl.pallas_call(
        functools.partial(_inproj_kernel, sub=min(256, tm), tn=n // 2),
        grid=(rows // tm,),
        in_specs=[pl.BlockSpec((tm, d), lambda i: (i, 0)),
                  mod_spec(0), mod_spec(1),
                  pl.BlockSpec((None, d, n), lambda i: (layer, 0, 0), pipeline_mode=pl.Buffered(1))],
        out_specs=pl.BlockSpec((tm, n), lambda i: (i, 0)),
        out_shape=jax.ShapeDtypeStruct((rows, n), out_dtype),
        compiler_params=_params(1),
        name="ln_inproj",
    )(x, mod3, mod3, w_bf16)


def _ctx_attn_kernel(*refs, group, normed):
    if normed:
        q_ref, k_ref, v_ref, qg_ref, kg_ref, o_ref, kn_ref = refs
    else:
        q_ref, k_ref, v_ref, o_ref = refs
    for hh in range(k_ref.shape[1] // HEAD_DIM):
        k = k_ref[:, hh * HEAD_DIM:(hh + 1) * HEAD_DIM]
        if normed:
            k = _rms(k, kg_ref[...])
            kn_ref[...] = k
        kb = k.astype(BF16)
        vb = v_ref[:, hh * HEAD_DIM:(hh + 1) * HEAD_DIM].astype(BF16)
        for g in range(group):
            cols = slice((hh * group + g) * HEAD_DIM, (hh * group + g + 1) * HEAD_DIM)
            q = q_ref[:, cols]
            if normed:
                q = _rms(q, qg_ref[...])
            s = _dot(q.astype(BF16), kb, _NT) * ATTN_SCALE
            o_ref[:, cols] = _softmax_pv([s], [vb]).astype(o_ref.dtype)


def _ctx_attention(z, batch, seq, heads, group, q_col, k_col, v_col, gains=None):
    normed = gains is not None
    hps = 1 if normed else heads
    assert q_col % (group * hps) == 0 and k_col % hps == 0 and v_col % hps == 0
    qw = group * hps * HEAD_DIM
    kw = hps * HEAD_DIM
    in_specs = [pl.BlockSpec((seq, qw), lambda b, h: (b, q_col // (group * hps) + h)),
                pl.BlockSpec((seq, kw), lambda b, h: (b, k_col // hps + h)),
                pl.BlockSpec((seq, kw), lambda b, h: (b, v_col // hps + h))]
    args = [z, z, z]
    out_specs = [pl.BlockSpec((seq, qw), lambda b, h: (b, h))]
    out_shape = [jax.ShapeDtypeStruct((batch * seq, heads * group * HEAD_DIM), BF16)]
    if normed:
        in_specs += [pl.BlockSpec((1, HEAD_DIM), lambda b, h: (0, 0))] * 2
        args += list(gains)
        out_specs.append(pl.BlockSpec((None, None, seq, HEAD_DIM), lambda b, h: (b, h, 0, 0)))
        out_shape.append(jax.ShapeDtypeStruct((batch, heads, seq, HEAD_DIM), F32))
    return pl.pallas_call(
        functools.partial(_ctx_attn_kernel, group=group, normed=normed),
        grid=(batch, heads // hps),
        in_specs=in_specs, out_specs=out_specs, out_shape=out_shape,
        compiler_params=_params(2),
        name="ctx_gqa" if normed else "ctx_na",
    )(*args)


def _log_sigmoid(x):
    return -jnp.log1p(jnp.exp(-x))


def _retention_kernel(*refs, seq, tq, with_state_in, with_state_out):
    q_ref, k_ref, v_ref, g_ref, df_ref, db_ref = refs[:6]
    refs = refs[6:]
    if with_state_in:
        sf_ref, sb_ref = refs[:2]
        refs = refs[2:]
    o_ref = refs[0]
    decay_ref = refs[-1]
    chunk = tq
    n_chunks = seq // chunk
    il = lax.broadcasted_iota(jnp.int32, (chunk, 1), 0).astype(F32)
    for hh in range(decay_ref.shape[0]):
        hs = slice(hh * HEAD_DIM, (hh + 1) * HEAD_DIM)
        lgf = _log_sigmoid(df_ref[hh])
        lgb = _log_sigmoid(db_ref[hh])

        @pl.when(pl.program_id(1) == 0)
        def _():
            i = lax.broadcasted_iota(jnp.int32, (chunk, chunk), 0)
            j = lax.broadcasted_iota(jnp.int32, (chunk, chunk), 1)
            dist = (i - j).astype(F32)
            decay = jnp.where(dist == 0.0, 2.0, jnp.exp(jnp.where(dist > 0.0, lgf * dist, -lgb * dist)))
            decay_ref[hh] = decay * ATTN_SCALE

        k_dec_f = jnp.exp(lgf * (chunk - 1.0 - il)) * ATTN_SCALE
        k_dec_b = jnp.exp(lgb * il) * ATTN_SCALE
        add_f, add_b = [], []
        for c in range(n_chunks):
            cs = slice(c * chunk, (c + 1) * chunk)
            kf = k_ref[cs, hs].astype(F32)
            vb = v_ref[cs, hs].astype(BF16)
            add_f.append(_dot((kf * k_dec_f).T.astype(BF16), vb))
            add_b.append(_dot((kf * k_dec_b).T.astype(BF16), vb))
        state_f, state_b = [None] * n_chunks, [None] * n_chunks
        carry_f = jnp.exp(lgf * chunk)
        carry_b = jnp.exp(lgb * chunk)
        s = sf_ref[hh] if with_state_in else None
        for c in range(n_chunks):
            state_f[c] = s
            s = add_f[c] if s is None else carry_f * s + add_f[c]
        final_f = s
        s = sb_ref[hh] if with_state_in else None
        for c in reversed(range(n_chunks)):
            state_b[c] = s
            s = add_b[c] if s is None else carry_b * s + add_b[c]
        final_b = s

        q_dec_f = jnp.exp(lgf * (il + 1.0))
        q_dec_b = jnp.exp(lgb * (chunk - il))
        for c in range(n_chunks):
            cs = slice(c * chunk, (c + 1) * chunk)
            q = q_ref[cs, hs]
            att = _dot(q.astype(BF16), k_ref[cs, hs].astype(BF16), _NT) * decay_ref[hh]
            o = _dot(att.astype(BF16), v_ref[cs, hs].astype(BF16))
            qf = q.astype(F32)
            if state_f[c] is not None:
                o += _dot((qf * q_dec_f).astype(BF16), state_f[c].astype(BF16))
            if state_b[c] is not None:
                o += _dot((qf * q_dec_b).astype(BF16), state_b[c].astype(BF16))
            o_ref[cs, hs] = (_ln(o) * _silu(g_ref[cs, hs].astype(F32))).astype(o_ref.dtype)
        if with_state_out:
            nsf_ref, nsb_ref = refs[1:3]
            nsf_ref[hh] = final_f
            nsb_ref[hh] = final_b


def _retention(z, batch, seq, heads, col0, dec_f, dec_b, states=None, state_out=False):
    chunk = min(512, seq)
    hps = heads if seq == chunk else 1
    assert col0 % hps == 0 and seq % chunk == 0
    w = hps * HEAD_DIM
    c0 = col0 // hps
    nh = heads // hps
    in_specs = [pl.BlockSpec((seq, w), lambda h, b: (b, c0 + h)),
                pl.BlockSpec((seq, w), lambda h, b: (b, c0 + nh + h)),
                pl.BlockSpec((seq, w), lambda h, b: (b, c0 + 2 * nh + h)),
                pl.BlockSpec((seq, w), lambda h, b: (b, c0 + 3 * nh + h)),
                pl.BlockSpec((hps, 1, 1), lambda h, b: (h, 0, 0)),
                pl.BlockSpec((hps, 1, 1), lambda h, b: (h, 0, 0))]
    args = [z, z, z, z, dec_f.reshape(heads, 1, 1), dec_b.reshape(heads, 1, 1)]
    if states is not None:
        (sf, sb), layer = states
        spec = pl.BlockSpec((None, None, hps, HEAD_DIM, HEAD_DIM), lambda h, b: (b, layer, h, 0, 0))
        in_specs += [spec, spec]
        args += [sf, sb]
    out_specs = [pl.BlockSpec((seq, w), lambda h, b: (b, h))]
    out_shape = [jax.ShapeDtypeStruct((batch * seq, heads * HEAD_DIM), BF16)]
    if state_out:
        spec = pl.BlockSpec((None, hps, HEAD_DIM, HEAD_DIM), lambda h, b: (b, h, 0, 0))
        out_specs += [spec, spec]
        out_shape += [jax.ShapeDtypeStruct((batch, heads, HEAD_DIM, HEAD_DIM), F32)] * 2
    return pl.pallas_call(
        functools.partial(_retention_kernel, seq=seq, tq=chunk, with_state_in=states is not None,
                          with_state_out=state_out),
        grid=(nh, batch),
        in_specs=in_specs, out_specs=out_specs, out_shape=out_shape,
        scratch_shapes=[pltpu.VMEM((hps, chunk, chunk), F32)],
        compiler_params=_params(2),
        name="retention",
    )(*args)


def _with_ones(v):
    return jnp.concatenate([v, jnp.ones(v.shape, v.dtype)], axis=1)


def _exp2_pv(scores, values_with_ones):
    m = functools.reduce(jnp.maximum, [jnp.max(s, axis=-1, keepdims=True) for s in scores])
    r = functools.reduce(lambda a, b: a + b,
                         [_dot(jnp.exp2(s - m).astype(BF16), v) for s, v in zip(scores, values_with_ones)])
    return r[:, :HEAD_DIM] / r[:, HEAD_DIM:HEAD_DIM + 1]


def _na_plan(rows):
    kr = min(NA_WIN_R, rows)
    nq = min(NA_BLOCK_ROWS, rows)
    nk = min(nq + kr - 1, rows)
    assert rows % nq == 0
    blocks, cases = [], []
    for r0 in range(0, rows, nq):
        start = int(np.clip(r0 - kr // 2, 0, rows - nk))
        win = [int(np.clip(r - kr // 2, 0, rows - kr)) for r in range(r0, r0 + nq)]
        assert all(start <= w and w + kr <= start + nk for w in win)
        key = (start - r0,) + tuple(w - start for w in win)
        if key not in cases:
            cases.append(key)
        blocks.append((r0, start, cases.index(key)))
    return kr, nq, nk, blocks, cases


def _na_kernel(q_ref, k_ref, v_ref, ck_ref, cv_ref, bias_ref, o_ref, vx_ref, cvx_ref, *, blocks, nq, nk):
    vx_ref[...] = _with_ones(v_ref[...])
    cvx_ref[...] = _with_ones(cv_ref[...].astype(BF16))
    ck = ck_ref[...].astype(BF16)
    c = ATTN_SCALE * LOG2E
    for r0, start, case in blocks:
        qs = slice(r0 * GRID_W, (r0 + nq) * GRID_W)
        ks = slice(start * GRID_W, (start + nk) * GRID_W)
        q = q_ref[qs, :]
        s_loc = _dot(q, k_ref[ks, :], _NT) * c + bias_ref[case]
        s_ctx = _dot(q, ck, _NT) * c
        o_ref[qs, :] = _exp2_pv([s_loc, s_ctx], [vx_ref[ks, :], cvx_ref[...]]).astype(o_ref.dtype)


def _na_bias(rpb, rows):
    kr, nq, nk, _, cases = _na_plan(rows)
    dr = np.zeros((len(cases), nq, nk), np.int32)
    row_ok = np.zeros((len(cases), nq, nk), bool)
    for c, key in enumerate(cases):
        rel, offs = key[0], key[1:]
        for ri in range(nq):
            for ju in range(nk):
                row_ok[c, ri, ju] = 0 <= ju - offs[ri] < kr
                dr[c, ri, ju] = np.clip(rel + ju - ri + NA_WIN_R - 1, 0, 2 * NA_WIN_R - 2)
    cq = np.arange(GRID_W)
    ck = np.arange(GRID_W)
    col_start = np.clip(cq - NA_WIN_C // 2, 0, GRID_W - NA_WIN_C)
    col_ok = (ck[None, :] >= col_start[:, None]) & (ck[None, :] < col_start[:, None] + NA_WIN_C)
    dc = np.clip(ck[None, :] - cq[:, None] + (NA_WIN_C - 1), 0, 2 * NA_WIN_C - 2)
    pick_col = (dc.reshape(-1)[None, :] == np.arange(2 * NA_WIN_C - 1)[:, None]).astype(np.float32)
    n_l, n_h = rpb.shape[:2]
    t = jnp.take(rpb.astype(F32), dr.reshape(-1), axis=2)
    t = jnp.einsum("lhxb,bn->lhxn", t, pick_col, precision=lax.Precision.HIGHEST)
    t = t.reshape(n_l, n_h, len(cases), nq, nk, GRID_W, GRID_W).transpose(0, 1, 2, 3, 5, 4, 6)
    ok = row_ok[:, :, None, :, None] & col_ok[None, None, :, None, :]
    t = jnp.where(ok, t * LOG2E, NEG_INF)
    return t.reshape(n_l, n_h, len(cases), nq * GRID_W, nk * GRID_W)


def _na_attention(z, batch, seq, heads, cache_k, cache_v, layer, bias):
    _, nq, nk, blocks, cases = _na_plan(seq // GRID_W)
    past = cache_k.shape[3]
    cspec = pl.BlockSpec((None, None, None, past, HEAD_DIM), lambda b, h: (b, layer, h, 0, 0))
    return pl.pallas_call(
        functools.partial(_na_kernel, blocks=blocks, nq=nq, nk=nk),
        grid=(batch, heads),
        in_specs=[pl.BlockSpec((seq, HEAD_DIM), lambda b, h: (b, h)),
                  pl.BlockSpec((seq, HEAD_DIM), lambda b, h: (b, heads + h)),
                  pl.BlockSpec((seq, HEAD_DIM), lambda b, h: (b, 2 * heads + h)),
                  cspec, cspec,
                  pl.BlockSpec((None, None, len(cases), nq * GRID_W, nk * GRID_W),
                               lambda b, h: (layer, h, 0, 0, 0))],
        out_specs=pl.BlockSpec((seq, HEAD_DIM), lambda b, h: (b, h)),
        out_shape=jax.ShapeDtypeStruct((batch * seq, heads * HEAD_DIM), BF16),
        scratch_shapes=[pltpu.VMEM((seq, 2 * HEAD_DIM), BF16), pltpu.VMEM((past, 2 * HEAD_DIM), BF16)],
        compiler_params=_params(2),
        name="latent_na",
    )(z, z, z, cache_k, cache_v, bias)


def _gqa_kernel(q_ref, k_ref, v_ref, ck_ref, cv_ref, qg_ref, kg_ref, cos_ref, sin_ref, o_ref, kf_ref, vf_ref,
                *, group, past, seq, tq):
    i = pl.program_id(2)

    @pl.when(i == 0)
    def _():
        kf_ref[0:past, :] = ck_ref[...].astype(BF16)
        vf_ref[0:past, :] = _with_ones(cv_ref[...].astype(BF16))
        k = _rope(_rms(k_ref[...].astype(F32), kg_ref[...]), cos_ref[...], sin_ref[...])
        kf_ref[past:past + seq, :] = k.astype(BF16)
        vf_ref[past:past + seq, :] = _with_ones(v_ref[...])

    t0 = pl.multiple_of(i * tq, tq)
    cos = cos_ref[pl.ds(t0, tq), :]
    sin = sin_ref[pl.ds(t0, tq), :]
    kf = kf_ref[...]
    vf = vf_ref[...]
    def scores(g):
        q = q_ref[:, g * HEAD_DIM:(g + 1) * HEAD_DIM].astype(F32)
        q = _rope(_rms(q, qg_ref[...]), cos, sin) * (ATTN_SCALE * LOG2E)
        return _dot(q.astype(BF16), kf, _NT)

    s = scores(0)
    for g in range(group):
        s_next = scores(g + 1) if g + 1 < group else None
        o_ref[:, g * HEAD_DIM:(g + 1) * HEAD_DIM] = _exp2_pv([s], [vf]).astype(o_ref.dtype)
        s = s_next


def _rope_tables(seq):
    t = np.arange(seq)
    half = HEAD_DIM // 2
    inv = 1.0 / (ROPE_THETA ** (np.arange(0, half, 2, dtype=np.float32) / half))
    row = (t // GRID_W).astype(np.float32)
    col = (t % GRID_W).astype(np.float32)
    ang = jnp.concatenate([jnp.asarray(row[:, None] * inv)] * 2 + [jnp.asarray(col[:, None] * inv)] * 2, axis=-1)
    sign = np.where((np.arange(HEAD_DIM) % half) < half // 2, -1.0, 1.0).astype(np.float32)
    return jnp.cos(ang), jnp.sin(ang) * sign


def _gqa_attention(z, batch, seq, kv_heads, group, q_col, k_col, v_col, cache_k, cache_v, layer, gains, tables):
    past = cache_k.shape[3]
    tq = min(512, seq)
    nq = seq // tq
    qw = group * HEAD_DIM
    cspec = pl.BlockSpec((None, None, None, past, HEAD_DIM), lambda b, h, i: (b, layer, h, 0, 0))
    gspec = pl.BlockSpec((1, HEAD_DIM), lambda b, h, i: (0, 0))
    tspec = pl.BlockSpec((seq, HEAD_DIM), lambda b, h, i: (0, 0))
    return pl.pallas_call(
        functools.partial(_gqa_kernel, group=group, past=past, seq=seq, tq=tq),
        grid=(batch, kv_heads, nq),
        in_specs=[pl.BlockSpec((tq, qw), lambda b, h, i: (b * nq + i, q_col // group + h)),
                  pl.BlockSpec((seq, HEAD_DIM), lambda b, h, i: (b, k_col + h)),
                  pl.BlockSpec((seq, HEAD_DIM), lambda b, h, i: (b, v_col + h)),
                  cspec, cspec, gspec, gspec, tspec, tspec],
        out_specs=pl.BlockSpec((tq, qw), lambda b, h, i: (b * nq + i, h)),
        out_shape=jax.ShapeDtypeStruct((batch * seq, kv_heads * qw), BF16),
        scratch_shapes=[pltpu.VMEM((past + seq, HEAD_DIM), BF16), pltpu.VMEM((past + seq, 2 * HEAD_DIM), BF16)],
        compiler_params=_params(3),
        name="latent_gqa",
    )(z, z, z, cache_k, cache_v, gains[0], gains[1], tables[0], tables[1])


def _outproj_kernel(x_ref, na_ref, gq_ref, rt_ref, w_ref, g1_ref, sh2_ref, sc2_ref, lg_ref, lb_ref, wr_ref,
                    x1_ref, h2_ref, lt_ref, *, alpha, sub):
    n_exp = lt_ref.shape[1]
    for r in range(x_ref.shape[0] // sub):
        rs = slice(r * sub, (r + 1) * sub)
        mix = jnp.concatenate([na_ref[rs, :], gq_ref[rs, :], rt_ref[rs, :]], axis=1)
        y = _dot(mix, w_ref[...])
        x1 = _ln(alpha * x_ref[rs, :] + g1_ref[...] * y) * lg_ref[...] + lb_ref[...]
        x1_ref[rs, :] = x1
        h2 = _ln(x1) * (1.0 + sc2_ref[...]) + sh2_ref[...]
        hi, lo = _split_bf16(h2)
        h2_ref[rs, :] = hi
        both = _dot(hi, wr_ref[...])
        lt_ref[rs, :] = both[:, :n_exp] + both[:, n_exp:] + _dot(lo, wr_ref[:, 0:n_exp])


def _out_projection(x, na_o, gq_o, rt_o, w_bf16, layer, mod3, ln_g, ln_b, wr_hl, tokens_per_batch, per_request,
                    alpha):
    rows, d = x.shape
    tm = 512
    per = tokens_per_batch // tm if per_request else 1
    n_exp = wr_hl.shape[1] // 2

    def mod_spec(chunk):
        return pl.BlockSpec((None, 1, d), lambda i: (_mod_row(i, per, per_request) * 6 + chunk, 0, 0))

    def row_spec(width):
        return pl.BlockSpec((tm, width), lambda i: (i, 0))

    vec = pl.BlockSpec((1, d), lambda i: (0, 0))
    return pl.pallas_call(
        functools.partial(_outproj_kernel, alpha=alpha, sub=256),
        grid=(rows // tm,),
        in_specs=[row_spec(d), row_spec(na_o.shape[1]), row_spec(gq_o.shape[1]), row_spec(rt_o.shape[1]),
                  pl.BlockSpec((None,) + w_bf16.shape[1:], lambda i: (layer, 0, 0)),
                  mod_spec(2), mod_spec(3), mod_spec(4), vec, vec,
                  pl.BlockSpec(wr_hl.shape, lambda i: (0, 0))],
        out_specs=[row_spec(d), row_spec(d), row_spec(n_exp)],
        out_shape=[jax.ShapeDtypeStruct((rows, d), F32), jax.ShapeDtypeStruct((rows, d), BF16),
                   jax.ShapeDtypeStruct((rows, n_exp), F32)],
        compiler_params=_params(1),
        name="outproj_postnorm_router",
    )(x, na_o, gq_o, rt_o, w_bf16, mod3, mod3, mod3, ln_g, ln_b, wr_hl)


def _topk_kernel(lt_ref, slot_ref, gate_ref, before_ref, *, cap, seq, group):
    affs = []
    for g in range(group):
        logits = lt_ref[:, g * seq:(g + 1) * seq]
        m = jnp.max(logits, axis=0, keepdims=True)
        ex = jnp.exp(logits - m)
        affs.append(ex / jnp.sum(ex, axis=0, keepdims=True))
    aff = jnp.concatenate(affs, axis=0)
    n_exp = aff.shape[0]
    bits = lax.bitcast_convert_type(aff, jnp.int32)

    def count(mask):
        return jnp.sum(jnp.where(mask, 1.0, 0.0), axis=1, keepdims=True)

    def value_step(it, thr):
        cand = thr | jnp.left_shift(jnp.int32(1), 30 - it)
        return jnp.where(count(bits >= cand) >= cap, cand, thr)

    thr = lax.fori_loop(0, 31, value_step, jnp.zeros((n_exp, 1), jnp.int32))
    above = bits > thr
    tied = bits == thr
    need = cap - count(above)
    tok = lax.broadcasted_iota(jnp.int32, (n_exp, seq), 1)
    n_bits = int(seq - 1).bit_length()

    def index_step(it, bound):
        cand = bound | jnp.left_shift(jnp.int32(1), n_bits - 1 - it)
        return jnp.where(count(tied & (tok < cand)) < need, cand, bound)

    bound = lax.fori_loop(0, n_bits, index_step, jnp.zeros((n_exp, 1), jnp.int32))
    sel = above | (tied & (tok <= bound))
    self = jnp.where(sel, 1.0, 0.0)
    lanes = 128
    upper = jnp.where(lax.broadcasted_iota(jnp.int32, (lanes, lanes), 0)
                      < lax.broadcasted_iota(jnp.int32, (lanes, lanes), 1), 1.0, 0.0).astype(BF16)
    running = jnp.zeros((n_exp, 1), F32)
    lane = lax.broadcasted_iota(jnp.int32, (n_exp, lanes), 1)
    before = jnp.zeros((n_exp, lanes), F32)
    for blk in range(seq // lanes):
        sl = slice(blk * lanes, (blk + 1) * lanes)
        chunk = self[:, sl]
        before = jnp.where(lane == blk, running, before)
        pos = _dot(chunk.astype(BF16), upper) + running
        slot_ref[:, sl] = jnp.where(chunk > 0.0, pos.astype(jnp.int32), -1)
        running = running + jnp.sum(chunk, axis=1, keepdims=True)
    gate_ref[...] = aff
    before_ref[...] = before.astype(jnp.int32)


def _route(logits_t, batch, seq, cap):
    n_exp = logits_t.shape[0]
    assert seq // 128 <= 128
    group = math.gcd(batch, ROUTE_GROUP)
    rows = group * n_exp
    spec = pl.BlockSpec((rows, seq), lambda i: (i, 0))
    slot, gate, before = pl.pallas_call(
        functools.partial(_topk_kernel, cap=cap, seq=seq, group=group),
        grid=(batch // group,),
        in_specs=[pl.BlockSpec((n_exp, group * seq), lambda i: (0, i))],
        out_specs=[spec, spec, pl.BlockSpec((rows, 128), lambda i: (i, 0))],
        out_shape=[jax.ShapeDtypeStruct((batch * n_exp, seq), jnp.int32),
                   jax.ShapeDtypeStruct((batch * n_exp, seq), F32),
                   jax.ShapeDtypeStruct((batch * n_exp, 128), jnp.int32)],
        compiler_params=_params(1),
        name="route_topk",
    )(logits_t)
    return (slot.reshape(batch, n_exp, seq), gate.reshape(batch, n_exp, seq), before.reshape(batch, n_exp, 128))


def _window_plan(before, cap, seq):
    per = MOE_CHUNK // 128
    lo = before[:, :, 0:seq // 128:per]
    hi = jnp.concatenate([lo[:, :, 1:], jnp.full(lo.shape[:2] + (1,), cap, jnp.int32)], axis=2)
    base = jnp.minimum(lo // 16 * 16, cap - MOE_WINDOW)
    overflow = jnp.any(hi - base > MOE_WINDOW, axis=1)
    return base.transpose(0, 2, 1).reshape(-1), overflow.astype(jnp.int32).reshape(-1)


def _gather_kernel(slot_ref, gate_ref, h_ref, xs_ref, gs_ref, *, n_inner, cap):
    h = h_ref[...]
    seq = h.shape[0]
    row = lax.broadcasted_iota(jnp.int32, (cap, seq), 0)
    for e in range(n_inner):
        onehot = slot_ref[e:e + 1, :] == row
        xs_ref[e] = _dot(jnp.where(onehot, 1.0, 0.0).astype(BF16), h).astype(BF16)
        gs_ref[e] = jnp.sum(jnp.where(onehot, gate_ref[e:e + 1, :], 0.0), axis=1, keepdims=True)


def _gather(slot, gate, h, batch, seq, cap, n_inner):
    n_exp = slot.shape[1]
    d = h.shape[1]
    n_outer = n_exp // n_inner
    slot4 = slot.reshape(batch, n_outer, n_inner, seq)
    gate4 = gate.reshape(batch, n_outer, n_inner, seq)
    sspec = pl.BlockSpec((None, None, n_inner, seq), lambda b, e: (b, e, 0, 0))
    return pl.pallas_call(
        functools.partial(_gather_kernel, n_inner=n_inner, cap=cap),
        grid=(batch, n_outer),
        in_specs=[sspec, sspec, pl.BlockSpec((seq, d), lambda b, e: (b, 0))],
        out_specs=[pl.BlockSpec((n_inner, cap, d), lambda b, e: (e, b, 0)),
                   pl.BlockSpec((n_inner, cap, 1), lambda b, e: (e, b, 0))],
        out_shape=[jax.ShapeDtypeStruct((n_exp, batch * cap, d), BF16),
                   jax.ShapeDtypeStruct((n_exp, batch * cap, 1), F32)],
        compiler_params=_params(2),
        name="moe_gather",
    )(slot4, gate4, h)


def _gather_win_kernel(base_ref, flag_ref, slot_ref, gate_ref, h_ref, xs_ref, gs_ref, *, cap):
    k = pl.program_id(1)
    step = pl.program_id(0) * pl.num_programs(1) + k
    n_exp, chunk = slot_ref.shape
    win = MOE_WINDOW

    @pl.when(k == 0)
    def _():
        xs_ref[...] = jnp.zeros(xs_ref.shape, xs_ref.dtype)
        gs_ref[...] = jnp.zeros(gs_ref.shape, gs_ref.dtype)

    h = h_ref[...]

    @pl.when(flag_ref[step] == 0)
    def _():
        row = lax.broadcasted_iota(jnp.int32, (win, chunk), 0)
        bases = [pl.multiple_of(base_ref[step * n_exp + e], 16) for e in range(n_exp)]
        hots = [(slot_ref[e:e + 1, :] - bases[e]) == row for e in range(n_exp)]
        stacked = jnp.concatenate([jnp.where(hot, 1.0, 0.0).astype(BF16) for hot in hots], axis=0)
        picked = _dot(stacked, h)
        for e in range(n_exp):
            rows = pl.ds(bases[e], win)
            xs_ref[e, rows, :] += picked[e * win:(e + 1) * win].astype(BF16)
            gs_ref[e, rows, :] += jnp.sum(jnp.where(hots[e], gate_ref[e:e + 1, :], 0.0), axis=1, keepdims=True)

    @pl.when(flag_ref[step] != 0)
    def _():
        row = lax.broadcasted_iota(jnp.int32, (cap, chunk), 0)
        for e in range(n_exp):
            hot = slot_ref[e:e + 1, :] == row
            xs_ref[e] += _dot(jnp.where(hot, 1.0, 0.0).astype(BF16), h).astype(BF16)
            gs_ref[e] += jnp.sum(jnp.where(hot, gate_ref[e:e + 1, :], 0.0), axis=1, keepdims=True)


def _gather_windowed(slot, gate, h, base, flag, batch, seq, cap):
    n_exp = slot.shape[1]
    d = h.shape[1]
    nk = seq // MOE_CHUNK
    sspec = pl.BlockSpec((None, n_exp, MOE_CHUNK), lambda b, k, *_: (b, 0, k))
    return pl.pallas_call(
        functools.partial(_gather_win_kernel, cap=cap),
        grid_spec=pltpu.PrefetchScalarGridSpec(
            num_scalar_prefetch=2,
            grid=(batch, nk),
            in_specs=[sspec, sspec, pl.BlockSpec((MOE_CHUNK, d), lambda b, k, *_: (b * nk + k, 0))],
            out_specs=[pl.BlockSpec((n_exp, cap, d), lambda b, k, *_: (0, b, 0)),
                       pl.BlockSpec((n_exp, cap, 1), lambda b, k, *_: (0, b, 0))]),
        out_shape=[jax.ShapeDtypeStruct((n_exp, batch * cap, d), BF16),
                   jax.ShapeDtypeStruct((n_exp, batch * cap, 1), F32)],
        compiler_params=_params(2),
        name="moe_gather_windowed",
    )(base, flag, slot, gate, h)


def _ffn_kernel(xc_ref, gc_ref, xl_ref, gl_ref, wg_hbm, wu_hbm, wd_hbm, y_ref, wg_s, wu_s, wd_s, stg_g, stg_u,
                stg_d, sems, *, layer, n_chunks, ctx_steps):
    e = pl.program_id(0)
    m = pl.program_id(1)
    slot = e % 2
    rows_gu = wg_s.shape[1] // n_chunks
    rows_d = wd_s.shape[1] // n_chunks

    def chunk_copies(expert, c):
        gu = pl.ds(pl.multiple_of(c * rows_gu, rows_gu), rows_gu)
        dn = pl.ds(pl.multiple_of(c * rows_d, rows_d), rows_d)
        return (pltpu.make_async_copy(wg_hbm.at[layer, expert, gu, :], stg_g, sems.at[0]),
                pltpu.make_async_copy(wu_hbm.at[layer, expert, gu, :], stg_u, sems.at[1]),
                pltpu.make_async_copy(wd_hbm.at[layer, expert, dn, :], stg_d, sems.at[2]))

    def land(dst_slot, c, copies):
        for cp in copies:
            cp.wait()
        gu = pl.ds(pl.multiple_of(c * rows_gu, rows_gu), rows_gu)
        dn = pl.ds(pl.multiple_of(c * rows_d, rows_d), rows_d)
        wg_s[dst_slot, gu, :] = stg_g[...].astype(BF16)
        wu_s[dst_slot, gu, :] = stg_u[...].astype(BF16)
        wd_s[dst_slot, dn, :] = stg_d[...].astype(BF16)

    @pl.when((e == 0) & (m == 0))
    def _():
        for c in range(n_chunks):
            copies = chunk_copies(0, c)
            for cp in copies:
                cp.start()
            land(0, c, copies)

    prefetch = (m < n_chunks) & (e + 1 < pl.num_programs(0))

    @pl.when(prefetch)
    def _():
        for cp in chunk_copies(e + 1, m):
            cp.start()

    def swiglu(x_ref, g_ref):
        x = x_ref[...]
        a = _dot(x, wg_s[slot])
        u = _dot(x, wu_s[slot])
        y = _dot((_silu(a) * u).astype(BF16), wd_s[slot])
        y_ref[...] = (y * g_ref[...]).astype(y_ref.dtype)

    @pl.when(m < ctx_steps)
    def _():
        swiglu(xc_ref, gc_ref)

    @pl.when(m >= ctx_steps)
    def _():
        swiglu(xl_ref, gl_ref)

    @pl.when(prefetch)
    def _():
        land(1 - slot, m, chunk_copies(e + 1, m))


def _expert_ffn(xs_c, gs_c, xs_l, gs_l, wg, wu, wd, layer):
    n_exp, rows_c, d = xs_c.shape
    rows_l = xs_l.shape[1]
    ff = wg.shape[3]
    tm = math.gcd(math.gcd(rows_c, rows_l), 512)
    ctx_steps = rows_c // tm
    steps = ctx_steps + rows_l // tm
    n_chunks = 4 if steps >= 4 else (2 if steps >= 2 else 1)
    any_spec = pl.BlockSpec(memory_space=pl.ANY)

    def ctx_spec(width):
        return pl.BlockSpec((None, tm, width), lambda e, i: (e, jnp.minimum(i, ctx_steps - 1), 0))

    def lat_spec(width):
        return pl.BlockSpec((None, tm, width), lambda e, i: (e, jnp.maximum(i - ctx_steps, 0), 0))

    return pl.pallas_call(
        functools.partial(_ffn_kernel, layer=layer, n_chunks=n_chunks, ctx_steps=ctx_steps),
        grid=(n_exp, steps),
        in_specs=[ctx_spec(d), ctx_spec(1), lat_spec(d), lat_spec(1), any_spec, any_spec, any_spec],
        out_specs=pl.BlockSpec((None, tm, d), lambda e, i: (e, i, 0)),
        out_shape=jax.ShapeDtypeStruct((n_exp, rows_c + rows_l, d), BF16),
        scratch_shapes=[pltpu.VMEM((2, d, ff), BF16), pltpu.VMEM((2, d, ff), BF16), pltpu.VMEM((2, ff, d), BF16),
                        pltpu.VMEM((d // n_chunks, ff), F32), pltpu.VMEM((d // n_chunks, ff), F32),
                        pltpu.VMEM((ff // n_chunks, d), F32), pltpu.SemaphoreType.DMA((3,))],
        compiler_params=_params(2),
        name="moe_ffn",
    )(xs_c, gs_c, xs_l, gs_l, wg, wu, wd)


def _combine_kernel(x_ref, y_ref, slot_ref, g2_ref, lg_ref, lb_ref, o_ref, *, alpha, cap):
    n_exp = y_ref.shape[0]
    tt = x_ref.shape[0]
    slot = slot_ref[...]
    lanes = 128
    if cap < lanes and lanes % cap == 0 and n_exp % (lanes // cap) == 0:
        per = lanes // cap
        lane = lax.broadcasted_iota(jnp.int32, (tt, lanes), 1)
        hots = []
        for g in range(n_exp // per):
            hit = None
            for j in range(per):
                s = slot[:, g * per + j:g * per + j + 1]
                match = jnp.where(s >= 0, s + j * cap, -1) == lane
                hit = match if hit is None else hit | match
            hots.append(jnp.where(hit, 1.0, 0.0).astype(BF16))
        acc = _dot(jnp.concatenate(hots, axis=1), y_ref[...].reshape(n_exp * cap, y_ref.shape[2]))
    else:
        col = lax.broadcasted_iota(jnp.int32, (tt, cap), 1)
        acc = jnp.zeros(x_ref.shape, F32)
        for e in range(n_exp):
            onehot = jnp.where(slot[:, e:e + 1] == col, 1.0, 0.0).astype(BF16)
            acc += _dot(onehot, y_ref[e])
    o_ref[...] = _ln(alpha * x_ref[...] + g2_ref[...] * acc) * lg_ref[...] + lb_ref[...]


def _combine(x1, y, slot_t, mod3, ln_g, ln_b, batch, seq, cap, row_blk0, per_request, alpha):
    rows, d = x1.shape
    n_exp = y.shape[0]
    tt = min(256, seq)
    nt = seq // tt
    vec = pl.BlockSpec((1, d), lambda b, i: (0, 0))
    return pl.pallas_call(
        functools.partial(_combine_kernel, alpha=alpha, cap=cap),
        grid=(batch, nt),
        in_specs=[pl.BlockSpec((tt, d), lambda b, i: (b * nt + i, 0)),
                  pl.BlockSpec((n_exp, cap, d), lambda b, i: (0, row_blk0 + b, 0)),
                  pl.BlockSpec((None, tt, n_exp), lambda b, i: (b, i, 0)),
                  pl.BlockSpec((None, 1, d), lambda b, i: (_mod_row(b, 1, per_request) * 6 + 5, 0, 0)),
                  vec, vec],
        out_specs=pl.BlockSpec((tt, d), lambda b, i: (b * nt + i, 0)),
        out_shape=jax.ShapeDtypeStruct((rows, d), F32),
        compiler_params=_params(2),
        name="moe_combine_postnorm",
    )(x1, y, slot_t, mod3, ln_g, ln_b)


def _combine_win_kernel(base_ref, flag_ref, x_ref, y_ref, slot_ref, g2_ref, lg_ref, lb_ref, o_ref, acc_ref,
                        *, alpha, cap):
    step = pl.program_id(0) * pl.num_programs(1) + pl.program_id(1)
    n_exp = y_ref.shape[0]
    tt = x_ref.shape[0]
    win = MOE_WINDOW
    slot = slot_ref[...]

    @pl.when(flag_ref[step] == 0)
    def _():
        lane = lax.broadcasted_iota(jnp.int32, (tt, 2 * win), 1)
        hots, ys = [], []
        for pair in range(n_exp // 2):
            picks = []
            for half in range(2):
                e = 2 * pair + half
                base = pl.multiple_of(base_ref[step * n_exp + e], 16)
                rel = slot[:, e:e + 1] - base
                picks.append(jnp.where((rel >= 0) & (rel < win), rel + half * win, -1))
                ys.append(y_ref[e, pl.ds(base, win), :])
            hots.append(jnp.where((picks[0] == lane) | (picks[1] == lane), 1.0, 0.0).astype(BF16))
        acc_ref[...] = _dot(jnp.concatenate(hots, axis=1), jnp.concatenate(ys, axis=0))

    @pl.when(flag_ref[step] != 0)
    def _():
        col = lax.broadcasted_iota(jnp.int32, (tt, cap), 1)
        acc = jnp.zeros(x_ref.shape, F32)
        for e in range(n_exp):
            acc += _dot(jnp.where(slot[:, e:e + 1] == col, 1.0, 0.0).astype(BF16), y_ref[e])
        acc_ref[...] = acc

    o_ref[...] = _ln(alpha * x_ref[...] + g2_ref[...] * acc_ref[...]) * lg_ref[...] + lb_ref[...]


def _combine_windowed(x1, y, slot_t, base, flag, mod3, ln_g, ln_b, batch, seq, cap, row_blk0, alpha):
    rows, d = x1.shape
    n_exp = y.shape[0]
    tt = MOE_CHUNK
    nt = seq // tt
    assert 2 * MOE_WINDOW == 128 and n_exp % 2 == 0
    vec = pl.BlockSpec((1, d), lambda b, i, *_: (0, 0))
    return pl.pallas_call(
        functools.partial(_combine_win_kernel, alpha=alpha, cap=cap),
        grid_spec=pltpu.PrefetchScalarGridSpec(
            num_scalar_prefetch=2,
            grid=(batch, nt),
            in_specs=[pl.BlockSpec((tt, d), lambda b, i, *_: (b * nt + i, 0)),
                      pl.BlockSpec((n_exp, cap, d), lambda b, i, *_: (0, row_blk0 + b, 0)),
                      pl.BlockSpec((None, tt, n_exp), lambda b, i, *_: (b, i, 0)),
                      pl.BlockSpec((None, 1, d), lambda b, i, *_: (_mod_row(b, 1, True) * 6 + 5, 0, 0)),
                      vec, vec],
            out_specs=pl.BlockSpec((tt, d), lambda b, i, *_: (b * nt + i, 0)),
            scratch_shapes=[pltpu.VMEM((tt, d), F32)]),
        out_shape=jax.ShapeDtypeStruct((rows, d), F32),
        compiler_params=_params(2),
        name="moe_combine_windowed_postnorm",
    )(base, flag, x1, y, slot_t, mod3, ln_g, ln_b)


def kernel(x_prompt, x_sample, cache_na_k, cache_na_v, cache_gqa_k, cache_gqa_v, state_ret_fwd, state_ret_bwd,
           c, c_ctx, w_in, w_out, w_mod, b_mod, ln1_g, ln1_b, ln2_g, ln2_b, q_norm_g, k_norm_g, na_rpb,
           ret_decay_fwd, ret_decay_bwd, w_router, w_gate, w_up, w_down):
    batch, seq, d = x_prompt.shape
    dbatch, dseq, _ = x_sample.shape
    depth = w_in.shape[0]
    na_heads = cache_na_k.shape[2]
    kv_heads = cache_gqa_k.shape[2]
    ret_heads = state_ret_fwd.shape[2]
    n_exp = w_router.shape[2]
    hd = HEAD_DIM
    gqa_heads = (w_in.shape[2] // hd - 3 * na_heads - 2 * kv_heads - 4 * ret_heads)
    group = gqa_heads // kv_heads
    alpha = float((2 * depth) ** 0.25)
    c_na = 0
    c_gq = 3 * na_heads
    c_gk = c_gq + gqa_heads
    c_gv = c_gk + kv_heads
    c_rt = c_gv + kv_heads
    cap_c = EC_CAPACITY_FACTOR * seq // n_exp
    cap_l = EC_CAPACITY_FACTOR * dseq // n_exp
    assert dbatch + 1 <= MOD_ROWS and (batch * cap_c) % cap_l == 0
    lat_blk0 = batch * cap_c // cap_l

    cond = jnp.zeros((MOD_ROWS, d), F32).at[0].set(c_ctx).at[1:1 + dbatch].set(c)
    mod_all = _modulation(cond, w_mod, b_mod)
    tables = _rope_tables(dseq)
    bias = _na_bias(na_rpb, dseq // GRID_W)

    w_in_b, w_out_b = w_in.astype(BF16), w_out.astype(BF16)

    xp = x_prompt.reshape(batch * seq, d)
    xs = x_sample.reshape(dbatch * dseq, d)
    outs = [[] for _ in range(6)]
    for l in range(depth):
        mod3 = mod_all[l].reshape(MOD_ROWS * 6, 1, d)
        wr_t = jnp.concatenate(_split_bf16(w_router[l]), axis=1)
        gains = (q_norm_g[l].reshape(1, hd), k_norm_g[l].reshape(1, hd))
        l1g, l1b = ln1_g[l].reshape(1, d), ln1_b[l].reshape(1, d)
        l2g, l2b = ln2_g[l].reshape(1, d), ln2_b[l].reshape(1, d)

        zc = _in_projection(xp, mod3, w_in_b, l, seq, False, F32)
        (na_c,) = _ctx_attention(zc, batch, seq, na_heads, 1, c_na, c_na + na_heads, c_na + 2 * na_heads)
        gq_c, gk_n = _ctx_attention(zc, batch, seq, kv_heads, group, c_gq, c_gk, c_gv, gains)
        rt_c, s_f, s_b = _retention(zc, batch, seq, ret_heads, c_rt, ret_decay_fwd[l], ret_decay_bwd[l],
                                    state_out=True)

        def heads_of(col, n):
            return zc[:, col * hd:(col + n) * hd].reshape(batch, seq, n, hd).transpose(0, 2, 1, 3)

        outs[0].append(heads_of(c_na + na_heads, na_heads))
        outs[1].append(heads_of(c_na + 2 * na_heads, na_heads))
        outs[2].append(gk_n)
        outs[3].append(heads_of(c_gv, kv_heads))
        outs[4].append(s_f)
        outs[5].append(s_b)

        zl = _in_projection(xs, mod3, w_in_b, l, dseq, True, BF16)
        (na_l,) = [_na_attention(zl, dbatch, dseq, na_heads, cache_na_k, cache_na_v, l, bias)]
        gq_l = _gqa_attention(zl, dbatch, dseq, kv_heads, group, c_gq, c_gk, c_gv, cache_gqa_k, cache_gqa_v, l,
                              gains, tables)
        (rt_l,) = _retention(zl, dbatch, dseq, ret_heads, c_rt, ret_decay_fwd[l], ret_decay_bwd[l],
                             states=((state_ret_fwd, state_ret_bwd), l))

        xp1, hp, lt_c = _out_projection(xp, na_c, gq_c, rt_c, w_out_b, l, mod3, l1g, l1b, wr_t, seq, False, alpha)
        xs1, hs, lt_l = _out_projection(xs, na_l, gq_l, rt_l, w_out_b, l, mod3, l1g, l1b, wr_t, dseq, True, alpha)

        slot_c, gate_c, _ = _route(lt_c.T, batch, seq, cap_c)
        slot_l, gate_l, before_l = _route(lt_l.T, dbatch, dseq, cap_l)
        windowed = cap_l > MOE_WINDOW and dseq % MOE_CHUNK == 0
        if windowed:
            base_l, flag_l = _window_plan(before_l, cap_l, dseq)
            rows_l, gates_l = _gather_windowed(slot_l, gate_l, hs, base_l, flag_l, dbatch, dseq, cap_l)
        else:
            rows_l, gates_l = _gather(slot_l, gate_l, hs, dbatch, dseq, cap_l, 2)
        rows_c, gates_c = _gather(slot_c, gate_c, hp, batch, seq, cap_c, n_exp)
        y = _expert_ffn(rows_c, gates_c, rows_l, gates_l, w_gate, w_up, w_down, l)
        xp = _combine(xp1, y, slot_c.transpose(0, 2, 1), mod3, l2g, l2b, batch, seq, cap_c, 0, False, alpha)
        if windowed:
            xs = _combine_windowed(xs1, y, slot_l.transpose(0, 2, 1), base_l, flag_l, mod3, l2g, l2b, dbatch, dseq,
                                   cap_l, lat_blk0, alpha)
        else:
            xs = _combine(xs1, y, slot_l.transpose(0, 2, 1), mod3, l2g, l2b, dbatch, dseq, cap_l, lat_blk0, True,
                          alpha)

    stacked = [jnp.stack(o, axis=1) for o in outs]
    return (xp.reshape(batch, seq, d), xs.reshape(dbatch, dseq, d), *stacked)
```

```python
import functools
import math

import numpy as np
import jax
import jax.numpy as jnp
from jax import lax
from jax.experimental import pallas as pl
from jax.experimental.pallas import tpu as pltpu

F32 = jnp.float32
BF16 = jnp.bfloat16

HEAD_DIM = 128
GRID_W = 64
NA_WIN_R = 8
NA_WIN_C = 16
ROPE_THETA = 10000.0
EC_CAPACITY_FACTOR = 2
LN_EPS = 1e-5
RMS_EPS = 1e-6
NEG_INF = -1e30
ATTN_SCALE = HEAD_DIM ** -0.5
LOG2E = 1.4426950408889634
NA_BLOCK_ROWS = 4
ROUTE_GROUP = 16
MOE_CHUNK = 256
MOE_WINDOW = 64
MOD_ROWS = 16
VMEM_LIMIT = 56 * 1024 * 1024

_NT = (((1,), (1,)), ((), ()))
_NN = (((1,), (0,)), ((), ()))


def _params(n_grid, vmem=VMEM_LIMIT):
    return pltpu.CompilerParams(dimension_semantics=("arbitrary",) * n_grid, vmem_limit_bytes=vmem)


def _dot(a, b, dims=_NN):
    return lax.dot_general(a, b, dims, preferred_element_type=F32)


def _split_bf16(x):
    hi = x.astype(BF16)
    lo = (x - hi.astype(F32)).astype(BF16)
    return hi, lo


def _dot3(a, b, dims=_NN):
    ah, al = _split_bf16(a)
    bh, bl = _split_bf16(b)
    return _dot(ah, bh, dims) + _dot(al, bh, dims) + _dot(ah, bl, dims)


def _silu(x):
    return x / (1.0 + jnp.exp(-x))


def _ln(x):
    mu = jnp.mean(x, axis=-1, keepdims=True)
    xc = x - mu
    var = jnp.mean(xc * xc, axis=-1, keepdims=True)
    return xc * lax.rsqrt(var + LN_EPS)


def _rms(x, g):
    return x * lax.rsqrt(jnp.mean(x * x, axis=-1, keepdims=True) + RMS_EPS) * g


def _rope(x, cos, sin_signed):
    lane = lax.broadcasted_iota(jnp.int32, x.shape, 1) & (HEAD_DIM // 2 - 1)
    partner = jnp.where(lane < HEAD_DIM // 4,
                        pltpu.roll(x, HEAD_DIM - HEAD_DIM // 4, 1),
                        pltpu.roll(x, HEAD_DIM // 4, 1))
    return x * cos + partner * sin_signed


def _softmax_pv(scores, values):
    m = functools.reduce(jnp.maximum, [jnp.max(s, axis=-1, keepdims=True) for s in scores])
    ps = [jnp.exp(s - m) for s in scores]
    denom = functools.reduce(lambda a, b: a + b, [jnp.sum(p, axis=-1, keepdims=True) for p in ps])
    o = functools.reduce(lambda a, b: a + b, [_dot(p.astype(BF16), v) for p, v in zip(ps, values)])
    return o / denom


def _mod_kernel(c_ref, w_ref, b_ref, o_ref):
    a = _silu(c_ref[...])
    o_ref[...] = _dot3(a, w_ref[...]) + b_ref[...]


def _modulation(cond, w_mod, b_mod):
    depth, d, n = w_mod.shape
    tn = 768
    return pl.pallas_call(
        _mod_kernel,
        grid=(depth, n // tn),
        in_specs=[pl.BlockSpec((MOD_ROWS, d), lambda l, j: (0, 0)),
                  pl.BlockSpec((None, d, tn), lambda l, j: (l, 0, j)),
                  pl.BlockSpec((None, 1, tn), lambda l, j: (l, 0, j))],
        out_specs=pl.BlockSpec((None, MOD_ROWS, tn), lambda l, j: (l, 0, j)),
        out_shape=jax.ShapeDtypeStruct((depth, MOD_ROWS, n), F32),
        compiler_params=_params(2),
        name="adaln_mod",
    )(cond, w_mod, b_mod.reshape(depth, 1, n))


def _inproj_kernel(x_ref, sh_ref, sc_ref, w_ref, z_ref, *, sub, tn):
    for r in range(x_ref.shape[0] // sub):
        rs = slice(r * sub, (r + 1) * sub)
        h = (_ln(x_ref[rs, :]) * (1.0 + sc_ref[...]) + sh_ref[...]).astype(BF16)
        for c in range(w_ref.shape[1] // tn):
            cs = slice(c * tn, (c + 1) * tn)
            z_ref[rs, cs] = _dot(h, w_ref[:, cs]).astype(z_ref.dtype)


def _mod_row(block, blocks_per_request, per_request):
    return 1 + block // blocks_per_request if per_request else 0


def _in_projection(x, mod3, w_bf16, layer, tokens_per_batch, per_request, out_dtype):
    rows, d = x.shape
    n = w_bf16.shape[2]
    tm = min(512 if out_dtype == BF16 else 256, tokens_per_batch)
    per = tokens_per_batch // tm

    def mod_spec(chunk):
        return pl.BlockSpec((None, 1, d), lambda i: (_mod_row(i, per, per_request) * 6 + chunk, 0, 0))

    return pl.pallas_call(
        functools.partial(_inproj_kernel, sub=min(256, tm), tn=n // 2),
        grid=(rows // tm,),
        in_specs=[pl.BlockSpec((tm, d), lambda i: (i, 0)),
                  mod_spec(0), mod_spec(1),
                  pl.BlockSpec((None, d, n), lambda i: (layer, 0, 0), pipeline_mode=pl.Buffered(1))],
        out_specs=pl.BlockSpec((tm, n), lambda i: (i, 0)),
        out_shape=jax.ShapeDtypeStruct((rows, n), out_dtype),
        compiler_params=_params(1),
        name="ln_inproj",
    )(x, mod3, mod3, w_bf16)


def _ctx_attn_kernel(*refs, group, normed):
    if normed:
        q_ref, k_ref, v_ref, qg_ref, kg_ref, o_ref, kn_ref = refs
    else:
        q_ref, k_ref, v_ref, o_ref = refs
    for hh in range(k_ref.shape[1] // HEAD_DIM):
        k = k_ref[:, hh * HEAD_DIM:(hh + 1) * HEAD_DIM]
        if normed:
            k = _rms(k, kg_ref[...])
            kn_ref[...] = k
        kb = k.astype(BF16)
        vb = v_ref[:, hh * HEAD_DIM:(hh + 1) * HEAD_DIM].astype(BF16)
        for g in range(group):
            cols = slice((hh * group + g) * HEAD_DIM, (hh * group + g + 1) * HEAD_DIM)
            q = q_ref[:, cols]
            if normed:
                q = _rms(q, qg_ref[...])
            s = _dot(q.astype(BF16), kb, _NT) * ATTN_SCALE
            o_ref[:, cols] = _softmax_pv([s], [vb]).astype(o_ref.dtype)


def _ctx_attention(z, batch, seq, heads, group, q_col, k_col, v_col, gains=None):
    normed = gains is not None
    hps = 1 if normed else heads
    assert q_col % (group * hps) == 0 and k_col % hps == 0 and v_col % hps == 0
    qw = group * hps * HEAD_DIM
    kw = hps * HEAD_DIM
    in_specs = [pl.BlockSpec((seq, qw), lambda b, h: (b, q_col // (group * hps) + h)),
                pl.BlockSpec((seq, kw), lambda b, h: (b, k_col // hps + h)),
                pl.BlockSpec((seq, kw), lambda b, h: (b, v_col // hps + h))]
    args = [z, z, z]
    out_specs = [pl.BlockSpec((seq, qw), lambda b, h: (b, h))]
    out_shape = [jax.ShapeDtypeStruct((batch * seq, heads * group * HEAD_DIM), BF16)]
    if normed:
        in_specs += [pl.BlockSpec((1, HEAD_DIM), lambda b, h: (0, 0))] * 2
        args += list(gains)
        out_specs.append(pl.BlockSpec((None, None, seq, HEAD_DIM), lambda b, h: (b, h, 0, 0)))
        out_shape.append(jax.ShapeDtypeStruct((batch, heads, seq, HEAD_DIM), F32))
    return pl.pallas_call(
        functools.partial(_ctx_attn_kernel, group=group, normed=normed),
        grid=(batch, heads // hps),
        in_specs=in_specs, out_specs=out_specs, out_shape=out_shape,
        compiler_params=_params(2),
        name="ctx_gqa" if normed else "ctx_na",
    )(*args)


def _log_sigmoid(x):
    return -jnp.log1p(jnp.exp(-x))


def _retention_kernel(*refs, seq, tq, with_state_in, with_state_out):
    q_ref, k_ref, v_ref, g_ref, df_ref, db_ref = refs[:6]
    refs = refs[6:]
    if with_state_in:
        sf_ref, sb_ref = refs[:2]
        refs = refs[2:]
    o_ref = refs[0]
    decay_ref = refs[-1]
    chunk = tq
    n_chunks = seq // chunk
    il = lax.broadcasted_iota(jnp.int32, (chunk, 1), 0).astype(F32)
    for hh in range(decay_ref.shape[0]):
        hs = slice(hh * HEAD_DIM, (hh + 1) * HEAD_DIM)
        lgf = _log_sigmoid(df_ref[hh])
        lgb = _log_sigmoid(db_ref[hh])

        @pl.when(pl.program_id(1) == 0)
        def _():
            i = lax.broadcasted_iota(jnp.int32, (chunk, chunk), 0)
            j = lax.broadcasted_iota(jnp.int32, (chunk, chunk), 1)
            dist = (i - j).astype(F32)
            decay = jnp.where(dist == 0.0, 2.0, jnp.exp(jnp.where(dist > 0.0, lgf * dist, -lgb * dist)))
            decay_ref[hh] = decay * ATTN_SCALE

        k_dec_f = jnp.exp(lgf * (chunk - 1.0 - il)) * ATTN_SCALE
        k_dec_b = jnp.exp(lgb * il) * ATTN_SCALE
        add_f, add_b = [], []
        for c in range(n_chunks):
            cs = slice(c * chunk, (c + 1) * chunk)
            kf = k_ref[cs, hs].astype(F32)
            vb = v_ref[cs, hs].astype(BF16)
            add_f.append(_dot((kf * k_dec_f).T.astype(BF16), vb))
            add_b.append(_dot((kf * k_dec_b).T.astype(BF16), vb))
        state_f, state_b = [None] * n_chunks, [None] * n_chunks
        carry_f = jnp.exp(lgf * chunk)
        carry_b = jnp.exp(lgb * chunk)
        s = sf_ref[hh] if with_state_in else None
        for c in range(n_chunks):
            state_f[c] = s
            s = add_f[c] if s is None else carry_f * s + add_f[c]
        final_f = s
        s = sb_ref[hh] if with_state_in else None
        for c in reversed(range(n_chunks)):
            state_b[c] = s
            s = add_b[c] if s is None else carry_b * s + add_b[c]
        final_b = s

        q_dec_f = jnp.exp(lgf * (il + 1.0))
        q_dec_b = jnp.exp(lgb * (chunk - il))
        for c in range(n_chunks):
            cs = slice(c * chunk, (c + 1) * chunk)
            q = q_ref[cs, hs]
            att = _dot(q.astype(BF16), k_ref[cs, hs].astype(BF16), _NT) * decay_ref[hh]
            o = _dot(att.astype(BF16), v_ref[cs, hs].astype(BF16))
            qf = q.astype(F32)
            if state_f[c] is not None:
                o += _dot((qf * q_dec_f).astype(BF16), state_f[c].astype(BF16))
            if state_b[c] is not None:
                o += _dot((qf * q_dec_b).astype(BF16), state_b[c].astype(BF16))
            o_ref[cs, hs] = (_ln(o) * _silu(g_ref[cs, hs].astype(F32))).astype(o_ref.dtype)
        if with_state_out:
            nsf_ref, nsb_ref = refs[1:3]
            nsf_ref[hh] = final_f
            nsb_ref[hh] = final_b


def _retention(z, batch, seq, heads, col0, dec_f, dec_b, states=None, state_out=False):
    chunk = min(512, seq)
    hps = heads if seq == chunk else 1
    assert col0 % hps == 0 and seq % chunk == 0
    w = hps * HEAD_DIM
    c0 = col0 // hps
    nh = heads // hps
    in_specs = [pl.BlockSpec((seq, w), lambda h, b: (b, c0 + h)),
                pl.BlockSpec((seq, w), lambda h, b: (b, c0 + nh + h)),
                pl.BlockSpec((seq, w), lambda h, b: (b, c0 + 2 * nh + h)),
                pl.BlockSpec((seq, w), lambda h, b: (b, c0 + 3 * nh + h)),
                pl.BlockSpec((hps, 1, 1), lambda h, b: (h, 0, 0)),
                pl.BlockSpec((hps, 1, 1), lambda h, b: (h, 0, 0))]
    args = [z, z, z, z, dec_f.reshape(heads, 1, 1), dec_b.reshape(heads, 1, 1)]
    if states is not None:
        (sf, sb), layer = states
        spec = pl.BlockSpec((None, None, hps, HEAD_DIM, HEAD_DIM), lambda h, b: (b, layer, h, 0, 0))
        in_specs += [spec, spec]
        args += [sf, sb]
    out_specs = [pl.BlockSpec((seq, w), lambda h, b: (b, h))]
    out_shape = [jax.ShapeDtypeStruct((batch * seq, heads * HEAD_DIM), BF16)]
    if state_out:
        spec = pl.BlockSpec((None, hps, HEAD_DIM, HEAD_DIM), lambda h, b: (b, h, 0, 0))
        out_specs += [spec, spec]
        out_shape += [jax.ShapeDtypeStruct((batch, heads, HEAD_DIM, HEAD_DIM), F32)] * 2
    return pl.pallas_call(
        functools.partial(_retention_kernel, seq=seq, tq=chunk, with_state_in=states is not None,
                          with_state_out=state_out),
        grid=(nh, batch),
        in_specs=in_specs, out_specs=out_specs, out_shape=out_shape,
        scratch_shapes=[pltpu.VMEM((hps, chunk, chunk), F32)],
        compiler_params=_params(2),
        name="retention",
    )(*args)


def _with_ones(v):
    return jnp.concatenate([v, jnp.ones(v.shape, v.dtype)], axis=1)


def _exp2_pv(scores, values_with_ones):
    m = functools.reduce(jnp.maximum, [jnp.max(s, axis=-1, keepdims=True) for s in scores])
    r = functools.reduce(lambda a, b: a + b,
                         [_dot(jnp.exp2(s - m).astype(BF16), v) for s, v in zip(scores, values_with_ones)])
    return r[:, :HEAD_DIM] / r[:, HEAD_DIM:HEAD_DIM + 1]


def _na_plan(rows):
    kr = min(NA_WIN_R, rows)
    nq = min(NA_BLOCK_ROWS, rows)
    nk = min(nq + kr - 1, rows)
    assert rows % nq == 0
    blocks, cases = [], []
    for r0 in range(0, rows, nq):
        start = int(np.clip(r0 - kr // 2, 0, rows - nk))
        win = [int(np.clip(r - kr // 2, 0, rows - kr)) for r in range(r0, r0 + nq)]
        assert all(start <= w and w + kr <= start + nk for w in win)
        key = (start - r0,) + tuple(w - start for w in win)
        if key not in cases:
            cases.append(key)
        blocks.append((r0, start, cases.index(key)))
    return kr, nq, nk, blocks, cases


def _na_kernel(q_ref, k_ref, v_ref, ck_ref, cv_ref, bias_ref, o_ref, vx_ref, cvx_ref, *, blocks, nq, nk):
    vx_ref[...] = _with_ones(v_ref[...])
    cvx_ref[...] = _with_ones(cv_ref[...].astype(BF16))
    ck = ck_ref[...].astype(BF16)
    c = ATTN_SCALE * LOG2E
    for r0, start, case in blocks:
        qs = slice(r0 * GRID_W, (r0 + nq) * GRID_W)
        ks = slice(start * GRID_W, (start + nk) * GRID_W)
        q = q_ref[qs, :]
        s_loc = _dot(q, k_ref[ks, :], _NT) * c + bias_ref[case]
        s_ctx = _dot(q, ck, _NT) * c
        o_ref[qs, :] = _exp2_pv([s_loc, s_ctx], [vx_ref[ks, :], cvx_ref[...]]).astype(o_ref.dtype)


def _na_bias(rpb, rows):
    kr, nq, nk, _, cases = _na_plan(rows)
    dr = np.zeros((len(cases), nq, nk), np.int32)
    row_ok = np.zeros((len(cases), nq, nk), bool)
    for c, key in enumerate(cases):
        rel, offs = key[0], key[1:]
        for ri in range(nq):
            for ju in range(nk):
                row_ok[c, ri, ju] = 0 <= ju - offs[ri] < kr
                dr[c, ri, ju] = np.clip(rel + ju - ri + NA_WIN_R - 1, 0, 2 * NA_WIN_R - 2)
    cq = np.arange(GRID_W)
    ck = np.arange(GRID_W)
    col_start = np.clip(cq - NA_WIN_C // 2, 0, GRID_W - NA_WIN_C)
    col_ok = (ck[None, :] >= col_start[:, None]) & (ck[None, :] < col_start[:, None] + NA_WIN_C)
    dc = np.clip(ck[None, :] - cq[:, None] + (NA_WIN_C - 1), 0, 2 * NA_WIN_C - 2)
    pick_col = (dc.reshape(-1)[None, :] == np.arange(2 * NA_WIN_C - 1)[:, None]).astype(np.float32)
    n_l, n_h = rpb.shape[:2]
    t = jnp.take(rpb.astype(F32), dr.reshape(-1), axis=2)
    t = jnp.einsum("lhxb,bn->lhxn", t, pick_col, precision=lax.Precision.HIGHEST)
    t = t.reshape(n_l, n_h, len(cases), nq, nk, GRID_W, GRID_W).transpose(0, 1, 2, 3, 5, 4, 6)
    ok = row_ok[:, :, None, :, None] & col_ok[None, None, :, None, :]
    t = jnp.where(ok, t * LOG2E, NEG_INF)
    return t.reshape(n_l, n_h, len(cases), nq * GRID_W, nk * GRID_W)


def _na_attention(z, batch, seq, heads, cache_k, cache_v, layer, bias):
    _, nq, nk, blocks, cases = _na_plan(seq // GRID_W)
    past = cache_k.shape[3]
    cspec = pl.BlockSpec((None, None, None, past, HEAD_DIM), lambda b, h: (b, layer, h, 0, 0))
    return pl.pallas_call(
        functools.partial(_na_kernel, blocks=blocks, nq=nq, nk=nk),
        grid=(batch, heads),
        in_specs=[pl.BlockSpec((seq, HEAD_DIM), lambda b, h: (b, h)),
                  pl.BlockSpec((seq, HEAD_DIM), lambda b, h: (b, heads + h)),
                  pl.BlockSpec((seq, HEAD_DIM), lambda b, h: (b, 2 * heads + h)),
                  cspec, cspec,
                  pl.BlockSpec((None, None, len(cases), nq * GRID_W, nk * GRID_W),
                               lambda b, h: (layer, h, 0, 0, 0))],
        out_specs=pl.BlockSpec((seq, HEAD_DIM), lambda b, h: (b, h)),
        out_shape=jax.ShapeDtypeStruct((batch * seq, heads * HEAD_DIM), BF16),
        scratch_shapes=[pltpu.VMEM((seq, 2 * HEAD_DIM), BF16), pltpu.VMEM((past, 2 * HEAD_DIM), BF16)],
        compiler_params=_params(2),
        name="latent_na",
    )(z, z, z, cache_k, cache_v, bias)


def _gqa_kernel(q_ref, k_ref, v_ref, ck_ref, cv_ref, qg_ref, kg_ref, cos_ref, sin_ref, o_ref, kf_ref, vf_ref,
                *, group, past, seq, tq):
    i = pl.program_id(2)

    @pl.when(i == 0)
    def _():
        kf_ref[0:past, :] = ck_ref[...].astype(BF16)
        vf_ref[0:past, :] = _with_ones(cv_ref[...].astype(BF16))
        k = _rope(_rms(k_ref[...].astype(F32), kg_ref[...]), cos_ref[...], sin_ref[...])
        kf_ref[past:past + seq, :] = k.astype(BF16)
        vf_ref[past:past + seq, :] = _with_ones(v_ref[...])

    t0 = pl.multiple_of(i * tq, tq)
    cos = cos_ref[pl.ds(t0, tq), :]
    sin = sin_ref[pl.ds(t0, tq), :]
    kf = kf_ref[...]
    vf = vf_ref[...]
    def scores(g):
        q = q_ref[:, g * HEAD_DIM:(g + 1) * HEAD_DIM].astype(F32)
        q = _rope(_rms(q, qg_ref[...]), cos, sin) * (ATTN_SCALE * LOG2E)
        return _dot(q.astype(BF16), kf, _NT)

    s = scores(0)
    for g in range(group):
        s_next = scores(g + 1) if g + 1 < group else None
        o_ref[:, g * HEAD_DIM:(g + 1) * HEAD_DIM] = _exp2_pv([s], [vf]).astype(o_ref.dtype)
        s = s_next


def _rope_tables(seq):
    t = np.arange(seq)
    half = HEAD_DIM // 2
    inv = 1.0 / (ROPE_THETA ** (np.arange(0, half, 2, dtype=np.float32) / half))
    row = (t // GRID_W).astype(np.float32)
    col = (t % GRID_W).astype(np.float32)
    ang = jnp.concatenate([jnp.asarray(row[:, None] * inv)] * 2 + [jnp.asarray(col[:, None] * inv)] * 2, axis=-1)
    sign = np.where((np.arange(HEAD_DIM) % half) < half // 2, -1.0, 1.0).astype(np.float32)
    return jnp.cos(ang), jnp.sin(ang) * sign


def _gqa_attention(z, batch, seq, kv_heads, group, q_col, k_col, v_col, cache_k, cache_v, layer, gains, tables):
    past = cache_k.shape[3]
    tq = min(512, seq)
    nq = seq // tq
    qw = group * HEAD_DIM
    cspec = pl.BlockSpec((None, None, None, past, HEAD_DIM), lambda b, h, i: (b, layer, h, 0, 0))
    gspec = pl.BlockSpec((1, HEAD_DIM), lambda b, h, i: (0, 0))
    tspec = pl.BlockSpec((seq, HEAD_DIM), lambda b, h, i: (0, 0))
    return pl.pallas_call(
        functools.partial(_gqa_kernel, group=group, past=past, seq=seq, tq=tq),
        grid=(batch, kv_heads, nq),
        in_specs=[pl.BlockSpec((tq, qw), lambda b, h, i: (b * nq + i, q_col // group + h)),
                  pl.BlockSpec((seq, HEAD_DIM), lambda b, h, i: (b, k_col + h)),
                  pl.BlockSpec((seq, HEAD_DIM), lambda b, h, i: (b, v_col + h)),
                  cspec, cspec, gspec, gspec, tspec, tspec],
        out_specs=pl.BlockSpec((tq, qw), lambda b, h, i: (b * nq + i, h)),
        out_shape=jax.ShapeDtypeStruct((batch * seq, kv_heads * qw), BF16),
        scratch_shapes=[pltpu.VMEM((past + seq, HEAD_DIM), BF16), pltpu.VMEM((past + seq, 2 * HEAD_DIM), BF16)],
        compiler_params=_params(3),
        name="latent_gqa",
    )(z, z, z, cache_k, cache_v, gains[0], gains[1], tables[0], tables[1])


def _outproj_kernel(x_ref, na_ref, gq_ref, rt_ref, w_ref, g1_ref, sh2_ref, sc2_ref, lg_ref, lb_ref, wr_ref,
                    x1_ref, h2_ref, lt_ref, *, alpha, sub):
    n_exp = lt_ref.shape[1]
    for r in range(x_ref.shape[0] // sub):
        rs = slice(r * sub, (r + 1) * sub)
        mix = jnp.concatenate([na_ref[rs, :], gq_ref[rs, :], rt_ref[rs, :]], axis=1)
        y = _dot(mix, w_ref[...])
        x1 = _ln(alpha * x_ref[rs, :] + g1_ref[...] * y) * lg_ref[...] + lb_ref[...]
        x1_ref[rs, :] = x1
        h2 = _ln(x1) * (1.0 + sc2_ref[...]) + sh2_ref[...]
        hi, lo = _split_bf16(h2)
        h2_ref[rs, :] = hi
        both = _dot(hi, wr_ref[...])
        lt_ref[rs, :] = both[:, :n_exp] + both[:, n_exp:] + _dot(lo, wr_ref[:, 0:n_exp])


def _out_projection(x, na_o, gq_o, rt_o, w_bf16, layer, mod3, ln_g, ln_b, wr_hl, tokens_per_batch, per_request,
                    alpha):
    rows, d = x.shape
    tm = 512
    per = tokens_per_batch // tm if per_request else 1
    n_exp = wr_hl.shape[1] // 2

    def mod_spec(chunk):
        return pl.BlockSpec((None, 1, d), lambda i: (_mod_row(i, per, per_request) * 6 + chunk, 0, 0))

    def row_spec(width):
        return pl.BlockSpec((tm, width), lambda i: (i, 0))

    vec = pl.BlockSpec((1, d), lambda i: (0, 0))
    return pl.pallas_call(
        functools.partial(_outproj_kernel, alpha=alpha, sub=256),
        grid=(rows // tm,),
        in_specs=[row_spec(d), row_spec(na_o.shape[1]), row_spec(gq_o.shape[1]), row_spec(rt_o.shape[1]),
                  pl.BlockSpec((None,) + w_bf16.shape[1:], lambda i: (layer, 0, 0)),
                  mod_spec(2), mod_spec(3), mod_spec(4), vec, vec,
                  pl.BlockSpec(wr_hl.shape, lambda i: (0, 0))],
        out_specs=[row_spec(d), row_spec(d), row_spec(n_exp)],
        out_shape=[jax.ShapeDtypeStruct((rows, d), F32), jax.ShapeDtypeStruct((rows, d), BF16),
                   jax.ShapeDtypeStruct((rows, n_exp), F32)],
        compiler_params=_params(1),
        name="outproj_postnorm_router",
    )(x, na_o, gq_o, rt_o, w_bf16, mod3, mod3, mod3, ln_g, ln_b, wr_hl)


def _topk_kernel(lt_ref, slot_ref, gate_ref, before_ref, *, cap, seq, group):
    affs = []
    for g in range(group):
        logits = lt_ref[:, g * seq:(g + 1) * seq]
        m = jnp.max(logits, axis=0, keepdims=True)
        ex = jnp.exp(logits - m)
        affs.append(ex / jnp.sum(ex, axis=0, keepdims=True))
    aff = jnp.concatenate(affs, axis=0)
    n_exp = aff.shape[0]
    bits = lax.bitcast_convert_type(aff, jnp.int32)

    def count(mask):
        return jnp.sum(jnp.where(mask, 1.0, 0.0), axis=1, keepdims=True)

    def value_step(it, thr):
        cand = thr | jnp.left_shift(jnp.int32(1), 30 - it)
        return jnp.where(count(bits >= cand) >= cap, cand, thr)

    thr = lax.fori_loop(0, 31, value_step, jnp.zeros((n_exp, 1), jnp.int32))
    above = bits > thr
    tied = bits == thr
    need = cap - count(above)
    tok = lax.broadcasted_iota(jnp.int32, (n_exp, seq), 1)
    n_bits = int(seq - 1).bit_length()

    def index_step(it, bound):
        cand = bound | jnp.left_shift(jnp.int32(1), n_bits - 1 - it)
        return jnp.where(count(tied & (tok < cand)) < need, cand, bound)

    bound = lax.fori_loop(0, n_bits, index_step, jnp.zeros((n_exp, 1), jnp.int32))
    sel = above | (tied & (tok <= bound))
    self = jnp.where(sel, 1.0, 0.0)
    lanes = 128
    upper = jnp.where(lax.broadcasted_iota(jnp.int32, (lanes, lanes), 0)
                      < lax.broadcasted_iota(jnp.int32, (lanes, lanes), 1), 1.0, 0.0).astype(BF16)
    running = jnp.zeros((n_exp, 1), F32)
    lane = lax.broadcasted_iota(jnp.int32, (n_exp, lanes), 1)
    before = jnp.zeros((n_exp, lanes), F32)
    for blk in range(seq // lanes):
        sl = slice(blk * lanes, (blk + 1) * lanes)
        chunk = self[:, sl]
        before = jnp.where(lane == blk, running, before)
        pos = _dot(chunk.astype(BF16), upper) + running
        slot_ref[:, sl] = jnp.where(chunk > 0.0, pos.astype(jnp.int32), -1)
        running = running + jnp.sum(chunk, axis=1, keepdims=True)
    gate_ref[...] = aff
    before_ref[...] = before.astype(jnp.int32)


def _route(logits_t, batch, seq, cap):
    n_exp = logits_t.shape[0]
    assert seq // 128 <= 128
    group = math.gcd(batch, ROUTE_GROUP)
    rows = group * n_exp
    spec = pl.BlockSpec((rows, seq), lambda i: (i, 0))
    slot, gate, before = pl.pallas_call(
        functools.partial(_topk_kernel, cap=cap, seq=seq, group=group),
        grid=(batch // group,),
        in_specs=[pl.BlockSpec((n_exp, group * seq), lambda i: (0, i))],
        out_specs=[spec, spec, pl.BlockSpec((rows, 128), lambda i: (i, 0))],
        out_shape=[jax.ShapeDtypeStruct((batch * n_exp, seq), jnp.int32),
                   jax.ShapeDtypeStruct((batch * n_exp, seq), F32),
                   jax.ShapeDtypeStruct((batch * n_exp, 128), jnp.int32)],
        compiler_params=_params(1),
        name="route_topk",
    )(logits_t)
    return (slot.reshape(batch, n_exp, seq), gate.reshape(batch, n_exp, seq), before.reshape(batch, n_exp, 128))


def _window_plan(before, cap, seq):
    per = MOE_CHUNK // 128
    lo = before[:, :, 0:seq // 128:per]
    hi = jnp.concatenate([lo[:, :, 1:], jnp.full(lo.shape[:2] + (1,), cap, jnp.int32)], axis=2)
    base = jnp.minimum(lo // 16 * 16, cap - MOE_WINDOW)
    overflow = jnp.any(hi - base > MOE_WINDOW, axis=1)
    return base.transpose(0, 2, 1).reshape(-1), overflow.astype(jnp.int32).reshape(-1)


def _gather_kernel(slot_ref, gate_ref, h_ref, xs_ref, gs_ref, *, n_inner, cap):
    h = h_ref[...]
    seq = h.shape[0]
    row = lax.broadcasted_iota(jnp.int32, (cap, seq), 0)
    for e in range(n_inner):
        onehot = slot_ref[e:e + 1, :] == row
        xs_ref[e] = _dot(jnp.where(onehot, 1.0, 0.0).astype(BF16), h).astype(BF16)
        gs_ref[e] = jnp.sum(jnp.where(onehot, gate_ref[e:e + 1, :], 0.0), axis=1, keepdims=True)


def _gather(slot, gate, h, batch, seq, cap, n_inner):
    n_exp = slot.shape[1]
    d = h.shape[1]
    n_outer = n_exp // n_inner
    slot4 = slot.reshape(batch, n_outer, n_inner, seq)
    gate4 = gate.reshape(batch, n_outer, n_inner, seq)
    sspec = pl.BlockSpec((None, None, n_inner, seq), lambda b, e: (b, e, 0, 0))
    return pl.pallas_call(
        functools.partial(_gather_kernel, n_inner=n_inner, cap=cap),
        grid=(batch, n_outer),
        in_specs=[sspec, sspec, pl.BlockSpec((seq, d), lambda b, e: (b, 0))],
        out_specs=[pl.BlockSpec((n_inner, cap, d), lambda b, e: (e, b, 0)),
                   pl.BlockSpec((n_inner, cap, 1), lambda b, e: (e, b, 0))],
        out_shape=[jax.ShapeDtypeStruct((n_exp, batch * cap, d), BF16),
                   jax.ShapeDtypeStruct((n_exp, batch * cap, 1), F32)],
        compiler_params=_params(2),
        name="moe_gather",
    )(slot4, gate4, h)


def _gather_win_kernel(base_ref, flag_ref, slot_ref, gate_ref, h_ref, xs_ref, gs_ref, *, cap, cps):
    k = pl.program_id(1)
    first = (pl.program_id(0) * pl.num_programs(1) + k) * cps
    n_exp = slot_ref.shape[0]
    chunk = MOE_CHUNK
    win = MOE_WINDOW

    @pl.when(k == 0)
    def _():
        xs_ref[...] = jnp.zeros(xs_ref.shape, xs_ref.dtype)
        gs_ref[...] = jnp.zeros(gs_ref.shape, gs_ref.dtype)

    for cc in range(cps):
        step = first + cc
        cols = slice(cc * chunk, (cc + 1) * chunk)

        @pl.when(flag_ref[step] == 0)
        def _():
            h = h_ref[cols, :]
            row = lax.broadcasted_iota(jnp.int32, (win, chunk), 0)
            bases = [pl.multiple_of(base_ref[step * n_exp + e], 16) for e in range(n_exp)]
            hots = [(slot_ref[e:e + 1, cols] - bases[e]) == row for e in range(n_exp)]
            stacked = jnp.concatenate([jnp.where(hot, 1.0, 0.0).astype(BF16) for hot in hots], axis=0)
            picked = _dot(stacked, h)
            for e in range(n_exp):
                rows = pl.ds(bases[e], win)
                xs_ref[e, rows, :] += picked[e * win:(e + 1) * win].astype(BF16)
                gs_ref[e, rows, :] += jnp.sum(jnp.where(hots[e], gate_ref[e:e + 1, cols], 0.0), axis=1,
                                              keepdims=True)

        @pl.when(flag_ref[step] != 0)
        def _():
            h = h_ref[cols, :]
            row = lax.broadcasted_iota(jnp.int32, (cap, chunk), 0)
            for e in range(n_exp):
                hot = slot_ref[e:e + 1, cols] == row
                xs_ref[e] += _dot(jnp.where(hot, 1.0, 0.0).astype(BF16), h).astype(BF16)
                gs_ref[e] += jnp.sum(jnp.where(hot, gate_ref[e:e + 1, cols], 0.0), axis=1, keepdims=True)


def _gather_windowed(slot, gate, h, base, flag, batch, seq, cap):
    n_exp = slot.shape[1]
    d = h.shape[1]
    cps = 2 if seq % (2 * MOE_CHUNK) == 0 else 1
    nk = seq // (MOE_CHUNK * cps)
    sspec = pl.BlockSpec((None, n_exp, MOE_CHUNK * cps), lambda b, k, *_: (b, 0, k))
    return pl.pallas_call(
        functools.partial(_gather_win_kernel, cap=cap, cps=cps),
        grid_spec=pltpu.PrefetchScalarGridSpec(
            num_scalar_prefetch=2,
            grid=(batch, nk),
            in_specs=[sspec, sspec, pl.BlockSpec((MOE_CHUNK * cps, d), lambda b, k, *_: (b * nk + k, 0))],
            out_specs=[pl.BlockSpec((n_exp, cap, d), lambda b, k, *_: (0, b, 0)),
                       pl.BlockSpec((n_exp, cap, 1), lambda b, k, *_: (0, b, 0))]),
        out_shape=[jax.ShapeDtypeStruct((n_exp, batch * cap, d), BF16),
                   jax.ShapeDtypeStruct((n_exp, batch * cap, 1), F32)],
        compiler_params=_params(2),
        name="moe_gather_windowed",
    )(base, flag, slot, gate, h)


def _ffn_kernel(xc_ref, gc_ref, xl_ref, gl_ref, wg_hbm, wu_hbm, wd_hbm, y_ref, wg_s, wu_s, wd_s, stg_g, stg_u,
                stg_d, sems, *, layer, n_chunks, ctx_steps):
    e = pl.program_id(0)
    m = pl.program_id(1)
    slot = e % 2
    rows_gu = wg_s.shape[1] // n_chunks
    rows_d = wd_s.shape[1] // n_chunks

    def chunk_copies(expert, c):
        gu = pl.ds(pl.multiple_of(c * rows_gu, rows_gu), rows_gu)
        dn = pl.ds(pl.multiple_of(c * rows_d, rows_d), rows_d)
        return (pltpu.make_async_copy(wg_hbm.at[layer, expert, gu, :], stg_g, sems.at[0]),
                pltpu.make_async_copy(wu_hbm.at[layer, expert, gu, :], stg_u, sems.at[1]),
                pltpu.make_async_copy(wd_hbm.at[layer, expert, dn, :], stg_d, sems.at[2]))

    def land(dst_slot, c, copies):
        for cp in copies:
            cp.wait()
        gu = pl.ds(pl.multiple_of(c * rows_gu, rows_gu), rows_gu)
        dn = pl.ds(pl.multiple_of(c * rows_d, rows_d), rows_d)
        wg_s[dst_slot, gu, :] = stg_g[...].astype(BF16)
        wu_s[dst_slot, gu, :] = stg_u[...].astype(BF16)
        wd_s[dst_slot, dn, :] = stg_d[...].astype(BF16)

    @pl.when((e == 0) & (m == 0))
    def _():
        for c in range(n_chunks):
            copies = chunk_copies(0, c)
            for cp in copies:
                cp.start()
            land(0, c, copies)

    prefetch = (m < n_chunks) & (e + 1 < pl.num_programs(0))

    @pl.when(prefetch)
    def _():
        for cp in chunk_copies(e + 1, m):
            cp.start()

    def swiglu(x_ref, g_ref):
        x = x_ref[...]
        a = _dot(x, wg_s[slot])
        u = _dot(x, wu_s[slot])
        y = _dot((_silu(a) * u).astype(BF16), wd_s[slot])
        y_ref[...] = (y * g_ref[...]).astype(y_ref.dtype)

    @pl.when(m < ctx_steps)
    def _():
        swiglu(xc_ref, gc_ref)

    @pl.when(m >= ctx_steps)
    def _():
        swiglu(xl_ref, gl_ref)

    @pl.when(prefetch)
    def _():
        land(1 - slot, m, chunk_copies(e + 1, m))


def _expert_ffn(xs_c, gs_c, xs_l, gs_l, wg, wu, wd, layer):
    n_exp, rows_c, d = xs_c.shape
    rows_l = xs_l.shape[1]
    ff = wg.shape[3]
    tm = math.gcd(math.gcd(rows_c, rows_l), 512)
    ctx_steps = rows_c // tm
    steps = ctx_steps + rows_l // tm
    n_chunks = 4 if steps >= 4 else (2 if steps >= 2 else 1)
    any_spec = pl.BlockSpec(memory_space=pl.ANY)

    def ctx_spec(width):
        return pl.BlockSpec((None, tm, width), lambda e, i: (e, jnp.minimum(i, ctx_steps - 1), 0))

    def lat_spec(width):
        return pl.BlockSpec((None, tm, width), lambda e, i: (e, jnp.maximum(i - ctx_steps, 0), 0))

    return pl.pallas_call(
        functools.partial(_ffn_kernel, layer=layer, n_chunks=n_chunks, ctx_steps=ctx_steps),
        grid=(n_exp, steps),
        in_specs=[ctx_spec(d), ctx_spec(1), lat_spec(d), lat_spec(1), any_spec, any_spec, any_spec],
        out_specs=pl.BlockSpec((None, tm, d), lambda e, i: (e, i, 0)),
        out_shape=jax.ShapeDtypeStruct((n_exp, rows_c + rows_l, d), BF16),
        scratch_shapes=[pltpu.VMEM((2, d, ff), BF16), pltpu.VMEM((2, d, ff), BF16), pltpu.VMEM((2, ff, d), BF16),
                        pltpu.VMEM((d // n_chunks, ff), F32), pltpu.VMEM((d // n_chunks, ff), F32),
                        pltpu.VMEM((ff // n_chunks, d), F32), pltpu.SemaphoreType.DMA((3,))],
        compiler_params=_params(2),
        name="moe_ffn",
    )(xs_c, gs_c, xs_l, gs_l, wg, wu, wd)


def _combine_kernel(x_ref, y_ref, slot_ref, g2_ref, lg_ref, lb_ref, o_ref, *, alpha, cap):
    n_exp = y_ref.shape[0]
    tt = x_ref.shape[0]
    slot = slot_ref[...]
    lanes = 128
    if cap < lanes and lanes % cap == 0 and n_exp % (lanes // cap) == 0:
        per = lanes // cap
        lane = lax.broadcasted_iota(jnp.int32, (tt, lanes), 1)
        hots = []
        for g in range(n_exp // per):
            hit = None
            for j in range(per):
                s = slot[:, g * per + j:g * per + j + 1]
                match = jnp.where(s >= 0, s + j * cap, -1) == lane
                hit = match if hit is None else hit | match
            hots.append(jnp.where(hit, 1.0, 0.0).astype(BF16))
        acc = _dot(jnp.concatenate(hots, axis=1), y_ref[...].reshape(n_exp * cap, y_ref.shape[2]))
    else:
        col = lax.broadcasted_iota(jnp.int32, (tt, cap), 1)
        acc = jnp.zeros(x_ref.shape, F32)
        for e in range(n_exp):
            onehot = jnp.where(slot[:, e:e + 1] == col, 1.0, 0.0).astype(BF16)
            acc += _dot(onehot, y_ref[e])
    o_ref[...] = _ln(alpha * x_ref[...] + g2_ref[...] * acc) * lg_ref[...] + lb_ref[...]


def _combine(x1, y, slot_t, mod3, ln_g, ln_b, batch, seq, cap, row_blk0, per_request, alpha):
    rows, d = x1.shape
    n_exp = y.shape[0]
    tt = min(256, seq)
    nt = seq // tt
    vec = pl.BlockSpec((1, d), lambda b, i: (0, 0))
    return pl.pallas_call(
        functools.partial(_combine_kernel, alpha=alpha, cap=cap),
        grid=(batch, nt),
        in_specs=[pl.BlockSpec((tt, d), lambda b, i: (b * nt + i, 0)),
                  pl.BlockSpec((n_exp, cap, d), lambda b, i: (0, row_blk0 + b, 0)),
                  pl.BlockSpec((None, tt, n_exp), lambda b, i: (b, i, 0)),
                  pl.BlockSpec((None, 1, d), lambda b, i: (_mod_row(b, 1, per_request) * 6 + 5, 0, 0)),
                  vec, vec],
        out_specs=pl.BlockSpec((tt, d), lambda b, i: (b * nt + i, 0)),
        out_shape=jax.ShapeDtypeStruct((rows, d), F32),
        compiler_params=_params(2),
        name="moe_combine_postnorm",
    )(x1, y, slot_t, mod3, ln_g, ln_b)


def _combine_win_kernel(base_ref, flag_ref, x_ref, y_ref, slot_ref, g2_ref, lg_ref, lb_ref, o_ref, acc_ref,
                        *, alpha, cap):
    step = pl.program_id(0) * pl.num_programs(1) + pl.program_id(1)
    n_exp = y_ref.shape[0]
    tt = x_ref.shape[0]
    win = MOE_WINDOW
    slot = slot_ref[...]

    @pl.when(flag_ref[step] == 0)
    def _():
        lane = lax.broadcasted_iota(jnp.int32, (tt, 2 * win), 1)
        hots, ys = [], []
        for pair in range(n_exp // 2):
            picks = []
            for half in range(2):
                e = 2 * pair + half
                base = pl.multiple_of(base_ref[step * n_exp + e], 16)
                rel = slot[:, e:e + 1] - base
                picks.append(jnp.where((rel >= 0) & (rel < win), rel + half * win, -1))
                ys.append(y_ref[e, pl.ds(base, win), :])
            hots.append(jnp.where((picks[0] == lane) | (picks[1] == lane), 1.0, 0.0).astype(BF16))
        acc_ref[...] = _dot(jnp.concatenate(hots, axis=1), jnp.concatenate(ys, axis=0))

    @pl.when(flag_ref[step] != 0)
    def _():
        col = lax.broadcasted_iota(jnp.int32, (tt, cap), 1)
        acc = jnp.zeros(x_ref.shape, F32)
        for e in range(n_exp):
            acc += _dot(jnp.where(slot[:, e:e + 1] == col, 1.0, 0.0).astype(BF16), y_ref[e])
        acc_ref[...] = acc

    o_ref[...] = _ln(alpha * x_ref[...] + g2_ref[...] * acc_ref[...]) * lg_ref[...] + lb_ref[...]


def _combine_windowed(x1, y, slot_t, base, flag, mod3, ln_g, ln_b, batch, seq, cap, row_blk0, alpha):
    rows, d = x1.shape
    n_exp = y.shape[0]
    tt = MOE_CHUNK
    nt = seq // tt
    assert 2 * MOE_WINDOW == 128 and n_exp % 2 == 0
    vec = pl.BlockSpec((1, d), lambda b, i, *_: (0, 0))
    return pl.pallas_call(
        functools.partial(_combine_win_kernel, alpha=alpha, cap=cap),
        grid_spec=pltpu.PrefetchScalarGridSpec(
            num_scalar_prefetch=2,
            grid=(batch, nt),
            in_specs=[pl.BlockSpec((tt, d), lambda b, i, *_: (b * nt + i, 0)),
                      pl.BlockSpec((n_exp, cap, d), lambda b, i, *_: (0, row_blk0 + b, 0)),
                      pl.BlockSpec((None, tt, n_exp), lambda b, i, *_: (b, i, 0)),
                      pl.BlockSpec((None, 1, d), lambda b, i, *_: (_mod_row(b, 1, True) * 6 + 5, 0, 0)),
                      vec, vec],
            out_specs=pl.BlockSpec((tt, d), lambda b, i, *_: (b * nt + i, 0)),
            scratch_shapes=[pltpu.VMEM((tt, d), F32)]),
        out_shape=jax.ShapeDtypeStruct((rows, d), F32),
        compiler_params=_params(2),
        name="moe_combine_windowed_postnorm",
    )(base, flag, x1, y, slot_t, mod3, ln_g, ln_b)


def kernel(x_prompt, x_sample, cache_na_k, cache_na_v, cache_gqa_k, cache_gqa_v, state_ret_fwd, state_ret_bwd,
           c, c_ctx, w_in, w_out, w_mod, b_mod, ln1_g, ln1_b, ln2_g, ln2_b, q_norm_g, k_norm_g, na_rpb,
           ret_decay_fwd, ret_decay_bwd, w_router, w_gate, w_up, w_down):
    batch, seq, d = x_prompt.shape
    dbatch, dseq, _ = x_sample.shape
    depth = w_in.shape[0]
    na_heads = cache_na_k.shape[2]
    kv_heads = cache_gqa_k.shape[2]
    ret_heads = state_ret_fwd.shape[2]
    n_exp = w_router.shape[2]
    hd = HEAD_DIM
    gqa_heads = (w_in.shape[2] // hd - 3 * na_heads - 2 * kv_heads - 4 * ret_heads)
    group = gqa_heads // kv_heads
    alpha = float((2 * depth) ** 0.25)
    c_na = 0
    c_gq = 3 * na_heads
    c_gk = c_gq + gqa_heads
    c_gv = c_gk + kv_heads
    c_rt = c_gv + kv_heads
    cap_c = EC_CAPACITY_FACTOR * seq // n_exp
    cap_l = EC_CAPACITY_FACTOR * dseq // n_exp
    assert dbatch + 1 <= MOD_ROWS and (batch * cap_c) % cap_l == 0
    lat_blk0 = batch * cap_c // cap_l

    cond = jnp.zeros((MOD_ROWS, d), F32).at[0].set(c_ctx).at[1:1 + dbatch].set(c)
    mod_all = _modulation(cond, w_mod, b_mod)
    tables = _rope_tables(dseq)
    bias = _na_bias(na_rpb, dseq // GRID_W)

    w_in_b, w_out_b = w_in.astype(BF16), w_out.astype(BF16)

    xp = x_prompt.reshape(batch * seq, d)
    xs = x_sample.reshape(dbatch * dseq, d)
    outs = [[] for _ in range(6)]
    for l in range(depth):
        mod3 = mod_all[l].reshape(MOD_ROWS * 6, 1, d)
        wr_t = jnp.concatenate(_split_bf16(w_router[l]), axis=1)
        gains = (q_norm_g[l].reshape(1, hd), k_norm_g[l].reshape(1, hd))
        l1g, l1b = ln1_g[l].reshape(1, d), ln1_b[l].reshape(1, d)
        l2g, l2b = ln2_g[l].reshape(1, d), ln2_b[l].reshape(1, d)

        zc = _in_projection(xp, mod3, w_in_b, l, seq, False, F32)
        (na_c,) = _ctx_attention(zc, batch, seq, na_heads, 1, c_na, c_na + na_heads, c_na + 2 * na_heads)
        gq_c, gk_n = _ctx_attention(zc, batch, seq, kv_heads, group, c_gq, c_gk, c_gv, gains)
        rt_c, s_f, s_b = _retention(zc, batch, seq, ret_heads, c_rt, ret_decay_fwd[l], ret_decay_bwd[l],
                                    state_out=True)

        def heads_of(col, n):
            return zc[:, col * hd:(col + n) * hd].reshape(batch, seq, n, hd).transpose(0, 2, 1, 3)

        outs[0].append(heads_of(c_na + na_heads, na_heads))
        outs[1].append(heads_of(c_na + 2 * na_heads, na_heads))
        outs[2].append(gk_n)
        outs[3].append(heads_of(c_gv, kv_heads))
        outs[4].append(s_f)
        outs[5].append(s_b)

        zl = _in_projection(xs, mod3, w_in_b, l, dseq, True, BF16)
        (na_l,) = [_na_attention(zl, dbatch, dseq, na_heads, cache_na_k, cache_na_v, l, bias)]
        gq_l = _gqa_attention(zl, dbatch, dseq, kv_heads, group, c_gq, c_gk, c_gv, cache_gqa_k, cache_gqa_v, l,
                              gains, tables)
        (rt_l,) = _retention(zl, dbatch, dseq, ret_heads, c_rt, ret_decay_fwd[l], ret_decay_bwd[l],
                             states=((state_ret_fwd, state_ret_bwd), l))

        xp1, hp, lt_c = _out_projection(xp, na_c, gq_c, rt_c, w_out_b, l, mod3, l1g, l1b, wr_t, seq, False, alpha)
        xs1, hs, lt_l = _out_projection(xs, na_l, gq_l, rt_l, w_out_b, l, mod3, l1g, l1b, wr_t, dseq, True, alpha)

        slot_c, gate_c, _ = _route(lt_c.T, batch, seq, cap_c)
        slot_l, gate_l, before_l = _route(lt_l.T, dbatch, dseq, cap_l)
        windowed = cap_l > MOE_WINDOW and dseq % MOE_CHUNK == 0
        if windowed:
            base_l, flag_l = _window_plan(before_l, cap_l, dseq)
            rows_l, gates_l = _gather_windowed(slot_l, gate_l, hs, base_l, flag_l, dbatch, dseq, cap_l)
        else:
            rows_l, gates_l = _gather(slot_l, gate_l, hs, dbatch, dseq, cap_l, 2)
        rows_c, gates_c = _gather(slot_c, gate_c, hp, batch, seq, cap_c, n_exp)
        y = _expert_ffn(rows_c, gates_c, rows_l, gates_l, w_gate, w_up, w_down, l)
        xp = _combine(xp1, y, slot_c.transpose(0, 2, 1), mod3, l2g, l2b, batch, seq, cap_c, 0, False, alpha)
        if windowed:
            xs = _combine_windowed(xs1, y, slot_l.transpose(0, 2, 1), base_l, flag_l, mod3, l2g, l2b, dbatch, dseq,
                                   cap_l, lat_blk0, alpha)
        else:
            xs = _combine(xs1, y, slot_l.transpose(0, 2, 1), mod3, l2g, l2b, dbatch, dseq, cap_l, lat_blk0, True,
                          alpha)

    stacked = [jnp.stack(o, axis=1) for o in outs]
    return (xp.reshape(batch, seq, d), xs.reshape(dbatch, dseq, d), *stacked)
```

```python
import functools
import math

import numpy as np
import jax
import jax.numpy as jnp
from jax import lax
from jax.experimental import pallas as pl
from jax.experimental.pallas import tpu as pltpu

F32 = jnp.float32
BF16 = jnp.bfloat16

HEAD_DIM = 128
GRID_W = 64
NA_WIN_R = 8
NA_WIN_C = 16
ROPE_THETA = 10000.0
EC_CAPACITY_FACTOR = 2
LN_EPS = 1e-5
RMS_EPS = 1e-6
NEG_INF = -1e30
ATTN_SCALE = HEAD_DIM ** -0.5
LOG2E = 1.4426950408889634
NA_BLOCK_ROWS = 4
ROUTE_GROUP = 16
MOE_CHUNK = 256
MOE_WINDOW = 64
MOD_ROWS = 16
VMEM_LIMIT = 56 * 1024 * 1024

_NT = (((1,), (1,)), ((), ()))
_NN = (((1,), (0,)), ((), ()))


def _params(n_grid, vmem=VMEM_LIMIT):
    return pltpu.CompilerParams(dimension_semantics=("arbitrary",) * n_grid, vmem_limit_bytes=vmem)


def _dot(a, b, dims=_NN):
    return lax.dot_general(a, b, dims, preferred_element_type=F32)


def _split_bf16(x):
    hi = x.astype(BF16)
    lo = (x - hi.astype(F32)).astype(BF16)
    return hi, lo


def _dot3(a, b, dims=_NN):
    ah, al = _split_bf16(a)
    bh, bl = _split_bf16(b)
    return _dot(ah, bh, dims) + _dot(al, bh, dims) + _dot(ah, bl, dims)


def _silu(x):
    return x / (1.0 + jnp.exp(-x))


def _ln(x):
    mu = jnp.mean(x, axis=-1, keepdims=True)
    xc = x - mu
    var = jnp.mean(xc * xc, axis=-1, keepdims=True)
    return xc * lax.rsqrt(var + LN_EPS)


def _rms(x, g):
    return x * lax.rsqrt(jnp.mean(x * x, axis=-1, keepdims=True) + RMS_EPS) * g


def _rope(x, cos, sin_signed):
    lane = lax.broadcasted_iota(jnp.int32, x.shape, 1) & (HEAD_DIM // 2 - 1)
    partner = jnp.where(lane < HEAD_DIM // 4,
                        pltpu.roll(x, HEAD_DIM - HEAD_DIM // 4, 1),
                        pltpu.roll(x, HEAD_DIM // 4, 1))
    return x * cos + partner * sin_signed


def _softmax_pv(scores, values):
    m = functools.reduce(jnp.maximum, [jnp.max(s, axis=-1, keepdims=True) for s in scores])
    ps = [jnp.exp(s - m) for s in scores]
    denom = functools.reduce(lambda a, b: a + b, [jnp.sum(p, axis=-1, keepdims=True) for p in ps])
    o = functools.reduce(lambda a, b: a + b, [_dot(p.astype(BF16), v) for p, v in zip(ps, values)])
    return o / denom


def _mod_kernel(c_ref, w_ref, b_ref, o_ref):
    a = _silu(c_ref[...])
    o_ref[...] = _dot3(a, w_ref[...]) + b_ref[...]


def _modulation(cond, w_mod, b_mod):
    depth, d, n = w_mod.shape
    tn = 768
    return pl.pallas_call(
        _mod_kernel,
        grid=(depth, n // tn),
        in_specs=[pl.BlockSpec((MOD_ROWS, d), lambda l, j: (0, 0)),
                  pl.BlockSpec((None, d, tn), lambda l, j: (l, 0, j)),
                  pl.BlockSpec((None, 1, tn), lambda l, j: (l, 0, j))],
        out_specs=pl.BlockSpec((None, MOD_ROWS, tn), lambda l, j: (l, 0, j)),
        out_shape=jax.ShapeDtypeStruct((depth, MOD_ROWS, n), F32),
        compiler_params=_params(2),
        name="adaln_mod",
    )(cond, w_mod, b_mod.reshape(depth, 1, n))


def _inproj_kernel(x_ref, sh_ref, sc_ref, w_ref, z_ref, *, sub, tn):
    for r in range(x_ref.shape[0] // sub):
        rs = slice(r * sub, (r + 1) * sub)
        h = (_ln(x_ref[rs, :]) * (1.0 + sc_ref[...]) + sh_ref[...]).astype(BF16)
        for c in range(w_ref.shape[1] // tn):
            cs = slice(c * tn, (c + 1) * tn)
            z_ref[rs, cs] = _dot(h, w_ref[:, cs]).astype(z_ref.dtype)


def _mod_row(block, blocks_per_request, per_request):
    return 1 + block // blocks_per_request if per_request else 0


def _in_projection(x, mod3, w_bf16, layer, tokens_per_batch, per_request, out_dtype):
    rows, d = x.shape
    n = w_bf16.shape[2]
    tm = min(512 if out_dtype == BF16 else 256, tokens_per_batch)
    per = tokens_per_batch // tm

    def mod_spec(chunk):
        return pl.BlockSpec((None, 1, d), lambda i: (_mod_row(i, per, per_request) * 6 + chunk, 0, 0))

    return pl.pallas_call(
        functools.partial(_inproj_kernel, sub=min(256, tm), tn=n // 2),
        grid=(rows // tm,),
        in_specs=[pl.BlockSpec((tm, d), lambda i: (i, 0)),
                  mod_spec(0), mod_spec(1),
                  pl.BlockSpec((None, d, n), lambda i: (layer, 0, 0), pipeline_mode=pl.Buffered(1))],
        out_specs=pl.BlockSpec((tm, n), lambda i: (i, 0)),
        out_shape=jax.ShapeDtypeStruct((rows, n), out_dtype),
        compiler_params=_params(1),
        name="ln_inproj",
    )(x, mod3, mod3, w_bf16)


def _ctx_attn_kernel(*refs, group, normed):
    if normed:
        q_ref, k_ref, v_ref, qg_ref, kg_ref, o_ref, kn_ref = refs
    else:
        q_ref, k_ref, v_ref, o_ref = refs
    for hh in range(k_ref.shape[1] // HEAD_DIM):
        k = k_ref[:, hh * HEAD_DIM:(hh + 1) * HEAD_DIM]
        if normed:
            k = _rms(k, kg_ref[...])
            kn_ref[...] = k
        kb = k.astype(BF16)
        vb = v_ref[:, hh * HEAD_DIM:(hh + 1) * HEAD_DIM].astype(BF16)
        for g in range(group):
            cols = slice((hh * group + g) * HEAD_DIM, (hh * group + g + 1) * HEAD_DIM)
            q = q_ref[:, cols]
            if normed:
                q = _rms(q, qg_ref[...])
            s = _dot(q.astype(BF16), kb, _NT) * ATTN_SCALE
            o_ref[:, cols] = _softmax_pv([s], [vb]).astype(o_ref.dtype)


def _ctx_attention(z, batch, seq, heads, group, q_col, k_col, v_col, gains=None):
    normed = gains is not None
    hps = 1 if normed else heads
    assert q_col % (group * hps) == 0 and k_col % hps == 0 and v_col % hps == 0
    qw = group * hps * HEAD_DIM
    kw = hps * HEAD_DIM
    in_specs = [pl.BlockSpec((seq, qw), lambda b, h: (b, q_col // (group * hps) + h)),
                pl.BlockSpec((seq, kw), lambda b, h: (b, k_col // hps + h)),
                pl.BlockSpec((seq, kw), lambda b, h: (b, v_col // hps + h))]
    args = [z, z, z]
    out_specs = [pl.BlockSpec((seq, qw), lambda b, h: (b, h))]
    out_shape = [jax.ShapeDtypeStruct((batch * seq, heads * group * HEAD_DIM), BF16)]
    if normed:
        in_specs += [pl.BlockSpec((1, HEAD_DIM), lambda b, h: (0, 0))] * 2
        args += list(gains)
        out_specs.append(pl.BlockSpec((None, None, seq, HEAD_DIM), lambda b, h: (b, h, 0, 0)))
        out_shape.append(jax.ShapeDtypeStruct((batch, heads, seq, HEAD_DIM), F32))
    return pl.pallas_call(
        functools.partial(_ctx_attn_kernel, group=group, normed=normed),
        grid=(batch, heads // hps),
        in_specs=in_specs, out_specs=out_specs, out_shape=out_shape,
        compiler_params=_params(2),
        name="ctx_gqa" if normed else "ctx_na",
    )(*args)


def _log_sigmoid(x):
    return -jnp.log1p(jnp.exp(-x))


def _retention_kernel(*refs, seq, tq, with_state_in, with_state_out):
    q_ref, k_ref, v_ref, g_ref, df_ref, db_ref = refs[:6]
    refs = refs[6:]
    if with_state_in:
        sf_ref, sb_ref = refs[:2]
        refs = refs[2:]
    o_ref = refs[0]
    decay_ref = refs[-1]
    chunk = tq
    n_chunks = seq // chunk
    il = lax.broadcasted_iota(jnp.int32, (chunk, 1), 0).astype(F32)
    for hh in range(decay_ref.shape[0]):
        hs = slice(hh * HEAD_DIM, (hh + 1) * HEAD_DIM)
        lgf = _log_sigmoid(df_ref[hh])
        lgb = _log_sigmoid(db_ref[hh])

        @pl.when(pl.program_id(1) == 0)
        def _():
            i = lax.broadcasted_iota(jnp.int32, (chunk, chunk), 0)
            j = lax.broadcasted_iota(jnp.int32, (chunk, chunk), 1)
            dist = (i - j).astype(F32)
            decay = jnp.where(dist == 0.0, 2.0, jnp.exp(jnp.where(dist > 0.0, lgf * dist, -lgb * dist)))
            decay_ref[hh] = decay * ATTN_SCALE

        k_dec_f = jnp.exp(lgf * (chunk - 1.0 - il)) * ATTN_SCALE
        k_dec_b = jnp.exp(lgb * il) * ATTN_SCALE
        add_f, add_b = [], []
        for c in range(n_chunks):
            cs = slice(c * chunk, (c + 1) * chunk)
            kf = k_ref[cs, hs].astype(F32)
            vb = v_ref[cs, hs].astype(BF16)
            add_f.append(_dot((kf * k_dec_f).T.astype(BF16), vb))
            add_b.append(_dot((kf * k_dec_b).T.astype(BF16), vb))
        state_f, state_b = [None] * n_chunks, [None] * n_chunks
        carry_f = jnp.exp(lgf * chunk)
        carry_b = jnp.exp(lgb * chunk)
        s = sf_ref[hh] if with_state_in else None
        for c in range(n_chunks):
            state_f[c] = s
            s = add_f[c] if s is None else carry_f * s + add_f[c]
        final_f = s
        s = sb_ref[hh] if with_state_in else None
        for c in reversed(range(n_chunks)):
            state_b[c] = s
            s = add_b[c] if s is None else carry_b * s + add_b[c]
        final_b = s

        q_dec_f = jnp.exp(lgf * (il + 1.0))
        q_dec_b = jnp.exp(lgb * (chunk - il))
        for c in range(n_chunks):
            cs = slice(c * chunk, (c + 1) * chunk)
            q = q_ref[cs, hs]
            att = _dot(q.astype(BF16), k_ref[cs, hs].astype(BF16), _NT) * decay_ref[hh]
            o = _dot(att.astype(BF16), v_ref[cs, hs].astype(BF16))
            qf = q.astype(F32)
            if state_f[c] is not None:
                o += _dot((qf * q_dec_f).astype(BF16), state_f[c].astype(BF16))
            if state_b[c] is not None:
                o += _dot((qf * q_dec_b).astype(BF16), state_b[c].astype(BF16))
            o_ref[cs, hs] = (_ln(o) * _silu(g_ref[cs, hs].astype(F32))).astype(o_ref.dtype)
        if with_state_out:
            nsf_ref, nsb_ref = refs[1:3]
            nsf_ref[hh] = final_f
            nsb_ref[hh] = final_b


def _retention(z, batch, seq, heads, col0, dec_f, dec_b, states=None, state_out=False):
    chunk = min(512, seq)
    hps = heads if seq == chunk else 1
    assert col0 % hps == 0 and seq % chunk == 0
    w = hps * HEAD_DIM
    c0 = col0 // hps
    nh = heads // hps
    in_specs = [pl.BlockSpec((seq, w), lambda h, b: (b, c0 + h)),
                pl.BlockSpec((seq, w), lambda h, b: (b, c0 + nh + h)),
                pl.BlockSpec((seq, w), lambda h, b: (b, c0 + 2 * nh + h)),
                pl.BlockSpec((seq, w), lambda h, b: (b, c0 + 3 * nh + h)),
                pl.BlockSpec((hps, 1, 1), lambda h, b: (h, 0, 0)),
                pl.BlockSpec((hps, 1, 1), lambda h, b: (h, 0, 0))]
    args = [z, z, z, z, dec_f.reshape(heads, 1, 1), dec_b.reshape(heads, 1, 1)]
    if states is not None:
        (sf, sb), layer = states
        spec = pl.BlockSpec((None, None, hps, HEAD_DIM, HEAD_DIM), lambda h, b: (b, layer, h, 0, 0))
        in_specs += [spec, spec]
        args += [sf, sb]
    out_specs = [pl.BlockSpec((seq, w), lambda h, b: (b, h))]
    out_shape = [jax.ShapeDtypeStruct((batch * seq, heads * HEAD_DIM), BF16)]
    if state_out:
        spec = pl.BlockSpec((None, hps, HEAD_DIM, HEAD_DIM), lambda h, b: (b, h, 0, 0))
        out_specs += [spec, spec]
        out_shape += [jax.ShapeDtypeStruct((batch, heads, HEAD_DIM, HEAD_DIM), F32)] * 2
    return pl.pallas_call(
        functools.partial(_retention_kernel, seq=seq, tq=chunk, with_state_in=states is not None,
                          with_state_out=state_out),
        grid=(nh, batch),
        in_specs=in_specs, out_specs=out_specs, out_shape=out_shape,
        scratch_shapes=[pltpu.VMEM((hps, chunk, chunk), F32)],
        compiler_params=_params(2),
        name="retention",
    )(*args)


def _with_ones(v):
    return jnp.concatenate([v, jnp.ones(v.shape, v.dtype)], axis=1)


def _exp2_pv(scores, values_with_ones):
    m = functools.reduce(jnp.maximum, [jnp.max(s, axis=-1, keepdims=True) for s in scores])
    r = functools.reduce(lambda a, b: a + b,
                         [_dot(jnp.exp2(s - m).astype(BF16), v) for s, v in zip(scores, values_with_ones)])
    return r[:, :HEAD_DIM] / r[:, HEAD_DIM:HEAD_DIM + 1]


def _na_plan(rows):
    kr = min(NA_WIN_R, rows)
    nq = min(NA_BLOCK_ROWS, rows)
    nk = min(nq + kr - 1, rows)
    assert rows % nq == 0
    blocks, cases = [], []
    for r0 in range(0, rows, nq):
        start = int(np.clip(r0 - kr // 2, 0, rows - nk))
        win = [int(np.clip(r - kr // 2, 0, rows - kr)) for r in range(r0, r0 + nq)]
        assert all(start <= w and w + kr <= start + nk for w in win)
        key = (start - r0,) + tuple(w - start for w in win)
        if key not in cases:
            cases.append(key)
        blocks.append((r0, start, cases.index(key)))
    return kr, nq, nk, blocks, cases


def _na_kernel(q_ref, k_ref, v_ref, ck_ref, cv_ref, bias_ref, o_ref, vx_ref, cvx_ref, *, blocks, nq, nk):
    vx_ref[...] = _with_ones(v_ref[...])
    cvx_ref[...] = _with_ones(cv_ref[...].astype(BF16))
    ck = ck_ref[...].astype(BF16)
    c = ATTN_SCALE * LOG2E
    for r0, start, case in blocks:
        qs = slice(r0 * GRID_W, (r0 + nq) * GRID_W)
        ks = slice(start * GRID_W, (start + nk) * GRID_W)
        q = q_ref[qs, :]
        s_loc = _dot(q, k_ref[ks, :], _NT) * c + bias_ref[case]
        s_ctx = _dot(q, ck, _NT) * c
        o_ref[qs, :] = _exp2_pv([s_loc, s_ctx], [vx_ref[ks, :], cvx_ref[...]]).astype(o_ref.dtype)


def _na_bias(rpb, rows):
    kr, nq, nk, _, cases = _na_plan(rows)
    dr = np.zeros((len(cases), nq, nk), np.int32)
    row_ok = np.zeros((len(cases), nq, nk), bool)
    for c, key in enumerate(cases):
        rel, offs = key[0], key[1:]
        for ri in range(nq):
            for ju in range(nk):
                row_ok[c, ri, ju] = 0 <= ju - offs[ri] < kr
                dr[c, ri, ju] = np.clip(rel + ju - ri + NA_WIN_R - 1, 0, 2 * NA_WIN_R - 2)
    cq = np.arange(GRID_W)
    ck = np.arange(GRID_W)
    col_start = np.clip(cq - NA_WIN_C // 2, 0, GRID_W - NA_WIN_C)
    col_ok = (ck[None, :] >= col_start[:, None]) & (ck[None, :] < col_start[:, None] + NA_WIN_C)
    dc = np.clip(ck[None, :] - cq[:, None] + (NA_WIN_C - 1), 0, 2 * NA_WIN_C - 2)
    pick_col = (dc.reshape(-1)[None, :] == np.arange(2 * NA_WIN_C - 1)[:, None]).astype(np.float32)
    n_l, n_h = rpb.shape[:2]
    t = jnp.take(rpb.astype(F32), dr.reshape(-1), axis=2)
    t = jnp.einsum("lhxb,bn->lhxn", t, pick_col, precision=lax.Precision.HIGHEST)
    t = t.reshape(n_l, n_h, len(cases), nq, nk, GRID_W, GRID_W).transpose(0, 1, 2, 3, 5, 4, 6)
    ok = row_ok[:, :, None, :, None] & col_ok[None, None, :, None, :]
    t = jnp.where(ok, t * LOG2E, NEG_INF)
    return t.reshape(n_l, n_h, len(cases), nq * GRID_W, nk * GRID_W)


def _na_attention(z, batch, seq, heads, cache_k, cache_v, layer, bias):
    _, nq, nk, blocks, cases = _na_plan(seq // GRID_W)
    past = cache_k.shape[3]
    cspec = pl.BlockSpec((None, None, None, past, HEAD_DIM), lambda b, h: (b, layer, h, 0, 0))
    return pl.pallas_call(
        functools.partial(_na_kernel, blocks=blocks, nq=nq, nk=nk),
        grid=(batch, heads),
        in_specs=[pl.BlockSpec((seq, HEAD_DIM), lambda b, h: (b, h)),
                  pl.BlockSpec((seq, HEAD_DIM), lambda b, h: (b, heads + h)),
                  pl.BlockSpec((seq, HEAD_DIM), lambda b, h: (b, 2 * heads + h)),
                  cspec, cspec,
                  pl.BlockSpec((None, None, len(cases), nq * GRID_W, nk * GRID_W),
                               lambda b, h: (layer, h, 0, 0, 0))],
        out_specs=pl.BlockSpec((seq, HEAD_DIM), lambda b, h: (b, h)),
        out_shape=jax.ShapeDtypeStruct((batch * seq, heads * HEAD_DIM), BF16),
        scratch_shapes=[pltpu.VMEM((seq, 2 * HEAD_DIM), BF16), pltpu.VMEM((past, 2 * HEAD_DIM), BF16)],
        compiler_params=_params(2),
        name="latent_na",
    )(z, z, z, cache_k, cache_v, bias)


def _gqa_kernel(q_ref, k_ref, v_ref, ck_ref, cv_ref, qg_ref, kg_ref, cos_ref, sin_ref, o_ref, kf_ref, vf_ref,
                *, group, past, seq, tq):
    i = pl.program_id(2)

    @pl.when(i == 0)
    def _():
        kf_ref[0:past, :] = ck_ref[...].astype(BF16)
        vf_ref[0:past, :] = _with_ones(cv_ref[...].astype(BF16))
        k = _rope(_rms(k_ref[...].astype(F32), kg_ref[...]), cos_ref[...], sin_ref[...])
        kf_ref[past:past + seq, :] = k.astype(BF16)
        vf_ref[past:past + seq, :] = _with_ones(v_ref[...])

    t0 = pl.multiple_of(i * tq, tq)
    cos = cos_ref[pl.ds(t0, tq), :]
    sin = sin_ref[pl.ds(t0, tq), :]
    kf = kf_ref[...]
    vf = vf_ref[...]
    def scores(g):
        q = q_ref[:, g * HEAD_DIM:(g + 1) * HEAD_DIM].astype(F32)
        q = _rope(_rms(q, qg_ref[...]), cos, sin) * (ATTN_SCALE * LOG2E)
        return _dot(q.astype(BF16), kf, _NT)

    s = scores(0)
    for g in range(group):
        s_next = scores(g + 1) if g + 1 < group else None
        o_ref[:, g * HEAD_DIM:(g + 1) * HEAD_DIM] = _exp2_pv([s], [vf]).astype(o_ref.dtype)
        s = s_next


def _rope_tables(seq):
    t = np.arange(seq)
    half = HEAD_DIM // 2
    inv = 1.0 / (ROPE_THETA ** (np.arange(0, half, 2, dtype=np.float32) / half))
    row = (t // GRID_W).astype(np.float32)
    col = (t % GRID_W).astype(np.float32)
    ang = jnp.concatenate([jnp.asarray(row[:, None] * inv)] * 2 + [jnp.asarray(col[:, None] * inv)] * 2, axis=-1)
    sign = np.where((np.arange(HEAD_DIM) % half) < half // 2, -1.0, 1.0).astype(np.float32)
    return jnp.cos(ang), jnp.sin(ang) * sign


def _gqa_attention(z, batch, seq, kv_heads, group, q_col, k_col, v_col, cache_k, cache_v, layer, gains, tables):
    past = cache_k.shape[3]
    tq = min(512, seq)
    nq = seq // tq
    qw = group * HEAD_DIM
    cspec = pl.BlockSpec((None, None, None, past, HEAD_DIM), lambda b, h, i: (b, layer, h, 0, 0))
    gspec = pl.BlockSpec((1, HEAD_DIM), lambda b, h, i: (0, 0))
    tspec = pl.BlockSpec((seq, HEAD_DIM), lambda b, h, i: (0, 0))
    return pl.pallas_call(
        functools.partial(_gqa_kernel, group=group, past=past, seq=seq, tq=tq),
        grid=(batch, kv_heads, nq),
        in_specs=[pl.BlockSpec((tq, qw), lambda b, h, i: (b * nq + i, q_col // group + h)),
                  pl.BlockSpec((seq, HEAD_DIM), lambda b, h, i: (b, k_col + h)),
                  pl.BlockSpec((seq, HEAD_DIM), lambda b, h, i: (b, v_col + h)),
                  cspec, cspec, gspec, gspec, tspec, tspec],
        out_specs=pl.BlockSpec((tq, qw), lambda b, h, i: (b * nq + i, h)),
        out_shape=jax.ShapeDtypeStruct((batch * seq, kv_heads * qw), BF16),
        scratch_shapes=[pltpu.VMEM((past + seq, HEAD_DIM), BF16), pltpu.VMEM((past + seq, 2 * HEAD_DIM), BF16)],
        compiler_params=_params(3),
        name="latent_gqa",
    )(z, z, z, cache_k, cache_v, gains[0], gains[1], tables[0], tables[1])


def _outproj_kernel(x_ref, na_ref, gq_ref, rt_ref, w_ref, g1_ref, sh2_ref, sc2_ref, lg_ref, lb_ref, wr_ref,
                    x1_ref, h2_ref, lt_ref, *, alpha, sub):
    n_exp = lt_ref.shape[1]
    for r in range(x_ref.shape[0] // sub):
        rs = slice(r * sub, (r + 1) * sub)
        mix = jnp.concatenate([na_ref[rs, :], gq_ref[rs, :], rt_ref[rs, :]], axis=1)
        y = _dot(mix, w_ref[...])
        x1 = _ln(alpha * x_ref[rs, :] + g1_ref[...] * y) * lg_ref[...] + lb_ref[...]
        x1_ref[rs, :] = x1
        h2 = _ln(x1) * (1.0 + sc2_ref[...]) + sh2_ref[...]
        hi, lo = _split_bf16(h2)
        h2_ref[rs, :] = hi
        both = _dot(hi, wr_ref[...])
        lt_ref[rs, :] = both[:, :n_exp] + both[:, n_exp:] + _dot(lo, wr_ref[:, 0:n_exp])


def _out_projection(x, na_o, gq_o, rt_o, w_bf16, layer, mod3, ln_g, ln_b, wr_hl, tokens_per_batch, per_request,
                    alpha):
    rows, d = x.shape
    tm = 512
    per = tokens_per_batch // tm if per_request else 1
    n_exp = wr_hl.shape[1] // 2

    def mod_spec(chunk):
        return pl.BlockSpec((None, 1, d), lambda i: (_mod_row(i, per, per_request) * 6 + chunk, 0, 0))

    def row_spec(width):
        return pl.BlockSpec((tm, width), lambda i: (i, 0))

    vec = pl.BlockSpec((1, d), lambda i: (0, 0))
    return pl.pallas_call(
        functools.partial(_outproj_kernel, alpha=alpha, sub=256),
        grid=(rows // tm,),
        in_specs=[row_spec(d), row_spec(na_o.shape[1]), row_spec(gq_o.shape[1]), row_spec(rt_o.shape[1]),
                  pl.BlockSpec((None,) + w_bf16.shape[1:], lambda i: (layer, 0, 0)),
                  mod_spec(2), mod_spec(3), mod_spec(4), vec, vec,
                  pl.BlockSpec(wr_hl.shape, lambda i: (0, 0))],
        out_specs=[row_spec(d), row_spec(d), row_spec(n_exp)],
        out_shape=[jax.ShapeDtypeStruct((rows, d), F32), jax.ShapeDtypeStruct((rows, d), BF16),
                   jax.ShapeDtypeStruct((rows, n_exp), F32)],
        compiler_params=_params(1),
        name="outproj_postnorm_router",
    )(x, na_o, gq_o, rt_o, w_bf16, mod3, mod3, mod3, ln_g, ln_b, wr_hl)


def _topk_kernel(lt_ref, slot_ref, gate_ref, before_ref, *, cap, seq, group):
    affs = []
    for g in range(group):
        logits = lt_ref[:, g * seq:(g + 1) * seq]
        m = jnp.max(logits, axis=0, keepdims=True)
        ex = jnp.exp(logits - m)
        affs.append(ex / jnp.sum(ex, axis=0, keepdims=True))
    aff = jnp.concatenate(affs, axis=0)
    n_exp = aff.shape[0]
    bits = lax.bitcast_convert_type(aff, jnp.int32)

    def count(mask):
        return jnp.sum(jnp.where(mask, 1.0, 0.0), axis=1, keepdims=True)

    def value_step(it, thr):
        cand = thr | jnp.left_shift(jnp.int32(1), 30 - it)
        return jnp.where(count(bits >= cand) >= cap, cand, thr)

    thr = lax.fori_loop(0, 31, value_step, jnp.zeros((n_exp, 1), jnp.int32))
    above = bits > thr
    tied = bits == thr
    need = cap - count(above)
    tok = lax.broadcasted_iota(jnp.int32, (n_exp, seq), 1)
    n_bits = int(seq - 1).bit_length()

    def index_step(it, bound):
        cand = bound | jnp.left_shift(jnp.int32(1), n_bits - 1 - it)
        return jnp.where(count(tied & (tok < cand)) < need, cand, bound)

    bound = lax.fori_loop(0, n_bits, index_step, jnp.zeros((n_exp, 1), jnp.int32))
    sel = above | (tied & (tok <= bound))
    self = jnp.where(sel, 1.0, 0.0)
    lanes = 128
    upper = jnp.where(lax.broadcasted_iota(jnp.int32, (lanes, lanes), 0)
                      < lax.broadcasted_iota(jnp.int32, (lanes, lanes), 1), 1.0, 0.0).astype(BF16)
    running = jnp.zeros((n_exp, 1), F32)
    lane = lax.broadcasted_iota(jnp.int32, (n_exp, lanes), 1)
    before = jnp.zeros((n_exp, lanes), F32)
    for blk in range(seq // lanes):
        sl = slice(blk * lanes, (blk + 1) * lanes)
        chunk = self[:, sl]
        before = jnp.where(lane == blk, running, before)
        pos = _dot(chunk.astype(BF16), upper) + running
        slot_ref[:, sl] = jnp.where(chunk > 0.0, pos.astype(jnp.int32), -1)
        running = running + jnp.sum(chunk, axis=1, keepdims=True)
    gate_ref[...] = aff
    before_ref[...] = before.astype(jnp.int32)


def _route(logits_t, batch, seq, cap):
    n_exp = logits_t.shape[0]
    assert seq // 128 <= 128
    group = math.gcd(batch, ROUTE_GROUP)
    rows = group * n_exp
    spec = pl.BlockSpec((rows, seq), lambda i: (i, 0))
    slot, gate, before = pl.pallas_call(
        functools.partial(_topk_kernel, cap=cap, seq=seq, group=group),
        grid=(batch // group,),
        in_specs=[pl.BlockSpec((n_exp, group * seq), lambda i: (0, i))],
        out_specs=[spec, spec, pl.BlockSpec((rows, 128), lambda i: (i, 0))],
        out_shape=[jax.ShapeDtypeStruct((batch * n_exp, seq), jnp.int32),
                   jax.ShapeDtypeStruct((batch * n_exp, seq), F32),
                   jax.ShapeDtypeStruct((batch * n_exp, 128), jnp.int32)],
        compiler_params=_params(1),
        name="route_topk",
    )(logits_t)
    return (slot.reshape(batch, n_exp, seq), gate.reshape(batch, n_exp, seq), before.reshape(batch, n_exp, 128))


def _window_plan(before, cap, seq):
    per = MOE_CHUNK // 128
    lo = before[:, :, 0:seq // 128:per]
    hi = jnp.concatenate([lo[:, :, 1:], jnp.full(lo.shape[:2] + (1,), cap, jnp.int32)], axis=2)
    base = jnp.minimum(lo // 16 * 16, cap - MOE_WINDOW)
    overflow = jnp.any(hi - base > MOE_WINDOW, axis=1)
    return base.transpose(0, 2, 1).reshape(-1), overflow.astype(jnp.int32).reshape(-1)


def _gather_kernel(slot_ref, gate_ref, h_ref, xs_ref, gs_ref, *, n_inner, cap):
    h = h_ref[...]
    seq = h.shape[0]
    row = lax.broadcasted_iota(jnp.int32, (cap, seq), 0)
    for e in range(n_inner):
        onehot = slot_ref[e:e + 1, :] == row
        xs_ref[e] = _dot(jnp.where(onehot, 1.0, 0.0).astype(BF16), h).astype(BF16)
        gs_ref[e] = jnp.sum(jnp.where(onehot, gate_ref[e:e + 1, :], 0.0), axis=1, keepdims=True)


def _gather(slot, gate, h, batch, seq, cap, n_inner):
    n_exp = slot.shape[1]
    d = h.shape[1]
    n_outer = n_exp // n_inner
    slot4 = slot.reshape(batch, n_outer, n_inner, seq)
    gate4 = gate.reshape(batch, n_outer, n_inner, seq)
    sspec = pl.BlockSpec((None, None, n_inner, seq), lambda b, e: (b, e, 0, 0))
    return pl.pallas_call(
        functools.partial(_gather_kernel, n_inner=n_inner, cap=cap),
        grid=(batch, n_outer),
        in_specs=[sspec, sspec, pl.BlockSpec((seq, d), lambda b, e: (b, 0))],
        out_specs=[pl.BlockSpec((n_inner, cap, d), lambda b, e: (e, b, 0)),
                   pl.BlockSpec((n_inner, cap, 1), lambda b, e: (e, b, 0))],
        out_shape=[jax.ShapeDtypeStruct((n_exp, batch * cap, d), BF16),
                   jax.ShapeDtypeStruct((n_exp, batch * cap, 1), F32)],
        compiler_params=_params(2),
        name="moe_gather",
    )(slot4, gate4, h)


def _gather_win_kernel(base_ref, flag_ref, slot_ref, gate_ref, h_ref, xs_ref, gs_ref, *, cap, cps):
    k = pl.program_id(1)
    first = (pl.program_id(0) * pl.num_programs(1) + k) * cps
    n_exp = slot_ref.shape[0]
    chunk = MOE_CHUNK
    win = MOE_WINDOW

    @pl.when(k == 0)
    def _():
        xs_ref[...] = jnp.zeros(xs_ref.shape, xs_ref.dtype)
        gs_ref[...] = jnp.zeros(gs_ref.shape, gs_ref.dtype)

    for cc in range(cps):
        step = first + cc
        cols = slice(cc * chunk, (cc + 1) * chunk)

        @pl.when(flag_ref[step] == 0)
        def _():
            h = h_ref[cols, :]
            row = lax.broadcasted_iota(jnp.int32, (win, chunk), 0)
            bases = [pl.multiple_of(base_ref[step * n_exp + e], 16) for e in range(n_exp)]
            hots = [(slot_ref[e:e + 1, cols] - bases[e]) == row for e in range(n_exp)]
            stacked = jnp.concatenate([jnp.where(hot, 1.0, 0.0).astype(BF16) for hot in hots], axis=0)
            picked = _dot(stacked, h)
            for e in range(n_exp):
                rows = pl.ds(bases[e], win)
                xs_ref[e, rows, :] += picked[e * win:(e + 1) * win].astype(BF16)
                gs_ref[e, rows, :] += jnp.sum(jnp.where(hots[e], gate_ref[e:e + 1, cols], 0.0), axis=1,
                                              keepdims=True)

        @pl.when(flag_ref[step] != 0)
        def _():
            h = h_ref[cols, :]
            row = lax.broadcasted_iota(jnp.int32, (cap, chunk), 0)
            for e in range(n_exp):
                hot = slot_ref[e:e + 1, cols] == row
                xs_ref[e] += _dot(jnp.where(hot, 1.0, 0.0).astype(BF16), h).astype(BF16)
                gs_ref[e] += jnp.sum(jnp.where(hot, gate_ref[e:e + 1, cols], 0.0), axis=1, keepdims=True)


def _gather_windowed(slot, gate, h, base, flag, batch, seq, cap):
    n_exp = slot.shape[1]
    d = h.shape[1]
    cps = 2 if seq % (2 * MOE_CHUNK) == 0 else 1
    nk = seq // (MOE_CHUNK * cps)
    sspec = pl.BlockSpec((None, n_exp, MOE_CHUNK * cps), lambda b, k, *_: (b, 0, k))
    return pl.pallas_call(
        functools.partial(_gather_win_kernel, cap=cap, cps=cps),
        grid_spec=pltpu.PrefetchScalarGridSpec(
            num_scalar_prefetch=2,
            grid=(batch, nk),
            in_specs=[sspec, sspec, pl.BlockSpec((MOE_CHUNK * cps, d), lambda b, k, *_: (b * nk + k, 0))],
            out_specs=[pl.BlockSpec((n_exp, cap, d), lambda b, k, *_: (0, b, 0)),
                       pl.BlockSpec((n_exp, cap, 1), lambda b, k, *_: (0, b, 0))]),
        out_shape=[jax.ShapeDtypeStruct((n_exp, batch * cap, d), BF16),
                   jax.ShapeDtypeStruct((n_exp, batch * cap, 1), F32)],
        compiler_params=_params(2),
        name="moe_gather_windowed",
    )(base, flag, slot, gate, h)


def _ffn_kernel(xc_ref, gc_ref, xl_ref, gl_ref, wg_hbm, wu_hbm, wd_hbm, y_ref, wg_s, wu_s, wd_s, stg_g, stg_u,
                stg_d, sems, *, layer, n_chunks, ctx_steps):
    e = pl.program_id(0)
    m = pl.program_id(1)
    slot = e % 2
    rows_gu = wg_s.shape[1] // n_chunks
    rows_d = wd_s.shape[1] // n_chunks

    def chunk_copies(expert, c):
        gu = pl.ds(pl.multiple_of(c * rows_gu, rows_gu), rows_gu)
        dn = pl.ds(pl.multiple_of(c * rows_d, rows_d), rows_d)
        return (pltpu.make_async_copy(wg_hbm.at[layer, expert, gu, :], stg_g, sems.at[0]),
                pltpu.make_async_copy(wu_hbm.at[layer, expert, gu, :], stg_u, sems.at[1]),
                pltpu.make_async_copy(wd_hbm.at[layer, expert, dn, :], stg_d, sems.at[2]))

    def land(dst_slot, c, copies):
        for cp in copies:
            cp.wait()
        gu = pl.ds(pl.multiple_of(c * rows_gu, rows_gu), rows_gu)
        dn = pl.ds(pl.multiple_of(c * rows_d, rows_d), rows_d)
        wg_s[dst_slot, gu, :] = stg_g[...].astype(BF16)
        wu_s[dst_slot, gu, :] = stg_u[...].astype(BF16)
        wd_s[dst_slot, dn, :] = stg_d[...].astype(BF16)

    @pl.when((e == 0) & (m == 0))
    def _():
        for c in range(n_chunks):
            copies = chunk_copies(0, c)
            for cp in copies:
                cp.start()
            land(0, c, copies)

    prefetch = (m < n_chunks) & (e + 1 < pl.num_programs(0))

    @pl.when(prefetch)
    def _():
        for cp in chunk_copies(e + 1, m):
            cp.start()

    is_ctx = m < ctx_steps
    x = jnp.where(is_ctx, xc_ref[...], xl_ref[...])
    gate = jnp.where(is_ctx, gc_ref[...], gl_ref[...])
    a = _dot(x, wg_s[slot])
    u = _dot(x, wu_s[slot])
    y = _dot((_silu(a) * u).astype(BF16), wd_s[slot])
    y_ref[...] = (y * gate).astype(y_ref.dtype)

    @pl.when(prefetch)
    def _():
        land(1 - slot, m, chunk_copies(e + 1, m))


def _expert_ffn(xs_c, gs_c, xs_l, gs_l, wg, wu, wd, layer):
    n_exp, rows_c, d = xs_c.shape
    rows_l = xs_l.shape[1]
    ff = wg.shape[3]
    tm = math.gcd(math.gcd(rows_c, rows_l), 512)
    ctx_steps = rows_c // tm
    steps = ctx_steps + rows_l // tm
    n_chunks = 4 if steps >= 4 else (2 if steps >= 2 else 1)
    any_spec = pl.BlockSpec(memory_space=pl.ANY)

    def ctx_spec(width):
        return pl.BlockSpec((None, tm, width), lambda e, i: (e, jnp.minimum(i, ctx_steps - 1), 0))

    def lat_spec(width):
        return pl.BlockSpec((None, tm, width), lambda e, i: (e, jnp.maximum(i - ctx_steps, 0), 0))

    return pl.pallas_call(
        functools.partial(_ffn_kernel, layer=layer, n_chunks=n_chunks, ctx_steps=ctx_steps),
        grid=(n_exp, steps),
        in_specs=[ctx_spec(d), ctx_spec(1), lat_spec(d), lat_spec(1), any_spec, any_spec, any_spec],
        out_specs=pl.BlockSpec((None, tm, d), lambda e, i: (e, i, 0)),
        out_shape=jax.ShapeDtypeStruct((n_exp, rows_c + rows_l, d), BF16),
        scratch_shapes=[pltpu.VMEM((2, d, ff), BF16), pltpu.VMEM((2, d, ff), BF16), pltpu.VMEM((2, ff, d), BF16),
                        pltpu.VMEM((d // n_chunks, ff), F32), pltpu.VMEM((d // n_chunks, ff), F32),
                        pltpu.VMEM((ff // n_chunks, d), F32), pltpu.SemaphoreType.DMA((3,))],
        compiler_params=_params(2),
        name="moe_ffn",
    )(xs_c, gs_c, xs_l, gs_l, wg, wu, wd)


def _combine_kernel(x_ref, y_ref, slot_ref, g2_ref, lg_ref, lb_ref, o_ref, *, alpha, cap):
    n_exp = y_ref.shape[0]
    tt = x_ref.shape[0]
    slot = slot_ref[...]
    lanes = 128
    if cap < lanes and lanes % cap == 0 and n_exp % (lanes // cap) == 0:
        per = lanes // cap
        lane = lax.broadcasted_iota(jnp.int32, (tt, lanes), 1)
        hots = []
        for g in range(n_exp // per):
            hit = None
            for j in range(per):
                s = slot[:, g * per + j:g * per + j + 1]
                match = jnp.where(s >= 0, s + j * cap, -1) == lane
                hit = match if hit is None else hit | match
            hots.append(jnp.where(hit, 1.0, 0.0).astype(BF16))
        acc = _dot(jnp.concatenate(hots, axis=1), y_ref[...].reshape(n_exp * cap, y_ref.shape[2]))
    else:
        col = lax.broadcasted_iota(jnp.int32, (tt, cap), 1)
        acc = jnp.zeros(x_ref.shape, F32)
        for e in range(n_exp):
            onehot = jnp.where(slot[:, e:e + 1] == col, 1.0, 0.0).astype(BF16)
            acc += _dot(onehot, y_ref[e])
    o_ref[...] = _ln(alpha * x_ref[...] + g2_ref[...] * acc) * lg_ref[...] + lb_ref[...]


def _combine(x1, y, slot_t, mod3, ln_g, ln_b, batch, seq, cap, row_blk0, per_request, alpha):
    rows, d = x1.shape
    n_exp = y.shape[0]
    tt = min(256, seq)
    nt = seq // tt
    vec = pl.BlockSpec((1, d), lambda b, i: (0, 0))
    return pl.pallas_call(
        functools.partial(_combine_kernel, alpha=alpha, cap=cap),
        grid=(batch, nt),
        in_specs=[pl.BlockSpec((tt, d), lambda b, i: (b * nt + i, 0)),
                  pl.BlockSpec((n_exp, cap, d), lambda b, i: (0, row_blk0 + b, 0)),
                  pl.BlockSpec((None, tt, n_exp), lambda b, i: (b, i, 0)),
                  pl.BlockSpec((None, 1, d), lambda b, i: (_mod_row(b, 1, per_request) * 6 + 5, 0, 0)),
                  vec, vec],
        out_specs=pl.BlockSpec((tt, d), lambda b, i: (b * nt + i, 0)),
        out_shape=jax.ShapeDtypeStruct((rows, d), F32),
        compiler_params=_params(2),
        name="moe_combine_postnorm",
    )(x1, y, slot_t, mod3, ln_g, ln_b)


def _combine_win_kernel(base_ref, flag_ref, x_ref, y_ref, slot_ref, g2_ref, lg_ref, lb_ref, o_ref, acc_ref,
                        *, alpha, cap):
    step = pl.program_id(0) * pl.num_programs(1) + pl.program_id(1)
    n_exp = y_ref.shape[0]
    tt = x_ref.shape[0]
    win = MOE_WINDOW
    slot = slot_ref[...]

    @pl.when(flag_ref[step] == 0)
    def _():
        lane = lax.broadcasted_iota(jnp.int32, (tt, 2 * win), 1)
        hots, ys = [], []
        for pair in range(n_exp // 2):
            picks = []
            for half in range(2):
                e = 2 * pair + half
                base = pl.multiple_of(base_ref[step * n_exp + e], 16)
                rel = slot[:, e:e + 1] - base
                picks.append(jnp.where((rel >= 0) & (rel < win), rel + half * win, -1))
                ys.append(y_ref[e, pl.ds(base, win), :])
            hots.append(jnp.where((picks[0] == lane) | (picks[1] == lane), 1.0, 0.0).astype(BF16))
        acc_ref[...] = _dot(jnp.concatenate(hots, axis=1), jnp.concatenate(ys, axis=0))

    @pl.when(flag_ref[step] != 0)
    def _():
        col = lax.broadcasted_iota(jnp.int32, (tt, cap), 1)
        acc = jnp.zeros(x_ref.shape, F32)
        for e in range(n_exp):
            acc += _dot(jnp.where(slot[:, e:e + 1] == col, 1.0, 0.0).astype(BF16), y_ref[e])
        acc_ref[...] = acc

    o_ref[...] = _ln(alpha * x_ref[...] + g2_ref[...] * acc_ref[...]) * lg_ref[...] + lb_ref[...]


def _combine_windowed(x1, y, slot_t, base, flag, mod3, ln_g, ln_b, batch, seq, cap, row_blk0, alpha):
    rows, d = x1.shape
    n_exp = y.shape[0]
    tt = MOE_CHUNK
    nt = seq // tt
    assert 2 * MOE_WINDOW == 128 and n_exp % 2 == 0
    vec = pl.BlockSpec((1, d), lambda b, i, *_: (0, 0))
    return pl.pallas_call(
        functools.partial(_combine_win_kernel, alpha=alpha, cap=cap),
        grid_spec=pltpu.PrefetchScalarGridSpec(
            num_scalar_prefetch=2,
            grid=(batch, nt),
            in_specs=[pl.BlockSpec((tt, d), lambda b, i, *_: (b * nt + i, 0)),
                      pl.BlockSpec((n_exp, cap, d), lambda b, i, *_: (0, row_blk0 + b, 0)),
                      pl.BlockSpec((None, tt, n_exp), lambda b, i, *_: (b, i, 0)),
                      pl.BlockSpec((None, 1, d), lambda b, i, *_: (_mod_row(b, 1, True) * 6 + 5, 0, 0)),
                      vec, vec],
            out_specs=pl.BlockSpec((tt, d), lambda b, i, *_: (b * nt + i, 0)),
            scratch_shapes=[pltpu.VMEM((tt, d), F32)]),
        out_shape=jax.ShapeDtypeStruct((rows, d), F32),
        compiler_params=_params(2),
        name="moe_combine_windowed_postnorm",
    )(base, flag, x1, y, slot_t, mod3, ln_g, ln_b)


def kernel(x_prompt, x_sample, cache_na_k, cache_na_v, cache_gqa_k, cache_gqa_v, state_ret_fwd, state_ret_bwd,
           c, c_ctx, w_in, w_out, w_mod, b_mod, ln1_g, ln1_b, ln2_g, ln2_b, q_norm_g, k_norm_g, na_rpb,
           ret_decay_fwd, ret_decay_bwd, w_router, w_gate, w_up, w_down):
    batch, seq, d = x_prompt.shape
    dbatch, dseq, _ = x_sample.shape
    depth = w_in.shape[0]
    na_heads = cache_na_k.shape[2]
    kv_heads = cache_gqa_k.shape[2]
    ret_heads = state_ret_fwd.shape[2]
    n_exp = w_router.shape[2]
    hd = HEAD_DIM
    gqa_heads = (w_in.shape[2] // hd - 3 * na_heads - 2 * kv_heads - 4 * ret_heads)
    group = gqa_heads // kv_heads
    alpha = float((2 * depth) ** 0.25)
    c_na = 0
    c_gq = 3 * na_heads
    c_gk = c_gq + gqa_heads
    c_gv = c_gk + kv_heads
    c_rt = c_gv + kv_heads
    cap_c = EC_CAPACITY_FACTOR * seq // n_exp
    cap_l = EC_CAPACITY_FACTOR * dseq // n_exp
    assert dbatch + 1 <= MOD_ROWS and (batch * cap_c) % cap_l == 0
    lat_blk0 = batch * cap_c // cap_l

    cond = jnp.zeros((MOD_ROWS, d), F32).at[0].set(c_ctx).at[1:1 + dbatch].set(c)
    mod_all = _modulation(cond, w_mod, b_mod)
    tables = _rope_tables(dseq)
    bias = _na_bias(na_rpb, dseq // GRID_W)

    w_in_b, w_out_b = w_in.astype(BF16), w_out.astype(BF16)

    xp = x_prompt.reshape(batch * seq, d)
    xs = x_sample.reshape(dbatch * dseq, d)
    outs = [[] for _ in range(6)]
    for l in range(depth):
        mod3 = mod_all[l].reshape(MOD_ROWS * 6, 1, d)
        wr_t = jnp.concatenate(_split_bf16(w_router[l]), axis=1)
        gains = (q_norm_g[l].reshape(1, hd), k_norm_g[l].reshape(1, hd))
        l1g, l1b = ln1_g[l].reshape(1, d), ln1_b[l].reshape(1, d)
        l2g, l2b = ln2_g[l].reshape(1, d), ln2_b[l].reshape(1, d)

        zc = _in_projection(xp, mod3, w_in_b, l, seq, False, F32)
        (na_c,) = _ctx_attention(zc, batch, seq, na_heads, 1, c_na, c_na + na_heads, c_na + 2 * na_heads)
        gq_c, gk_n = _ctx_attention(zc, batch, seq, kv_heads, group, c_gq, c_gk, c_gv, gains)
        rt_c, s_f, s_b = _retention(zc, batch, seq, ret_heads, c_rt, ret_decay_fwd[l], ret_decay_bwd[l],
                                    state_out=True)

        def heads_of(col, n):
            return zc[:, col * hd:(col + n) * hd].reshape(batch, seq, n, hd).transpose(0, 2, 1, 3)

        outs[0].append(heads_of(c_na + na_heads, na_heads))
        outs[1].append(heads_of(c_na + 2 * na_heads, na_heads))
        outs[2].append(gk_n)
        outs[3].append(heads_of(c_gv, kv_heads))
        outs[4].append(s_f)
        outs[5].append(s_b)

        zl = _in_projection(xs, mod3, w_in_b, l, dseq, True, BF16)
        (na_l,) = [_na_attention(zl, dbatch, dseq, na_heads, cache_na_k, cache_na_v, l, bias)]
        gq_l = _gqa_attention(zl, dbatch, dseq, kv_heads, group, c_gq, c_gk, c_gv, cache_gqa_k, cache_gqa_v, l,
                              gains, tables)
        (rt_l,) = _retention(zl, dbatch, dseq, ret_heads, c_rt, ret_decay_fwd[l], ret_decay_bwd[l],
                             states=((state_ret_fwd, state_ret_bwd), l))

        xp1, hp, lt_c = _out_projection(xp, na_c, gq_c, rt_c, w_out_b, l, mod3, l1g, l1b, wr_t, seq, False, alpha)
        xs1, hs, lt_l = _out_projection(xs, na_l, gq_l, rt_l, w_out_b, l, mod3, l1g, l1b, wr_t, dseq, True, alpha)

        slot_c, gate_c, _ = _route(lt_c.T, batch, seq, cap_c)
        slot_l, gate_l, before_l = _route(lt_l.T, dbatch, dseq, cap_l)
        windowed = cap_l > MOE_WINDOW and dseq % MOE_CHUNK == 0
        if windowed:
            base_l, flag_l = _window_plan(before_l, cap_l, dseq)
            rows_l, gates_l = _gather_windowed(slot_l, gate_l, hs, base_l, flag_l, dbatch, dseq, cap_l)
        else:
            rows_l, gates_l = _gather(slot_l, gate_l, hs, dbatch, dseq, cap_l, 2)
        rows_c, gates_c = _gather(slot_c, gate_c, hp, batch, seq, cap_c, n_exp)
        y = _expert_ffn(rows_c, gates_c, rows_l, gates_l, w_gate, w_up, w_down, l)
        xp = _combine(xp1, y, slot_c.transpose(0, 2, 1), mod3, l2g, l2b, batch, seq, cap_c, 0, False, alpha)
        if windowed:
            xs = _combine_windowed(xs1, y, slot_l.transpose(0, 2, 1), base_l, flag_l, mod3, l2g, l2b, dbatch, dseq,
                                   cap_l, lat_blk0, alpha)
        else:
            xs = _combine(xs1, y, slot_l.transpose(0, 2, 1), mod3, l2g, l2b, dbatch, dseq, cap_l, lat_blk0, True,
                          alpha)

    stacked = [jnp.stack(o, axis=1) for o in outs]
    return (xp.reshape(batch, seq, d), xs.reshape(dbatch, dseq, d), *stacked)
```
